```python
import math
import jax, jax.numpy as jnp
from jax import lax
import numpy as np

D_MODEL = 1024
BATCH = 4
SEQ = 4096
DEPTH = 2

PLE_DIM = 256

A_WIDTH = D_MODEL // 4
A_DK = 64
A_DV = 64
A_HEADS = A_WIDTH // A_DV
A_CHUNK = 16

B_WIDTH = D_MODEL // 4
B_CH = 64
B_GROUPS = B_WIDTH // B_CH
B_CHUNK = 128

C_WIDTH = D_MODEL // 2
C_NOPE = 64
C_ROPE = 32
C_V = 64
C_HEADS = C_WIDTH // C_V
C_Q_RANK = 384
C_KV_RANK = 256
Q_BLOCK = 128
ROPE_THETA = 10000.0

MIX_WIDTH = A_WIDTH + B_WIDTH + C_WIDTH
IN_SIZES = (A_WIDTH, A_WIDTH, A_WIDTH, A_WIDTH, B_WIDTH, B_WIDTH, C_Q_RANK, C_KV_RANK, C_ROPE)
IN_COLS = sum(IN_SIZES)
IN_SPLITS = tuple(int(s) for s in np.cumsum(IN_SIZES)[:-1])

D_FF = int(math.ceil(8 * D_MODEL / 3 / 256)) * 256
LN_EPS = 1e-5
RMS_EPS = 1e-6
DEEPNORM_ALPHA = (2 * DEPTH) ** 0.25
DEEPNORM_BETA = (8 * DEPTH) ** -0.25

kernel_name = "hybrid_hgrn2_sgu_mla_deepnorm"


def layer_norm(x, g, b):
    xf = x.astype(jnp.float32)
    mu = jnp.mean(xf, -1, keepdims=True)
    var = jnp.mean(jnp.square(xf - mu), -1, keepdims=True)
    return ((xf - mu) * lax.rsqrt(var + LN_EPS)).astype(x.dtype) * g + b


def rms_norm(x, g):
    xf = x.astype(jnp.float32)
    y = xf * lax.rsqrt(jnp.mean(xf * xf, -1, keepdims=True) + RMS_EPS)
    return y.astype(x.dtype) * g


def rope(x, cos, sin):
    x1, x2 = jnp.split(x, 2, axis=-1)
    return jnp.concatenate([x1 * cos - x2 * sin, x2 * cos + x1 * sin], axis=-1).astype(x.dtype)


def hgrn2_mixer(q, f_logit, i_in, g, lb, norm_g):
    bsz, s, _ = q.shape
    n = s // A_CHUNK
    f32 = jnp.float32

    def heads(t):
        return t.astype(f32).reshape(bsz, n, A_CHUNK, A_HEADS, -1)

    lbf = lb.astype(f32)
    f = lbf + (1.0 - lbf) * jax.nn.sigmoid(f_logit.astype(f32))
    qh = heads(jax.nn.silu(q.astype(f32)))
    kh = heads(1.0 - f)
    vh = heads(i_in)
    bcum = jnp.cumsum(heads(jnp.log(f)), axis=2)

    causal = jnp.tril(jnp.ones((A_CHUNK, A_CHUNK), dtype=bool))[None, None, :, :, None, None]
    diff = bcum[:, :, :, None] - bcum[:, :, None, :]
    decay = jnp.exp(jnp.where(causal, diff, -jnp.inf))
    scores = jnp.einsum('bnthk,bnshk,bntshk->bnhts', qh, kh, decay)
    o_intra = jnp.einsum('bnhts,bnshv->bnthv', scores, vh)

    b_last = bcum[:, :, -1]
    k_to_end = kh * jnp.exp(b_last[:, :, None] - bcum)
    chunk_kv = jnp.einsum('bnshk,bnshv->bnhkv', k_to_end, vh)
    chunk_decay = jnp.exp(b_last)

    def step(state, xs):
        dec, kv = xs
        return dec[..., None] * state + kv, state

    init = jnp.zeros((bsz, A_HEADS, A_DK, A_DV), f32)
    _, prev_states = lax.scan(step, init, (jnp.moveaxis(chunk_decay, 1, 0), jnp.moveaxis(chunk_kv, 1, 0)))
    prev_states = jnp.moveaxis(prev_states, 0, 1)
    o_inter = jnp.einsum('bnthk,bnhkv->bnthv', qh * jnp.exp(bcum), prev_states)

    o = (o_intra + o_inter).reshape(bsz, s, A_HEADS, A_DV)
    o = o * lax.rsqrt(jnp.mean(o * o, -1, keepdims=True) + RMS_EPS)
    o = o.reshape(bsz, s, A_WIDTH) * norm_g.astype(f32) * jax.nn.silu(g.astype(f32))
    return o.astype(g.dtype)


def sgu_mixer(u, v, ln_g, ln_b, w_s, b_s):
    bsz, s, _ = u.shape
    n = s // B_CHUNK
    u = jax.nn.gelu(u, approximate=False)
    v = layer_norm(jax.nn.gelu(v, approximate=False), ln_g, ln_b)
    vh = v.reshape(bsz, n, B_CHUNK, B_GROUPS, B_CH)
    w = w_s * jnp.tril(jnp.ones((B_CHUNK, B_CHUNK), dtype=w_s.dtype))
    z = jnp.einsum('gts,bnsgc->bntgc', w, vh) + b_s.T[None, None, :, :, None]
    return u * z.reshape(bsz, s, B_WIDTH)


def mla_mixer(c_q, c_kv, k_rope_raw, cos, sin, q_norm_g, w_uq, kv_norm_g, w_ukv):
    bsz, s, _ = c_q.shape
    q = (rms_norm(c_q, q_norm_g) @ w_uq).reshape(bsz, s, C_HEADS, C_NOPE + C_ROPE)
    q_nope = q[..., :C_NOPE]
    q_rope = rope(q[..., C_NOPE:], cos[:, :, None], sin[:, :, None])
    kv = (rms_norm(c_kv, kv_norm_g) @ w_ukv).reshape(bsz, s, C_HEADS, C_NOPE + C_V)
    k_nope, v = kv[..., :C_NOPE], kv[..., C_NOPE:]
    k_rope = rope(k_rope_raw, cos, sin)

    nb = s // Q_BLOCK
    scale = (C_NOPE + C_ROPE) ** -0.5
    key_idx = jnp.arange(s)

    def blocks(t):
        return jnp.moveaxis(t.reshape(bsz, nb, Q_BLOCK, *t.shape[2:]), 1, 0)

    def attend(args):
        qn, qr, blk = args
        sc = jnp.einsum('bqhd,bkhd->bhqk', qn, k_nope) + jnp.einsum('bqhr,bkr->bhqk', qr, k_rope)
        sc = sc.astype(jnp.float32) * scale
        q_idx = blk * Q_BLOCK + jnp.arange(Q_BLOCK)
        sc = jnp.where(key_idx[None, :] <= q_idx[:, None], sc, -jnp.inf)
        pr = jax.nn.softmax(sc, axis=-1).astype(v.dtype)
        return jnp.einsum('bhqk,bkhv->bqhv', pr, v)

    out = lax.map(attend, (blocks(q_nope), blocks(q_rope), jnp.arange(nb)))
    return jnp.moveaxis(out, 0, 1).reshape(bsz, s, C_WIDTH)


def setup_inputs(seed: int = 0) -> dict:
    key = jax.random.key(seed)
    ks = jax.random.split(key, 32)
    f32 = jnp.float32

    def nrm(k, shape, scale):
        return jax.random.normal(k, shape, f32) * scale

    def gain(k, shape):
        return 1.0 + 0.02 * jax.random.normal(k, shape, f32)

    positions = (jnp.arange(SEQ, dtype=jnp.int32)[None, :]
                 + jax.random.randint(ks[2], (BATCH, 1), 0, 64, dtype=jnp.int32))
    return {
        "x": nrm(ks[0], (BATCH, SEQ, D_MODEL), 1.0),
        "p": nrm(ks[1], (DEPTH, BATCH, SEQ, PLE_DIM), 1.0),
        "positions": positions,
        "ln_in_g": gain(ks[3], (D_MODEL,)),
        "ln_in_b": nrm(ks[4], (D_MODEL,), 0.02),
        "w_in": nrm(ks[5], (DEPTH, D_MODEL, IN_COLS), D_MODEL ** -0.5),
        "hgrn_lb_logits": nrm(ks[6], (DEPTH, A_WIDTH), 0.5),
        "hgrn_norm_g": gain(ks[7], (DEPTH, A_WIDTH)),
        "sgu_ln_g": gain(ks[8], (DEPTH, B_WIDTH)),
        "sgu_ln_b": nrm(ks[9], (DEPTH, B_WIDTH), 0.02),
        "sgu_w_s": nrm(ks[10], (DEPTH, B_GROUPS, B_CHUNK, B_CHUNK), B_CHUNK ** -0.5),
        "sgu_b_s": gain(ks[11], (DEPTH, B_GROUPS, B_CHUNK)),
        "mla_q_norm_g": gain(ks[12], (DEPTH, C_Q_RANK)),
        "mla_w_uq": nrm(ks[13], (DEPTH, C_Q_RANK, C_HEADS * (C_NOPE + C_ROPE)), C_Q_RANK ** -0.5),
        "mla_kv_norm_g": gain(ks[14], (DEPTH, C_KV_RANK)),
        "mla_w_ukv": nrm(ks[15], (DEPTH, C_KV_RANK, C_HEADS * (C_NOPE + C_V)), C_KV_RANK ** -0.5),
        "w_out": nrm(ks[16], (DEPTH, MIX_WIDTH, D_MODEL), DEEPNORM_BETA * MIX_WIDTH ** -0.5),
        "ln1_g": gain(ks[17], (DEPTH, D_MODEL)),
        "ln1_b": nrm(ks[18], (DEPTH, D_MODEL), 0.02),
        "w_gate_up": nrm(ks[19], (DEPTH, D_MODEL, 2 * D_FF), D_MODEL ** -0.5),
        "w_down": nrm(ks[20], (DEPTH, D_FF, D_MODEL), DEEPNORM_BETA * D_FF ** -0.5),
        "ple_w_gate": nrm(ks[21], (DEPTH, D_MODEL, D_MODEL), D_MODEL ** -0.5),
        "ple_w_proj": nrm(ks[22], (DEPTH, PLE_DIM, D_MODEL), DEEPNORM_BETA * PLE_DIM ** -0.5),
        "ln2_g": gain(ks[23], (DEPTH, D_MODEL)),
        "ln2_b": nrm(ks[24], (DEPTH, D_MODEL), 0.02),
    }


def reference(x, p, positions, ln_in_g, ln_in_b, w_in, hgrn_lb_logits, hgrn_norm_g,
              sgu_ln_g, sgu_ln_b, sgu_w_s, sgu_b_s, mla_q_norm_g, mla_w_uq,
              mla_kv_norm_g, mla_w_ukv, w_out, ln1_g, ln1_b, w_gate_up, w_down,
              ple_w_gate, ple_w_proj, ln2_g, ln2_b):
    lb_cum = jnp.cumsum(jax.nn.softmax(hgrn_lb_logits.astype(jnp.float32), axis=0), axis=0)
    lower_bounds = lb_cum - lb_cum[0]

    inv_freq = ROPE_THETA ** (-jnp.arange(0, C_ROPE, 2, dtype=jnp.float32) / C_ROPE)
    ang = positions.astype(jnp.float32)[..., None] * inv_freq
    cos, sin = jnp.cos(ang), jnp.sin(ang)

    h = layer_norm(x, ln_in_g, ln_in_b)
    for i in range(DEPTH):
        proj = h @ w_in[i]
        a_q, a_f, a_i, a_g, b_u, b_v, c_q, c_kv, c_kr = jnp.split(proj, IN_SPLITS, axis=-1)
        o_a = hgrn2_mixer(a_q, a_f, a_i, a_g, lower_bounds[i], hgrn_norm_g[i])
        o_b = sgu_mixer(b_u, b_v, sgu_ln_g[i], sgu_ln_b[i], sgu_w_s[i], sgu_b_s[i])
        o_c = mla_mixer(c_q, c_kv, c_kr, cos, sin, mla_q_norm_g[i], mla_w_uq[i],
                        mla_kv_norm_g[i], mla_w_ukv[i])
        mix = jnp.concatenate([o_a, o_b, o_c], axis=-1) @ w_out[i]
        h = layer_norm(DEEPNORM_ALPHA * h + mix, ln1_g[i], ln1_b[i])

        gate, up = jnp.split(h @ w_gate_up[i], 2, axis=-1)
        ffn = (jax.nn.silu(gate) * up) @ w_down[i]
        ple = jax.nn.sigmoid(h @ ple_w_gate[i]) * (p[i] @ ple_w_proj[i])
        h = layer_norm(DEEPNORM_ALPHA * h + ffn + ple, ln2_g[i], ln2_b[i])
    return h
```

```python
import functools
import math

import numpy as np
import jax
import jax.numpy as jnp
from jax import lax
from jax.experimental import pallas as pl
from jax.experimental.pallas import tpu as pltpu

F32 = jnp.float32
BF16 = jnp.bfloat16

D_MODEL = 1024
DEPTH = 2
PLE_DIM = 256

A_WIDTH = 256
A_DK = 64
A_HEADS = 4
A_CHUNK = 16
B_WIDTH = 256
B_CH = 64
B_GROUPS = 4
B_CHUNK = 128
C_WIDTH = 512
C_NOPE = 64
C_ROPE = 32
C_V = 64
C_HEADS = 8
C_Q_RANK = 384
C_KV_RANK = 256
ROPE_THETA = 10000.0
HEAD_PAD = 128

D_FF = 2816
LN_EPS = 1e-5
RMS_EPS = 1e-6
DEEPNORM_ALPHA = (2 * DEPTH) ** 0.25
ATT_SCALE = (C_NOPE + C_ROPE) ** -0.5

MIX_BLOCK = 256
MIX_NCHUNK = MIX_BLOCK // A_CHUNK
ATT_TQ = 256
ATT_TK = 256
ROW_BLOCK = 512
FF_CHUNK = 256
VMEM_LIMIT = 56 * 1024 * 1024
NEG_BIG = -1e30


def _dot(a, b):
    return jnp.dot(a, b, preferred_element_type=F32)


def _dot_nt(a, b):
    return lax.dot_general(a, b, (((1,), (1,)), ((), ())), preferred_element_type=F32)


def _dot_tn(a, b):
    return lax.dot_general(a, b, (((0,), (0,)), ((), ())), preferred_element_type=F32)


def _layer_norm(x, g, b):
    mu = jnp.mean(x, axis=-1, keepdims=True)
    xc = x - mu
    var = jnp.mean(xc * xc, axis=-1, keepdims=True)
    return xc * lax.rsqrt(var + LN_EPS) * g + b


def _rms_norm(x, g):
    return x * lax.rsqrt(jnp.mean(x * x, axis=-1, keepdims=True) + RMS_EPS) * g


def _sigmoid(x):
    return 1.0 / (1.0 + jnp.exp(-x))


def _const_spec(shape):
    nd = len(shape)
    return pl.BlockSpec(shape, lambda *_: (0,) * nd)


def _params(sem):
    return pltpu.CompilerParams(dimension_semantics=sem, vmem_limit_bytes=VMEM_LIMIT)


def _rope_table_kernel(pos_ref, freq_ref, cos_ref, sin_ref):
    ang = pos_ref[...] * freq_ref[...]
    cos_ref[...] = jnp.cos(ang)
    sin_ref[...] = jnp.sin(ang)


def _rope_tables(positions):
    t = positions.size
    half = C_ROPE // 2
    per_row = 128 // half
    inv_freq = ROPE_THETA ** (-jnp.arange(0, C_ROPE, 2, dtype=F32) / C_ROPE)
    pos_rep = jnp.repeat(positions.astype(F32).reshape(t), half).reshape(t // per_row, 128)
    freq = jnp.tile(inv_freq, per_row).reshape(1, 128)
    rows = t // per_row
    cos, sin = pl.pallas_call(
        _rope_table_kernel,
        out_shape=(jax.ShapeDtypeStruct((rows, 128), F32),) * 2,
        grid=(1,),
        in_specs=[_const_spec((rows, 128)), _const_spec((1, 128))],
        out_specs=(_const_spec((rows, 128)),) * 2,
        name="rope_tables",
    )(pos_rep, freq)
    cos = cos.reshape(t, half)
    sin = sin.reshape(t, half)
    ones = jnp.ones((t, C_NOPE), F32)
    zeros = jnp.zeros((t, C_NOPE), F32)
    pad1 = jnp.ones((t, HEAD_PAD - C_NOPE - C_ROPE), F32)
    pad0 = jnp.zeros((t, HEAD_PAD - C_NOPE - C_ROPE), F32)
    cos_pat = jnp.concatenate([ones, cos, cos, pad1], axis=1)
    sin_pat = jnp.concatenate([zeros, sin, sin, pad0], axis=1)
    return cos_pat, sin_pat


def _ln_kernel(x_ref, g_ref, b_ref, o_ref):
    o_ref[...] = _layer_norm(x_ref[...], g_ref[...], b_ref[...])


def _ln_in(x2d, g, b):
    t, d = x2d.shape
    return pl.pallas_call(
        _ln_kernel,
        out_shape=jax.ShapeDtypeStruct((t, d), F32),
        grid=(t // ROW_BLOCK,),
        in_specs=[pl.BlockSpec((ROW_BLOCK, d), lambda i: (i, 0)), _const_spec((1, d)), _const_spec((1, d))],
        out_specs=pl.BlockSpec((ROW_BLOCK, d), lambda i: (i, 0)),
        compiler_params=_params(("parallel",)),
        name="ln_in",
    )(x2d, g.reshape(1, d), b.reshape(1, d))


def _inproj_kernel(h_ref, cos_ref, sin_ref, w_hg, w_sg, w_cq, w_ckv, w_krl, w_krr,
                   gq_ref, wq_l, wq_r, gkv_ref, wk_ref, wv_ref,
                   hg_out, sg_out, q_out, k_out, v_out):
    hb = h_ref[...].astype(BF16)
    hg_out[...] = _dot(hb, w_hg[...])
    sg_out[...] = _dot(hb, w_sg[...])
    cos = cos_ref[...]
    sin = sin_ref[...]

    cqn = _rms_norm(_dot(hb, w_cq[...]), gq_ref[...]).astype(BF16)
    q_lin = _dot(cqn, wq_l[...])
    q_rot = _dot(cqn, wq_r[...])
    for h in range(C_HEADS):
        sl = slice(h * HEAD_PAD, (h + 1) * HEAD_PAD)
        q_out[:, sl] = ((q_lin[:, sl] * cos + q_rot[:, sl] * sin) * ATT_SCALE).astype(BF16)

    ckvn = _rms_norm(_dot(hb, w_ckv[...]), gkv_ref[...]).astype(BF16)
    k_lin = _dot(ckvn, wk_ref[...])
    v_out[...] = _dot(ckvn, wv_ref[...]).astype(BF16)
    k_rope = _dot(hb, w_krl[...]) * cos + _dot(hb, w_krr[...]) * sin
    for h in range(C_HEADS):
        sl = slice(h * HEAD_PAD, (h + 1) * HEAD_PAD)
        k_out[:, sl] = (k_lin[:, sl] + k_rope).astype(BF16)


def _in_proj(h, cos_pat, sin_pat, lw):
    t = h.shape[0]
    tm = ROW_BLOCK
    row = lambda w: pl.BlockSpec((tm, w), lambda i: (i, 0))
    weights = [lw["w_hg"], lw["w_sg"], lw["w_cq"], lw["w_ckv"], lw["w_krl"], lw["w_krr"],
               lw["gq"], lw["wq_l"], lw["wq_r"], lw["gkv"], lw["wk"], lw["wv"]]
    hp = C_HEADS * HEAD_PAD
    return pl.pallas_call(
        _inproj_kernel,
        out_shape=(jax.ShapeDtypeStruct((t, 4 * A_WIDTH), F32),
                   jax.ShapeDtypeStruct((t, 2 * B_WIDTH), F32),
                   jax.ShapeDtypeStruct((t, hp), BF16),
                   jax.ShapeDtypeStruct((t, hp), BF16),
                   jax.ShapeDtypeStruct((t, C_WIDTH), BF16)),
        grid=(t // tm,),
        in_specs=[row(D_MODEL), row(HEAD_PAD), row(HEAD_PAD)] + [_const_spec(w.shape) for w in weights],
        out_specs=(row(4 * A_WIDTH), row(2 * B_WIDTH), row(hp), row(hp), row(C_WIDTH)),
        compiler_params=_params(("parallel",)),
        name="in_proj",
    )(h, cos_pat, sin_pat, *weights)


def _erf(x):
    return lax.erf(x)


def _gelu(x):
    return 0.5 * x * (1.0 + _erf(x * (2.0 ** -0.5)))


def _rows_to_array(rows, like):
    zero = jnp.zeros_like(like)
    return jnp.concatenate([zero if r is None else r for r in rows], axis=0)


def _mixer_kernel(hg_ref, sg_ref, lb_ref, ng_ref, lng_ref, lnb_ref, ws_ref, bs_ref, ones_ref,
                  o_ref,
                  q_s, k_s, f_s, v_s, qt_s, kt_s, oi_s, a_s, b_s, qs_s, ks_s, state_s):
    nc, cs, w = MIX_NCHUNK, A_CHUNK, A_WIDTH
    tb = nc * cs

    @pl.when(pl.program_id(1) == 0)
    def _():
        state_s[...] = jnp.zeros_like(state_s)

    hg = hg_ref[...]
    lb = lb_ref[...]
    f = lb + (1.0 - lb) * _sigmoid(hg[:, w:2 * w])
    aq = hg[:, 0:w]
    q_s[...] = (aq * _sigmoid(aq)).reshape(nc, cs, w)
    f_s[...] = f.reshape(nc, cs, w)
    k_s[...] = (1.0 - f).reshape(nc, cs, w)
    v_s[...] = hg[:, 2 * w:3 * w].reshape(nc, cs, w)

    ones_bd = ones_ref[...]

    kd = []
    for p in range(cs):
        f_p = f_s[:, p, :]
        q_p = q_s[:, p, :]
        kd = [k_s[:, p, :]] + [f_p * x for x in kd]
        xs = jnp.concatenate([q_p * x for x in kd], axis=0).astype(BF16)
        wgt = _dot(xs, ones_bd)
        o_p = wgt[0:nc] * v_s[:, p, :]
        for d in range(1, p + 1):
            o_p = o_p + wgt[d * nc:(d + 1) * nc] * v_s[:, p - d, :]
        oi_s[:, p, :] = o_p

    pref = f_s[:, 0, :]
    qt_s[:, 0, :] = q_s[:, 0, :] * pref
    for p in range(1, cs):
        pref = pref * f_s[:, p, :]
        qt_s[:, p, :] = q_s[:, p, :] * pref
    chunk_decay = pref
    suf = jnp.ones_like(pref)
    kt_s[:, cs - 1, :] = k_s[:, cs - 1, :]
    for p in range(cs - 2, -1, -1):
        suf = suf * f_s[:, p + 1, :]
        kt_s[:, p, :] = k_s[:, p, :] * suf

    drow = [chunk_decay[c:c + 1, :] for c in range(nc)]
    one_row = jnp.ones_like(drow[0])
    levels = []
    m = 1
    while m < nc:
        levels.append(m)
        m *= 2
    la_arrays, lb_arrays = [], []
    for m in levels:
        la = [None] * nc
        lbr = [None] * nc
        for base in range(0, nc, 2 * m):
            mid = base + m
            la[mid] = one_row
            for i in range(mid + 1, mid + m):
                la[i] = la[i - 1] * drow[i - 1]
            lbr[mid - 1] = one_row
            for j in range(mid - 2, base - 1, -1):
                lbr[j] = lbr[j + 1] * drow[j + 1]
        la_arrays.append(_rows_to_array(la, one_row))
        lb_arrays.append(_rows_to_array(lbr, one_row))
    ep = [one_row]
    for c in range(1, nc):
        ep.append(ep[-1] * drow[c - 1])
    es = [one_row] * nc
    for c in range(nc - 2, -1, -1):
        es[c] = es[c + 1] * drow[c + 1]
    total_decay = ep[-1] * drow[nc - 1]
    ep_arr = jnp.concatenate(ep, axis=0)
    es_arr = jnp.concatenate(es, axis=0)

    for p in range(cs):
        qt_p = qt_s[:, p, :]
        kt_p = kt_s[:, p, :]
        for li in range(len(levels)):
            a_s[li, :, p, :] = qt_p * la_arrays[li]
            b_s[li, :, p, :] = kt_p * lb_arrays[li]
        qs_s[:, p, :] = qt_p * ep_arr
        ks_s[:, p, :] = kt_p * es_arr

    lane = lax.broadcasted_iota(jnp.int32, (tb, w), 1)
    head_masks = [(lane >= h * A_DK) & (lane < (h + 1) * A_DK) for h in range(A_HEADS)]
    row_t = lax.broadcasted_iota(jnp.int32, (tb, tb), 0)
    col_t = lax.broadcasted_iota(jnp.int32, (tb, tb), 1)
    v_tok = v_s[...].reshape(tb, w)
    v_bf = v_tok.astype(BF16)
    a_tok = [a_s[li].reshape(tb, w) for li in range(len(levels))]
    b_tok = [b_s[li].reshape(tb, w).astype(BF16) for li in range(len(levels))]
    o_cross = jnp.zeros((tb, w), F32)
    for h in range(A_HEADS):
        sc = None
        for li, m in enumerate(levels):
            a_h = jnp.where(head_masks[h], a_tok[li], 0.0).astype(BF16)
            s_l = _dot_nt(a_h, b_tok[li])
            if 2 * m < nc:
                grp = 2 * m * cs
                s_l = jnp.where((row_t // grp) == (col_t // grp), s_l, 0.0)
            sc = s_l if sc is None else sc + s_l
        o_h = _dot(sc.astype(BF16), v_bf)
        o_cross = o_cross + jnp.where(head_masks[h], o_h, 0.0)

    st = state_s[...]
    o_state = _dot_nt(qs_s[...].reshape(tb, w).astype(BF16), st.astype(BF16))
    kv = _dot_tn(v_bf, ks_s[...].reshape(tb, w).astype(BF16))
    rr = lax.broadcasted_iota(jnp.int32, (w, w), 0)
    cc = lax.broadcasted_iota(jnp.int32, (w, w), 1)
    state_s[...] = st * total_decay + jnp.where((rr // A_DK) == (cc // A_DK), kv, 0.0)

    o = oi_s[...].reshape(tb, w) + o_cross + o_state
    ms = _dot((o * o).astype(BF16), ones_bd) * (1.0 / A_DK)
    ag = hg[:, 3 * w:4 * w]
    o_a = o * lax.rsqrt(ms + RMS_EPS) * ng_ref[...] * (ag * _sigmoid(ag))
    o_ref[:, 0:w] = o_a.astype(o_ref.dtype)

    sg = sg_ref[...]
    u = _gelu(sg[:, 0:B_WIDTH])
    vv = _layer_norm(_gelu(sg[:, B_WIDTH:2 * B_WIDTH]), lng_ref[...], lnb_ref[...]).astype(BF16)
    lane_b = lax.broadcasted_iota(jnp.int32, (B_CHUNK, B_WIDTH), 1)
    bs = bs_ref[...]
    for ci in range(tb // B_CHUNK):
        v_c = vv[ci * B_CHUNK:(ci + 1) * B_CHUNK]
        z = bs
        for g in range(B_GROUPS):
            z_g = _dot(ws_ref[g], v_c)
            z = z + jnp.where((lane_b >= g * B_CH) & (lane_b < (g + 1) * B_CH), z_g, 0.0)
        rows = slice(ci * B_CHUNK, (ci + 1) * B_CHUNK)
        o_ref[rows, w:w + B_WIDTH] = (u[rows] * z).astype(o_ref.dtype)


def _mixers(hg, sg, lw, bsz, seq):
    nb = seq // MIX_BLOCK
    nc, cs, w = MIX_NCHUNK, A_CHUNK, A_WIDTH
    nlev = int(round(math.log2(nc)))
    blk = lambda width: pl.BlockSpec((MIX_BLOCK, width), lambda b, i: (b * nb + i, 0))
    pm = lambda: pltpu.VMEM((nc, cs, w), F32)
    return pl.pallas_call(
        _mixer_kernel,
        out_shape=jax.ShapeDtypeStruct((bsz * seq, A_WIDTH + B_WIDTH), BF16),
        grid=(bsz, nb),
        in_specs=[blk(4 * A_WIDTH), blk(2 * B_WIDTH),
                  _const_spec((1, w)), _const_spec((1, w)), _const_spec((1, B_WIDTH)), _const_spec((1, B_WIDTH)),
                  _const_spec((B_GROUPS, B_CHUNK, B_CHUNK)), _const_spec((B_CHUNK, B_WIDTH)),
                  _const_spec((w, w))],
        out_specs=blk(A_WIDTH + B_WIDTH),
        scratch_shapes=[pm(), pm(), pm(), pm(), pm(), pm(), pm(),
                        pltpu.VMEM((nlev, nc, cs, w), F32), pltpu.VMEM((nlev, nc, cs, w), F32),
                        pm(), pm(), pltpu.VMEM((w, w), F32)],
        compiler_params=_params(("parallel", "arbitrary")),
        name="mixers",
    )(hg, sg, lw["lb"], lw["hgrn_g"], lw["sgu_g"], lw["sgu_b"], lw["ws"], lw["bs"], lw["ones_bd"])


def _attn_kernel(q_ref, k_ref, v_ref, o_ref):
    tq, tk = ATT_TQ, ATT_TK
    qi = pl.program_id(2)
    qs = [q_ref[:, hh * HEAD_PAD:(hh + 1) * HEAD_PAD] for hh in range(2)]

    def tile(j, carry, masked):
        off = pl.multiple_of(j * tk, tk)
        v_t = v_ref[pl.ds(off, tk), :]
        new = []
        for hh in range(2):
            m, l, acc = carry[hh]
            k_t = k_ref[pl.ds(off, tk), hh * HEAD_PAD:(hh + 1) * HEAD_PAD]
            s = _dot_nt(qs[hh], k_t)
            if masked:
                row = lax.broadcasted_iota(jnp.int32, (tq, tk), 0)
                col = lax.broadcasted_iota(jnp.int32, (tq, tk), 1)
                s = jnp.where(col <= row, s, NEG_BIG)
            m_new = jnp.maximum(m, jnp.max(s, axis=-1, keepdims=True))
            alpha = jnp.exp(m - m_new)
            p = jnp.exp(s - m_new)
            l = alpha * l + jnp.sum(p, axis=-1, keepdims=True)
            acc = alpha * acc + _dot(p.astype(BF16), v_t)
            new.append((m_new, l, acc))
        return tuple(new)

    init = tuple((jnp.full((tq, 1), NEG_BIG, F32), jnp.zeros((tq, 1), F32), jnp.zeros((tq, 2 * C_V), F32))
                 for _ in range(2))
    carry = lax.fori_loop(0, qi, lambda j, c: tile(j, c, False), init)
    carry = tile(qi, carry, True)
    lane = lax.broadcasted_iota(jnp.int32, (tq, 2 * C_V), 1)
    o0 = carry[0][2] / carry[0][1]
    o1 = carry[1][2] / carry[1][1]
    o_ref[...] = jnp.where(lane < C_V, o0, o1).astype(o_ref.dtype)


def _attention(q, k, v, bsz, seq):
    hp2 = 2 * HEAD_PAD
    return pl.pallas_call(
        _attn_kernel,
        out_shape=jax.ShapeDtypeStruct((bsz, seq, C_WIDTH), BF16),
        grid=(bsz, C_HEADS // 2, seq // ATT_TQ),
        in_specs=[pl.BlockSpec((None, ATT_TQ, hp2), lambda b, g, i: (b, i, g)),
                  pl.BlockSpec((None, seq, hp2), lambda b, g, i: (b, 0, g)),
                  pl.BlockSpec((None, seq, 2 * C_V), lambda b, g, i: (b, 0, g))],
        out_specs=pl.BlockSpec((None, ATT_TQ, 2 * C_V), lambda b, g, i: (b, i, g)),
        compiler_params=_params(("parallel", "parallel", "arbitrary")),
        name="mla_attention",
    )(q.reshape(bsz, seq, -1), k.reshape(bsz, seq, -1), v.reshape(bsz, seq, -1))


def _outproj_kernel(oab_ref, oc_ref, h_ref, wab_ref, wc_ref, g_ref, b_ref, o_ref):
    mix = _dot(oab_ref[...], wab_ref[...]) + _dot(oc_ref[...], wc_ref[...])
    o_ref[...] = _layer_norm(DEEPNORM_ALPHA * h_ref[...] + mix, g_ref[...], b_ref[...])


def _out_proj(o_ab, o_c, h, lw):
    t = h.shape[0]
    tm = ROW_BLOCK
    row = lambda w: pl.BlockSpec((tm, w), lambda i: (i, 0))
    return pl.pallas_call(
        _outproj_kernel,
        out_shape=jax.ShapeDtypeStruct((t, D_MODEL), F32),
        grid=(t // tm,),
        in_specs=[row(A_WIDTH + B_WIDTH), row(C_WIDTH), row(D_MODEL),
                  _const_spec((A_WIDTH + B_WIDTH, D_MODEL)), _const_spec((C_WIDTH, D_MODEL)),
                  _const_spec((1, D_MODEL)), _const_spec((1, D_MODEL))],
        out_specs=row(D_MODEL),
        compiler_params=_params(("parallel",)),
        name="out_proj_ln",
    )(o_ab, o_c, h, lw["w_out_ab"], lw["w_out_c"], lw["ln1_g"], lw["ln1_b"])


def _ffn_kernel(h_ref, p_ref, wg_ref, wu_ref, wd_ref, wpg_ref, wpp_ref, g_ref, b_ref, o_ref, act_s):
    h = h_ref[...]
    hb = h.astype(BF16)
    for c in range(D_FF // FF_CHUNK):
        cols = slice(c * FF_CHUNK, (c + 1) * FF_CHUNK)
        gate = _dot(hb, wg_ref[:, cols])
        up = _dot(hb, wu_ref[:, cols])
        act_s[:, cols] = (gate * _sigmoid(gate) * up).astype(BF16)
    ffn = _dot(act_s[...], wd_ref[...])
    ple = _sigmoid(_dot(hb, wpg_ref[...])) * _dot(p_ref[...].astype(BF16), wpp_ref[...])
    o_ref[...] = _layer_norm(DEEPNORM_ALPHA * h + ffn + ple, g_ref[...], b_ref[...])


def _ffn(h, p, lw):
    t = h.shape[0]
    tm = ROW_BLOCK
    row = lambda w: pl.BlockSpec((tm, w), lambda i: (i, 0))
    single = lambda shape: pl.BlockSpec(shape, lambda i: (0,) * len(shape), pipeline_mode=pl.Buffered(1))
    return pl.pallas_call(
        _ffn_kernel,
        out_shape=jax.ShapeDtypeStruct((t, D_MODEL), F32),
        grid=(t // tm,),
        in_specs=[row(D_MODEL), row(PLE_DIM),
                  single((D_MODEL, D_FF)), single((D_MODEL, D_FF)), single((D_FF, D_MODEL)),
                  single((D_MODEL, D_MODEL)), single((PLE_DIM, D_MODEL)),
                  _const_spec((1, D_MODEL)), _const_spec((1, D_MODEL))],
        out_specs=row(D_MODEL),
        scratch_shapes=[pltpu.VMEM((tm, D_FF), BF16)],
        compiler_params=_params(("parallel",)),
        name="ffn_ple_ln",
    )(h, p, lw["wg"], lw["wu"], lw["wd"], lw["wpg"], lw["wpp"], lw["ln2_g"], lw["ln2_b"])


def _head_pad_cols(w_nope, w_rope):
    kdim = w_nope.shape[0]
    pad = jnp.zeros((kdim, C_HEADS, HEAD_PAD - C_NOPE - C_ROPE), w_nope.dtype)
    return jnp.concatenate([w_nope, w_rope, pad], axis=2).reshape(kdim, C_HEADS * HEAD_PAD)


def _rot_half(w_rope):
    half = C_ROPE // 2
    return jnp.concatenate([-w_rope[..., half:], w_rope[..., :half]], axis=-1)


def _prep_layer(i, lower_bounds, w_in, hgrn_norm_g, sgu_ln_g, sgu_ln_b, sgu_w_s, sgu_b_s,
                mla_q_norm_g, mla_w_uq, mla_kv_norm_g, mla_w_ukv, w_out, ln1_g, ln1_b,
                w_gate_up, w_down, ple_w_gate, ple_w_proj, ln2_g, ln2_b):
    wi = w_in[i]
    o_hg, o_sg = 4 * A_WIDTH, 4 * A_WIDTH + 2 * B_WIDTH
    o_kv = o_sg + C_Q_RANK
    o_kr = o_kv + C_KV_RANK
    w_kr = wi[:, o_kr:o_kr + C_ROPE]
    lane_pad = lambda wr: jnp.concatenate(
        [jnp.zeros((D_MODEL, C_NOPE), F32), wr, jnp.zeros((D_MODEL, HEAD_PAD - C_NOPE - C_ROPE), F32)], axis=1)
    wq = mla_w_uq[i].reshape(C_Q_RANK, C_HEADS, C_NOPE + C_ROPE)
    wq_nope, wq_rope = wq[..., :C_NOPE], wq[..., C_NOPE:]
    wkv = mla_w_ukv[i].reshape(C_KV_RANK, C_HEADS, C_NOPE + C_V)
    wk_nope, wv = wkv[..., :C_NOPE], wkv[..., C_NOPE:]
    zeros_rope = jnp.zeros((C_KV_RANK, C_HEADS, C_ROPE), F32)
    tri = jnp.tril(jnp.ones((B_CHUNK, B_CHUNK), F32))
    head_id = np.arange(A_WIDTH) // A_DK
    ones_bd = jnp.asarray(head_id[:, None] == head_id[None, :], BF16)
    wo = w_out[i]
    wgu = w_gate_up[i]
    return dict(
        w_hg=wi[:, :o_hg].astype(BF16), w_sg=wi[:, o_hg:o_sg].astype(BF16),
        w_cq=wi[:, o_sg:o_kv].astype(BF16), w_ckv=wi[:, o_kv:o_kr].astype(BF16),
        w_krl=lane_pad(w_kr).astype(BF16), w_krr=lane_pad(_rot_half(w_kr)).astype(BF16),
        gq=mla_q_norm_g[i].reshape(1, -1), gkv=mla_kv_norm_g[i].reshape(1, -1),
        wq_l=_head_pad_cols(wq_nope, wq_rope).astype(BF16),
        wq_r=_head_pad_cols(jnp.zeros_like(wq_nope), _rot_half(wq_rope)).astype(BF16),
        wk=_head_pad_cols(wk_nope, zeros_rope).astype(BF16),
        wv=wv.reshape(C_KV_RANK, C_WIDTH).astype(BF16),
        lb=lower_bounds[i].reshape(1, -1), hgrn_g=hgrn_norm_g[i].reshape(1, -1),
        sgu_g=sgu_ln_g[i].reshape(1, -1), sgu_b=sgu_ln_b[i].reshape(1, -1),
        ws=(sgu_w_s[i] * tri).astype(BF16),
        bs=jnp.repeat(sgu_b_s[i].T, B_CH, axis=1),
        ones_bd=ones_bd,
        w_out_ab=wo[:A_WIDTH + B_WIDTH].astype(BF16), w_out_c=wo[A_WIDTH + B_WIDTH:].astype(BF16),
        ln1_g=ln1_g[i].reshape(1, -1), ln1_b=ln1_b[i].reshape(1, -1),
        wg=wgu[:, :D_FF].astype(BF16), wu=wgu[:, D_FF:].astype(BF16), wd=w_down[i].astype(BF16),
        wpg=ple_w_gate[i].astype(BF16), wpp=ple_w_proj[i].astype(BF16),
        ln2_g=ln2_g[i].reshape(1, -1), ln2_b=ln2_b[i].reshape(1, -1),
    )


def kernel(x, p, positions, ln_in_g, ln_in_b, w_in, hgrn_lb_logits, hgrn_norm_g, sgu_ln_g, sgu_ln_b, sgu_w_s, sgu_b_s, mla_q_norm_g, mla_w_uq, mla_kv_norm_g, mla_w_ukv, w_out, ln1_g, ln1_b, w_gate_up, w_down, ple_w_gate, ple_w_proj, ln2_g, ln2_b):
    bsz, seq, d = x.shape
    t = bsz * seq
    lb_cum = jnp.cumsum(jax.nn.softmax(hgrn_lb_logits.astype(F32), axis=0), axis=0)
    lower_bounds = lb_cum - lb_cum[0]
    cos_pat, sin_pat = _rope_tables(positions)

    h = _ln_in(x.reshape(t, d), ln_in_g, ln_in_b)
    for i in range(DEPTH):
        lw = _prep_layer(i, lower_bounds, w_in, hgrn_norm_g, sgu_ln_g, sgu_ln_b, sgu_w_s, sgu_b_s,
                         mla_q_norm_g, mla_w_uq, mla_kv_norm_g, mla_w_ukv, w_out, ln1_g, ln1_b,
                         w_gate_up, w_down, ple_w_gate, ple_w_proj, ln2_g, ln2_b)
        hg, sg, q, k, v = _in_proj(h, cos_pat, sin_pat, lw)
        o_ab = _mixers(hg, sg, lw, bsz, seq)
        o_c = _attention(q, k, v, bsz, seq).reshape(t, C_WIDTH)
        h = _out_proj(o_ab, o_c, h, lw)
        h = _ffn(h, p[i].reshape(t, PLE_DIM), lw)
    return h.reshape(bsz, seq, d)
```

```python
import functools
import math

import numpy as np
import jax
import jax.numpy as jnp
from jax import lax
from jax.experimental import pallas as pl
from jax.experimental.pallas import tpu as pltpu

F32 = jnp.float32
BF16 = jnp.bfloat16

D_MODEL = 1024
DEPTH = 2
PLE_DIM = 256

A_WIDTH = 256
A_DK = 64
A_HEADS = 4
A_CHUNK = 16
B_WIDTH = 256
B_CH = 64
B_GROUPS = 4
B_CHUNK = 128
C_WIDTH = 512
C_NOPE = 64
C_ROPE = 32
C_V = 64
C_HEADS = 8
C_Q_RANK = 384
C_KV_RANK = 256
ROPE_THETA = 10000.0
HEAD_PAD = 128

D_FF = 2816
LN_EPS = 1e-5
RMS_EPS = 1e-6
DEEPNORM_ALPHA = (2 * DEPTH) ** 0.25
ATT_SCALE = (C_NOPE + C_ROPE) ** -0.5
Q_SCALE = ATT_SCALE * math.log2(math.e)

MIX_BLOCK = 256
MIX_NCHUNK = MIX_BLOCK // A_CHUNK
ATT_TQ = 512
ATT_TK = 256
ROW_BLOCK = 512
FF_CHUNK = 256
VMEM_LIMIT = 56 * 1024 * 1024
NEG_BIG = -1e30


def _dot(a, b):
    return jnp.dot(a, b, preferred_element_type=F32)


def _dot_nt(a, b):
    return lax.dot_general(a, b, (((1,), (1,)), ((), ())), preferred_element_type=F32)


def _dot_tn(a, b):
    return lax.dot_general(a, b, (((0,), (0,)), ((), ())), preferred_element_type=F32)


def _layer_norm(x, g, b):
    mu = jnp.mean(x, axis=-1, keepdims=True)
    xc = x - mu
    var = jnp.mean(xc * xc, axis=-1, keepdims=True)
    return xc * lax.rsqrt(var + LN_EPS) * g + b


def _rms_norm(x, g):
    return x * lax.rsqrt(jnp.mean(x * x, axis=-1, keepdims=True) + RMS_EPS) * g


def _sigmoid(x):
    return 1.0 / (1.0 + jnp.exp(-x))


def _const_spec(shape):
    nd = len(shape)
    return pl.BlockSpec(shape, lambda *_: (0,) * nd)


def _params(sem):
    return pltpu.CompilerParams(dimension_semantics=sem, vmem_limit_bytes=VMEM_LIMIT)


def _rope_table_kernel(pos_ref, freq_ref, cos_ref, sin_ref):
    ang = pos_ref[...] * freq_ref[...]
    cos_ref[...] = jnp.cos(ang)
    sin_ref[...] = jnp.sin(ang)


def _rope_tables(positions):
    t = positions.size
    half = C_ROPE // 2
    per_row = 128 // half
    inv_freq = ROPE_THETA ** (-jnp.arange(0, C_ROPE, 2, dtype=F32) / C_ROPE)
    pos_rep = jnp.repeat(positions.astype(F32).reshape(t), half).reshape(t // per_row, 128)
    freq = jnp.tile(inv_freq, per_row).reshape(1, 128)
    rows = t // per_row
    cos, sin = pl.pallas_call(
        _rope_table_kernel,
        out_shape=(jax.ShapeDtypeStruct((rows, 128), F32),) * 2,
        grid=(1,),
        in_specs=[_const_spec((rows, 128)), _const_spec((1, 128))],
        out_specs=(_const_spec((rows, 128)),) * 2,
        name="rope_tables",
    )(pos_rep, freq)
    cos = cos.reshape(t, half)
    sin = sin.reshape(t, half)
    ones = jnp.ones((t, C_NOPE), F32)
    zeros = jnp.zeros((t, C_NOPE), F32)
    pad1 = jnp.ones((t, HEAD_PAD - C_NOPE - C_ROPE), F32)
    pad0 = jnp.zeros((t, HEAD_PAD - C_NOPE - C_ROPE), F32)
    cos_pat = jnp.concatenate([ones, cos, cos, pad1], axis=1)
    sin_pat = jnp.concatenate([zeros, sin, sin, pad0], axis=1)
    return cos_pat, sin_pat


def _ln_kernel(x_ref, g_ref, b_ref, o_ref):
    o_ref[...] = _layer_norm(x_ref[...], g_ref[...], b_ref[...])


def _ln_in(x2d, g, b):
    t, d = x2d.shape
    return pl.pallas_call(
        _ln_kernel,
        out_shape=jax.ShapeDtypeStruct((t, d), F32),
        grid=(t // ROW_BLOCK,),
        in_specs=[pl.BlockSpec((ROW_BLOCK, d), lambda i: (i, 0)), _const_spec((1, d)), _const_spec((1, d))],
        out_specs=pl.BlockSpec((ROW_BLOCK, d), lambda i: (i, 0)),
        compiler_params=_params(("parallel",)),
        name="ln_in",
    )(x2d, g.reshape(1, d), b.reshape(1, d))


def _inproj_kernel(h_ref, cos_ref, sin_ref, w_hg, w_sg, w_cq, w_ckv, w_krl, w_krr,
                   gq_ref, wq_l, wq_r, gkv_ref, wk_ref, wv_ref,
                   hg_out, sg_out, q_out, k_out, v_out):
    hb = h_ref[...].astype(BF16)
    hg_out[...] = _dot(hb, w_hg[...])
    sg_out[...] = _dot(hb, w_sg[...])
    cos = cos_ref[...]
    sin = sin_ref[...]

    cqn = _rms_norm(_dot(hb, w_cq[...]), gq_ref[...]).astype(BF16)
    q_lin = _dot(cqn, wq_l[...])
    q_rot = _dot(cqn, wq_r[...])
    for h in range(C_HEADS):
        sl = slice(h * HEAD_PAD, (h + 1) * HEAD_PAD)
        q_out[:, sl] = ((q_lin[:, sl] * cos + q_rot[:, sl] * sin) * Q_SCALE).astype(BF16)

    ckvn = _rms_norm(_dot(hb, w_ckv[...]), gkv_ref[...]).astype(BF16)
    k_lin = _dot(ckvn, wk_ref[...])
    vt = _dot_nt(wv_ref[...], ckvn).astype(BF16)
    for c in range(v_out.shape[0]):
        v_out[c] = vt[:, c * ATT_TK:(c + 1) * ATT_TK]
    k_rope = _dot(hb, w_krl[...]) * cos + _dot(hb, w_krr[...]) * sin
    for h in range(C_HEADS):
        sl = slice(h * HEAD_PAD, (h + 1) * HEAD_PAD)
        k_out[:, sl] = (k_lin[:, sl] + k_rope).astype(BF16)


def _in_proj(h, cos_pat, sin_pat, lw):
    t = h.shape[0]
    tm = ROW_BLOCK
    row = lambda w: pl.BlockSpec((tm, w), lambda i: (i, 0))
    weights = [lw["w_hg"], lw["w_sg"], lw["w_cq"], lw["w_ckv"], lw["w_krl"], lw["w_krr"],
               lw["gq"], lw["wq_l"], lw["wq_r"], lw["gkv"], lw["wk"], lw["wv"]]
    hp = C_HEADS * HEAD_PAD
    return pl.pallas_call(
        _inproj_kernel,
        out_shape=(jax.ShapeDtypeStruct((t, 4 * A_WIDTH), F32),
                   jax.ShapeDtypeStruct((t, 2 * B_WIDTH), F32),
                   jax.ShapeDtypeStruct((t, hp), BF16),
                   jax.ShapeDtypeStruct((t, hp), BF16),
                   jax.ShapeDtypeStruct((t // ATT_TK, C_WIDTH, ATT_TK), BF16)),
        grid=(t // tm,),
        in_specs=[row(D_MODEL), row(HEAD_PAD), row(HEAD_PAD)] + [_const_spec(w.shape) for w in weights],
        out_specs=(row(4 * A_WIDTH), row(2 * B_WIDTH), row(hp), row(hp),
                   pl.BlockSpec((tm // ATT_TK, C_WIDTH, ATT_TK), lambda i: (i, 0, 0))),
        compiler_params=_params(("parallel",)),
        name="in_proj",
    )(h, cos_pat, sin_pat, *weights)


def _erf(x):
    return lax.erf(x)


def _gelu(x):
    return 0.5 * x * (1.0 + _erf(x * (2.0 ** -0.5)))


def _rows_to_array(rows, like):
    zero = jnp.zeros_like(like)
    return jnp.concatenate([zero if r is None else r for r in rows], axis=0)


def _mixer_kernel(hg_ref, sg_ref, lb_ref, ng_ref, lng_ref, lnb_ref, ws_ref, bs_ref, ones_ref,
                  o_ref,
                  q_s, k_s, f_s, v_s, qt_s, kt_s, oi_s, a_s, b_s, qs_s, ks_s, state_s):
    nc, cs, w = MIX_NCHUNK, A_CHUNK, A_WIDTH
    tb = nc * cs
    hgw, sgw, ow = 4 * A_WIDTH, 2 * B_WIDTH, A_WIDTH + B_WIDTH

    @pl.when(pl.program_id(1) == 0)
    def _():
        state_s[...] = jnp.zeros_like(state_s)

    lb = lb_ref[...]
    for p in range(cs):
        base = p * hgw
        aq = hg_ref[:, base:base + w]
        f_p = lb + (1.0 - lb) * _sigmoid(hg_ref[:, base + w:base + 2 * w])
        q_s[p] = aq * _sigmoid(aq)
        f_s[p] = f_p
        k_s[p] = 1.0 - f_p
        v_s[p] = hg_ref[:, base + 2 * w:base + 3 * w]

    ones_bd = ones_ref[...]

    kd = []
    for p in range(cs):
        f_p = f_s[p]
        q_p = q_s[p]
        kd = [k_s[p]] + [f_p * x for x in kd]
        xs = jnp.concatenate([q_p * x for x in kd], axis=0).astype(BF16)
        wgt = _dot(xs, ones_bd)
        o_p = wgt[0:nc] * v_s[p]
        for d in range(1, p + 1):
            o_p = o_p + wgt[d * nc:(d + 1) * nc] * v_s[p - d]
        oi_s[p] = o_p

    pref = f_s[0]
    qt_s[0] = q_s[0] * pref
    for p in range(1, cs):
        pref = pref * f_s[p]
        qt_s[p] = q_s[p] * pref
    chunk_decay = pref
    suf = jnp.ones_like(pref)
    kt_s[cs - 1] = k_s[cs - 1]
    for p in range(cs - 2, -1, -1):
        suf = suf * f_s[p + 1]
        kt_s[p] = k_s[p] * suf

    drow = [chunk_decay[c:c + 1, :] for c in range(nc)]
    one_row = jnp.ones_like(drow[0])
    levels = []
    m = 1
    while m < nc:
        levels.append(m)
        m *= 2
    la_arrays, lb_arrays = [], []
    for m in levels:
        la = [None] * nc
        lbr = [None] * nc
        for base in range(0, nc, 2 * m):
            mid = base + m
            la[mid] = one_row
            for i in range(mid + 1, mid + m):
                la[i] = la[i - 1] * drow[i - 1]
            lbr[mid - 1] = one_row
            for j in range(mid - 2, base - 1, -1):
                lbr[j] = lbr[j + 1] * drow[j + 1]
        la_arrays.append(_rows_to_array(la, one_row))
        lb_arrays.append(_rows_to_array(lbr, one_row))
    ep = [one_row]
    for c in range(1, nc):
        ep.append(ep[-1] * drow[c - 1])
    es = [one_row] * nc
    for c in range(nc - 2, -1, -1):
        es[c] = es[c + 1] * drow[c + 1]
    total_decay = ep[-1] * drow[nc - 1]
    ep_arr = jnp.concatenate(ep, axis=0)
    es_arr = jnp.concatenate(es, axis=0)

    for p in range(cs):
        qt_p = qt_s[p]
        kt_p = kt_s[p]
        for li in range(len(levels)):
            a_s[li, p] = qt_p * la_arrays[li]
            b_s[li, p] = kt_p * lb_arrays[li]
        qs_s[p] = qt_p * ep_arr
        ks_s[p] = kt_p * es_arr

    lane = lax.broadcasted_iota(jnp.int32, (tb, w), 1)
    head_masks = [(lane >= h * A_DK) & (lane < (h + 1) * A_DK) for h in range(A_HEADS)]
    row_c = lax.broadcasted_iota(jnp.int32, (tb, tb), 0) % nc
    col_c = lax.broadcasted_iota(jnp.int32, (tb, tb), 1) % nc
    v_bf = v_s[...].reshape(tb, w).astype(BF16)
    a_tok = [a_s[li].reshape(tb, w) for li in range(len(levels))]
    b_tok = [b_s[li].reshape(tb, w).astype(BF16) for li in range(len(levels))]
    o_cross = jnp.zeros((tb, w), F32)
    for h in range(A_HEADS):
        sc = None
        for li, m in enumerate(levels):
            a_h = jnp.where(head_masks[h], a_tok[li], 0.0).astype(BF16)
            s_l = _dot_nt(a_h, b_tok[li])
            if 2 * m < nc:
                s_l = jnp.where((row_c // (2 * m)) == (col_c // (2 * m)), s_l, 0.0)
            sc = s_l if sc is None else sc + s_l
        o_h = _dot(sc.astype(BF16), v_bf)
        o_cross = o_cross + jnp.where(head_masks[h], o_h, 0.0)

    st = state_s[...]
    o_state = _dot_nt(qs_s[...].reshape(tb, w).astype(BF16), st.astype(BF16))
    kv = _dot_tn(v_bf, ks_s[...].reshape(tb, w).astype(BF16))
    rr = lax.broadcasted_iota(jnp.int32, (w, w), 0)
    cc = lax.broadcasted_iota(jnp.int32, (w, w), 1)
    state_s[...] = st * total_decay + jnp.where((rr // A_DK) == (cc // A_DK), kv, 0.0)

    o = oi_s[...].reshape(tb, w) + o_cross + o_state
    ms = _dot((o * o).astype(BF16), ones_bd) * (1.0 / A_DK)
    on = o * lax.rsqrt(ms + RMS_EPS) * ng_ref[...]
    for p in range(cs):
        ag = hg_ref[:, p * hgw + 3 * w:(p + 1) * hgw]
        o_ref[:, p * ow:p * ow + w] = (on[p * nc:(p + 1) * nc] * (ag * _sigmoid(ag))).astype(o_ref.dtype)

    cps = B_CHUNK // cs
    us, vs = [], []
    for p in range(cs):
        us.append(_gelu(sg_ref[:, p * sgw:p * sgw + B_WIDTH]))
        vs.append(_layer_norm(_gelu(sg_ref[:, p * sgw + B_WIDTH:(p + 1) * sgw]), lng_ref[...], lnb_ref[...]))
    lane_b = lax.broadcasted_iota(jnp.int32, (B_CHUNK, B_WIDTH), 1)
    zs = []
    for ci in range(nc // cps):
        v_c = jnp.concatenate([vs[p][ci * cps:(ci + 1) * cps] for p in range(cs)], axis=0).astype(BF16)
        z = bs_ref[...]
        for g in range(B_GROUPS):
            z_g = _dot(ws_ref[g], v_c)
            z = z + jnp.where((lane_b >= g * B_CH) & (lane_b < (g + 1) * B_CH), z_g, 0.0)
        zs.append(z)
    for p in range(cs):
        z_p = jnp.concatenate([z[p * cps:(p + 1) * cps] for z in zs], axis=0)
        o_ref[:, p * ow + w:(p + 1) * ow] = (us[p] * z_p).astype(o_ref.dtype)


def _mixers(hg, sg, lw, bsz, seq):
    nb = seq // MIX_BLOCK
    nc, cs, w = MIX_NCHUNK, A_CHUNK, A_WIDTH
    nlev = int(round(math.log2(nc)))
    t = bsz * seq
    blk = lambda width: pl.BlockSpec((None, nc, cs * width), lambda b, i: (b * nb + i, 0, 0))
    pm = lambda: pltpu.VMEM((cs, nc, w), F32)
    ow = A_WIDTH + B_WIDTH
    out = pl.pallas_call(
        _mixer_kernel,
        out_shape=jax.ShapeDtypeStruct((t // MIX_BLOCK, nc, cs * ow), BF16),
        grid=(bsz, nb),
        in_specs=[blk(4 * A_WIDTH), blk(2 * B_WIDTH),
                  _const_spec((1, w)), _const_spec((1, w)), _const_spec((1, B_WIDTH)), _const_spec((1, B_WIDTH)),
                  _const_spec((B_GROUPS, B_CHUNK, B_CHUNK)), _const_spec((B_CHUNK, B_WIDTH)),
                  _const_spec((w, w))],
        out_specs=blk(ow),
        scratch_shapes=[pm(), pm(), pm(), pm(), pm(), pm(), pm(),
                        pltpu.VMEM((nlev, cs, nc, w), F32), pltpu.VMEM((nlev, cs, nc, w), F32),
                        pm(), pm(), pltpu.VMEM((w, w), F32)],
        compiler_params=_params(("parallel", "arbitrary")),
        name="mixers",
    )(hg.reshape(t // MIX_BLOCK, nc, cs * 4 * A_WIDTH), sg.reshape(t // MIX_BLOCK, nc, cs * 2 * B_WIDTH),
      lw["lb"], lw["hgrn_g"], lw["sgu_g"], lw["sgu_b"], lw["ws"], lw["bs"], lw["ones_bd"])
    return out.reshape(t, ow)


def _attn_kernel(q_ref, k_ref, vt_ref, o_ref, sa_ref, sb_ref, acc_ref):
    tq, tk = ATT_TQ, ATT_TK
    qi = pl.program_id(2)
    qs = [q_ref[:, hh * HEAD_PAD:(hh + 1) * HEAD_PAD] for hh in range(2)]

    def produce(j, dst, mask=None):
        off = pl.multiple_of(j * tk, tk)
        for hh in range(2):
            k_t = k_ref[pl.ds(off, tk), hh * HEAD_PAD:(hh + 1) * HEAD_PAD]
            st = _dot_nt(k_t, qs[hh])
            if mask is not None:
                st = jnp.where(mask, st, NEG_BIG)
            dst[hh] = st

    def consume(j, src, ml):
        new = []
        for hh in range(2):
            m, l = ml[hh]
            st = src[hh]
            m_new = jnp.maximum(m, jnp.max(st, axis=0, keepdims=True))
            alpha = jnp.exp2(m - m_new)
            pt = jnp.exp2(st - m_new)
            l = alpha * l + jnp.sum(pt, axis=0, keepdims=True)
            vt_h = vt_ref[j, hh * C_V:(hh + 1) * C_V, :]
            acc_ref[hh] = alpha * acc_ref[hh] + _dot(vt_h, pt.astype(BF16))
            new.append((m_new, l))
        return tuple(new)

    key = lax.broadcasted_iota(jnp.int32, (tk, tq), 0)
    qry = lax.broadcasted_iota(jnp.int32, (tk, tq), 1)
    d0 = 2 * qi
    acc_ref[...] = jnp.zeros_like(acc_ref)
    produce(d0, sa_ref, key <= qry)
    ml0 = tuple((jnp.full((1, tq), NEG_BIG, F32), jnp.zeros((1, tq), F32)) for _ in range(2))

    def body(p, carry):
        ml, in_a = carry
        produce(2 * p, sb_ref)
        ml = consume(in_a, sa_ref, ml)
        produce(2 * p + 1, sa_ref)
        ml = consume(2 * p, sb_ref, ml)
        return ml, 2 * p + 1

    ml, in_a = lax.fori_loop(0, qi, body, (ml0, d0))
    produce(d0 + 1, sb_ref, key + tk <= qry)
    ml = consume(in_a, sa_ref, ml)
    ml = consume(d0 + 1, sb_ref, ml)
    for hh in range(2):
        o_ref[hh * C_V:(hh + 1) * C_V, :] = (acc_ref[hh] / ml[hh][1]).astype(o_ref.dtype)


def _attention(q, k, vt, bsz, seq):
    assert ATT_TQ == 2 * ATT_TK
    hp2 = 2 * HEAD_PAD
    nq = seq // ATT_TQ
    nk = seq // ATT_TK
    return pl.pallas_call(
        _attn_kernel,
        out_shape=jax.ShapeDtypeStruct((bsz * nq, C_WIDTH, ATT_TQ), BF16),
        grid=(bsz, C_HEADS // 2, nq),
        in_specs=[pl.BlockSpec((None, ATT_TQ, hp2), lambda b, g, i: (b, i, g)),
                  pl.BlockSpec((None, seq, hp2), lambda b, g, i: (b, 0, g)),
                  pl.BlockSpec((nk, 2 * C_V, ATT_TK), lambda b, g, i: (b, g, 0))],
        out_specs=pl.BlockSpec((None, 2 * C_V, ATT_TQ), lambda b, g, i: (b * nq + i, g, 0)),
        scratch_shapes=[pltpu.VMEM((2, ATT_TK, ATT_TQ), F32), pltpu.VMEM((2, ATT_TK, ATT_TQ), F32),
                        pltpu.VMEM((2, C_V, ATT_TQ), F32)],
        compiler_params=_params(("parallel", "parallel", "arbitrary")),
        name="mla_attention",
    )(q.reshape(bsz, seq, -1), k.reshape(bsz, seq, -1), vt)


def _outproj_kernel(oab_ref, oct_ref, h_ref, wab_ref, wc_ref, g_ref, b_ref, o_ref):
    mix = _dot(oab_ref[...], wab_ref[...]) + _dot_tn(oct_ref[...], wc_ref[...])
    o_ref[...] = _layer_norm(DEEPNORM_ALPHA * h_ref[...] + mix, g_ref[...], b_ref[...])


def _out_proj(o_ab, o_ct, h, lw):
    t = h.shape[0]
    tm = ROW_BLOCK
    row = lambda w: pl.BlockSpec((tm, w), lambda i: (i, 0))
    return pl.pallas_call(
        _outproj_kernel,
        out_shape=jax.ShapeDtypeStruct((t, D_MODEL), F32),
        grid=(t // tm,),
        in_specs=[row(A_WIDTH + B_WIDTH), pl.BlockSpec((None, C_WIDTH, tm), lambda i: (i, 0, 0)), row(D_MODEL),
                  _const_spec((A_WIDTH + B_WIDTH, D_MODEL)), _const_spec((C_WIDTH, D_MODEL)),
                  _const_spec((1, D_MODEL)), _const_spec((1, D_MODEL))],
        out_specs=row(D_MODEL),
        compiler_params=_params(("parallel",)),
        name="out_proj_ln",
    )(o_ab, o_ct, h, lw["w_out_ab"], lw["w_out_c"], lw["ln1_g"], lw["ln1_b"])


def _ffn_kernel(h_ref, p_ref, wg_ref, wu_ref, wd_ref, wpg_ref, wpp_ref, g_ref, b_ref, o_ref, act_s):
    h = h_ref[...]
    hb = h.astype(BF16)
    for c in range(D_FF // FF_CHUNK):
        cols = slice(c * FF_CHUNK, (c + 1) * FF_CHUNK)
        gate = _dot(hb, wg_ref[:, cols])
        up = _dot(hb, wu_ref[:, cols])
        act_s[:, cols] = (gate * _sigmoid(gate) * up).astype(BF16)
    ffn = _dot(act_s[...], wd_ref[...])
    ple = _sigmoid(_dot(hb, wpg_ref[...])) * _dot(p_ref[...].astype(BF16), wpp_ref[...])
    o_ref[...] = _layer_norm(DEEPNORM_ALPHA * h + ffn + ple, g_ref[...], b_ref[...])


def _ffn(h, p, lw):
    t = h.shape[0]
    tm = ROW_BLOCK
    row = lambda w: pl.BlockSpec((tm, w), lambda i: (i, 0))
    single = lambda shape: pl.BlockSpec(shape, lambda i: (0,) * len(shape), pipeline_mode=pl.Buffered(1))
    return pl.pallas_call(
        _ffn_kernel,
        out_shape=jax.ShapeDtypeStruct((t, D_MODEL), F32),
        grid=(t // tm,),
        in_specs=[row(D_MODEL), row(PLE_DIM),
                  single((D_MODEL, D_FF)), single((D_MODEL, D_FF)), single((D_FF, D_MODEL)),
                  single((D_MODEL, D_MODEL)), single((PLE_DIM, D_MODEL)),
                  _const_spec((1, D_MODEL)), _const_spec((1, D_MODEL))],
        out_specs=row(D_MODEL),
        scratch_shapes=[pltpu.VMEM((tm, D_FF), BF16)],
        compiler_params=_params(("parallel",)),
        name="ffn_ple_ln",
    )(h, p, lw["wg"], lw["wu"], lw["wd"], lw["wpg"], lw["wpp"], lw["ln2_g"], lw["ln2_b"])


def _head_pad_cols(w_nope, w_rope):
    kdim = w_nope.shape[0]
    pad = jnp.zeros((kdim, C_HEADS, HEAD_PAD - C_NOPE - C_ROPE), w_nope.dtype)
    return jnp.concatenate([w_nope, w_rope, pad], axis=2).reshape(kdim, C_HEADS * HEAD_PAD)


def _rot_half(w_rope):
    half = C_ROPE // 2
    return jnp.concatenate([-w_rope[..., half:], w_rope[..., :half]], axis=-1)


def _pos_major(x, axis):
    cps = B_CHUNK // A_CHUNK
    shape = x.shape
    x = x.reshape(shape[:axis] + (cps, A_CHUNK) + shape[axis + 1:])
    return jnp.swapaxes(x, axis, axis + 1).reshape(shape)


def _prep_layer(i, lower_bounds, w_in, hgrn_norm_g, sgu_ln_g, sgu_ln_b, sgu_w_s, sgu_b_s,
                mla_q_norm_g, mla_w_uq, mla_kv_norm_g, mla_w_ukv, w_out, ln1_g, ln1_b,
                w_gate_up, w_down, ple_w_gate, ple_w_proj, ln2_g, ln2_b):
    wi = w_in[i]
    o_hg, o_sg = 4 * A_WIDTH, 4 * A_WIDTH + 2 * B_WIDTH
    o_kv = o_sg + C_Q_RANK
    o_kr = o_kv + C_KV_RANK
    w_kr = wi[:, o_kr:o_kr + C_ROPE]
    lane_pad = lambda wr: jnp.concatenate(
        [jnp.zeros((D_MODEL, C_NOPE), F32), wr, jnp.zeros((D_MODEL, HEAD_PAD - C_NOPE - C_ROPE), F32)], axis=1)
    wq = mla_w_uq[i].reshape(C_Q_RANK, C_HEADS, C_NOPE + C_ROPE)
    wq_nope, wq_rope = wq[..., :C_NOPE], wq[..., C_NOPE:]
    wkv = mla_w_ukv[i].reshape(C_KV_RANK, C_HEADS, C_NOPE + C_V)
    wk_nope, wv = wkv[..., :C_NOPE], wkv[..., C_NOPE:]
    zeros_rope = jnp.zeros((C_KV_RANK, C_HEADS, C_ROPE), F32)
    tri = jnp.tril(jnp.ones((B_CHUNK, B_CHUNK), F32))
    head_id = np.arange(A_WIDTH) // A_DK
    ones_bd = jnp.asarray(head_id[:, None] == head_id[None, :], BF16)
    wo = w_out[i]
    wgu = w_gate_up[i]
    return dict(
        w_hg=wi[:, :o_hg].astype(BF16), w_sg=wi[:, o_hg:o_sg].astype(BF16),
        w_cq=wi[:, o_sg:o_kv].astype(BF16), w_ckv=wi[:, o_kv:o_kr].astype(BF16),
        w_krl=lane_pad(w_kr).astype(BF16), w_krr=lane_pad(_rot_half(w_kr)).astype(BF16),
        gq=mla_q_norm_g[i].reshape(1, -1), gkv=mla_kv_norm_g[i].reshape(1, -1),
        wq_l=_head_pad_cols(wq_nope, wq_rope).astype(BF16),
        wq_r=_head_pad_cols(jnp.zeros_like(wq_nope), _rot_half(wq_rope)).astype(BF16),
        wk=_head_pad_cols(wk_nope, zeros_rope).astype(BF16),
        wv=wv.reshape(C_KV_RANK, C_WIDTH).T.astype(BF16),
        lb=lower_bounds[i].reshape(1, -1), hgrn_g=hgrn_norm_g[i].reshape(1, -1),
        sgu_g=sgu_ln_g[i].reshape(1, -1), sgu_b=sgu_ln_b[i].reshape(1, -1),
        ws=_pos_major(_pos_major(sgu_w_s[i] * tri, 1), 2).astype(BF16),
        bs=_pos_major(jnp.repeat(sgu_b_s[i].T, B_CH, axis=1), 0),
        ones_bd=ones_bd,
        w_out_ab=wo[:A_WIDTH + B_WIDTH].astype(BF16), w_out_c=wo[A_WIDTH + B_WIDTH:].astype(BF16),
        ln1_g=ln1_g[i].reshape(1, -1), ln1_b=ln1_b[i].reshape(1, -1),
        wg=wgu[:, :D_FF].astype(BF16), wu=wgu[:, D_FF:].astype(BF16), wd=w_down[i].astype(BF16),
        wpg=ple_w_gate[i].astype(BF16), wpp=ple_w_proj[i].astype(BF16),
        ln2_g=ln2_g[i].reshape(1, -1), ln2_b=ln2_b[i].reshape(1, -1),
    )


def kernel(x, p, positions, ln_in_g, ln_in_b, w_in, hgrn_lb_logits, hgrn_norm_g, sgu_ln_g, sgu_ln_b, sgu_w_s, sgu_b_s, mla_q_norm_g, mla_w_uq, mla_kv_norm_g, mla_w_ukv, w_out, ln1_g, ln1_b, w_gate_up, w_down, ple_w_gate, ple_w_proj, ln2_g, ln2_b):
    bsz, seq, d = x.shape
    t = bsz * seq
    lb_cum = jnp.cumsum(jax.nn.softmax(hgrn_lb_logits.astype(F32), axis=0), axis=0)
    lower_bounds = lb_cum - lb_cum[0]
    cos_pat, sin_pat = _rope_tables(positions)

    h = _ln_in(x.reshape(t, d), ln_in_g, ln_in_b)
    for i in range(DEPTH):
        lw = _prep_layer(i, lower_bounds, w_in, hgrn_norm_g, sgu_ln_g, sgu_ln_b, sgu_w_s, sgu_b_s,
                         mla_q_norm_g, mla_w_uq, mla_kv_norm_g, mla_w_ukv, w_out, ln1_g, ln1_b,
                         w_gate_up, w_down, ple_w_gate, ple_w_proj, ln2_g, ln2_b)
        hg, sg, q, k, v = _in_proj(h, cos_pat, sin_pat, lw)
        o_ab = _mixers(hg, sg, lw, bsz, seq)
        o_c = _attention(q, k, v, bsz, seq)
        h = _out_proj(o_ab, o_c, h, lw)
        h = _ffn(h, p[i].reshape(t, PLE_DIM), lw)
    return h.reshape(bsz, seq, d)
```

```python
import functools
import math

import numpy as np
import jax
import jax.numpy as jnp
from jax import lax
from jax.experimental import pallas as pl
from jax.experimental.pallas import tpu as pltpu

F32 = jnp.float32
BF16 = jnp.bfloat16

D_MODEL = 1024
DEPTH = 2
PLE_DIM = 256

A_WIDTH = 256
A_DK = 64
A_HEADS = 4
A_CHUNK = 16
B_WIDTH = 256
B_CH = 64
B_GROUPS = 4
B_CHUNK = 128
C_WIDTH = 512
C_NOPE = 64
C_ROPE = 32
C_V = 64
C_HEADS = 8
C_Q_RANK = 384
C_KV_RANK = 256
ROPE_THETA = 10000.0
HEAD_PAD = 128

D_FF = 2816
LN_EPS = 1e-5
RMS_EPS = 1e-6
DEEPNORM_ALPHA = (2 * DEPTH) ** 0.25
ATT_SCALE = (C_NOPE + C_ROPE) ** -0.5
Q_SCALE = ATT_SCALE * math.log2(math.e)

MIX_BLOCK = 256
MIX_NCHUNK = MIX_BLOCK // A_CHUNK
ATT_TQ = 512
ATT_TK = 256
ROW_BLOCK = 512
FF_CHUNK = 256
VMEM_LIMIT = 56 * 1024 * 1024
NEG_BIG = -1e30


def _dot(a, b):
    return jnp.dot(a, b, preferred_element_type=F32)


def _dot_nt(a, b):
    return lax.dot_general(a, b, (((1,), (1,)), ((), ())), preferred_element_type=F32)


def _dot_tn(a, b):
    return lax.dot_general(a, b, (((0,), (0,)), ((), ())), preferred_element_type=F32)


def _layer_norm(x, g, b):
    mu = jnp.mean(x, axis=-1, keepdims=True)
    xc = x - mu
    var = jnp.mean(xc * xc, axis=-1, keepdims=True)
    return xc * lax.rsqrt(var + LN_EPS) * g + b


def _rms_norm(x, g):
    return x * lax.rsqrt(jnp.mean(x * x, axis=-1, keepdims=True) + RMS_EPS) * g


def _sigmoid(x):
    return 1.0 / (1.0 + jnp.exp(-x))


def _const_spec(shape):
    nd = len(shape)
    return pl.BlockSpec(shape, lambda *_: (0,) * nd)


def _layer_spec(arr, layer, single_buffer=False):
    shape = arr.shape[1:]
    nd = len(shape)
    kw = dict(pipeline_mode=pl.Buffered(1)) if single_buffer else {}
    return pl.BlockSpec((None,) + shape, lambda *_: (layer,) + (0,) * nd, **kw)


def _params(sem):
    return pltpu.CompilerParams(dimension_semantics=sem, vmem_limit_bytes=VMEM_LIMIT)


def _rope_table_kernel(pos_ref, freq_ref, cos_ref, sin_ref):
    ang = pos_ref[...] * freq_ref[...]
    cos_ref[...] = jnp.cos(ang)
    sin_ref[...] = jnp.sin(ang)


def _rope_tables(positions):
    t = positions.size
    half = C_ROPE // 2
    per_row = 128 // half
    inv_freq = ROPE_THETA ** (-jnp.arange(0, C_ROPE, 2, dtype=F32) / C_ROPE)
    pos_rep = jnp.repeat(positions.astype(F32).reshape(t), half).reshape(t // per_row, 128)
    freq = jnp.tile(inv_freq, per_row).reshape(1, 128)
    rows = t // per_row
    cos, sin = pl.pallas_call(
        _rope_table_kernel,
        out_shape=(jax.ShapeDtypeStruct((rows, 128), F32),) * 2,
        grid=(1,),
        in_specs=[_const_spec((rows, 128)), _const_spec((1, 128))],
        out_specs=(_const_spec((rows, 128)),) * 2,
        name="rope_tables",
    )(pos_rep, freq)
    cos = cos.reshape(t, half)
    sin = sin.reshape(t, half)
    ones = jnp.ones((t, C_NOPE), F32)
    zeros = jnp.zeros((t, C_NOPE), F32)
    pad1 = jnp.ones((t, HEAD_PAD - C_NOPE - C_ROPE), F32)
    pad0 = jnp.zeros((t, HEAD_PAD - C_NOPE - C_ROPE), F32)
    cos_pat = jnp.concatenate([ones, cos, cos, pad1], axis=1)
    sin_pat = jnp.concatenate([zeros, sin, sin, pad0], axis=1)
    return cos_pat, sin_pat


def _read_pos_major(ref):
    return jnp.concatenate([ref[blk, :, p, :] for blk in range(ref.shape[0]) for p in range(A_CHUNK)], axis=0)


def _write_token_major(ref, val):
    for blk in range(ref.shape[0]):
        for p in range(A_CHUNK):
            r0 = blk * MIX_BLOCK + p * MIX_NCHUNK
            ref[blk, :, p, :] = val[r0:r0 + MIX_NCHUNK]


def _token_blocks(x2d):
    t, wd = x2d.shape
    return x2d.reshape(t // MIX_BLOCK, MIX_NCHUNK, A_CHUNK, wd)


def _token_block_spec(wd, first_block=0):
    nblk = ROW_BLOCK // MIX_BLOCK
    return pl.BlockSpec((nblk, MIX_NCHUNK, A_CHUNK, wd), lambda i: (first_block + i, 0, 0, 0))


def _ln_kernel(x_ref, g_ref, b_ref, o_ref):
    o_ref[...] = _layer_norm(_read_pos_major(x_ref), g_ref[...], b_ref[...])


def _ln_in(x2d, g, b):
    t, d = x2d.shape
    return pl.pallas_call(
        _ln_kernel,
        out_shape=jax.ShapeDtypeStruct((t, d), F32),
        grid=(t // ROW_BLOCK,),
        in_specs=[_token_block_spec(d), _const_spec((1, d)), _const_spec((1, d))],
        out_specs=pl.BlockSpec((ROW_BLOCK, d), lambda i: (i, 0)),
        compiler_params=_params(("parallel",)),
        name="ln_in",
    )(_token_blocks(x2d), g.reshape(1, d), b.reshape(1, d))


def _inproj_kernel(h_ref, cos_ref, sin_ref, w_hg, w_sg, w_cq, w_ckv, w_krl, w_krr,
                   gq_ref, wq_l, wq_r, gkv_ref, wk_ref, wv_ref,
                   hg_out, sg_out, q_out, k_out, v_out):
    hb = h_ref[...].astype(BF16)
    hg_out[...] = _dot(hb, w_hg[...])
    sg_out[...] = _dot(hb, w_sg[...])
    cos = cos_ref[...]
    sin = sin_ref[...]

    cqn = _rms_norm(_dot(hb, w_cq[...]), gq_ref[...]).astype(BF16)
    q_lin = _dot(cqn, wq_l[...])
    q_rot = _dot(cqn, wq_r[...])
    for h in range(C_HEADS):
        sl = slice(h * HEAD_PAD, (h + 1) * HEAD_PAD)
        q_out[:, sl] = ((q_lin[:, sl] * cos + q_rot[:, sl] * sin) * Q_SCALE).astype(BF16)

    ckvn = _rms_norm(_dot(hb, w_ckv[...]), gkv_ref[...]).astype(BF16)
    k_lin = _dot(ckvn, wk_ref[...])
    vt = _dot_nt(wv_ref[...], ckvn).astype(BF16)
    for c in range(v_out.shape[0]):
        v_out[c] = vt[:, c * ATT_TK:(c + 1) * ATT_TK]
    k_rope = _dot(hb, w_krl[...]) * cos + _dot(hb, w_krr[...]) * sin
    for h in range(C_HEADS):
        sl = slice(h * HEAD_PAD, (h + 1) * HEAD_PAD)
        k_out[:, sl] = (k_lin[:, sl] + k_rope).astype(BF16)


def _in_proj(h, cos_pat, sin_pat, sw, layer):
    t = h.shape[0]
    tm = ROW_BLOCK
    row = lambda w: pl.BlockSpec((tm, w), lambda i: (i, 0))
    weights = [sw[n] for n in ("w_hg", "w_sg", "w_cq", "w_ckv", "w_krl", "w_krr",
                               "gq", "wq_l", "wq_r", "gkv", "wk", "wv")]
    hp = C_HEADS * HEAD_PAD
    return pl.pallas_call(
        _inproj_kernel,
        out_shape=(jax.ShapeDtypeStruct((t, 4 * A_WIDTH), F32),
                   jax.ShapeDtypeStruct((t, 2 * B_WIDTH), F32),
                   jax.ShapeDtypeStruct((t, hp), BF16),
                   jax.ShapeDtypeStruct((t, hp), BF16),
                   jax.ShapeDtypeStruct((t // ATT_TK, C_WIDTH, ATT_TK), BF16)),
        grid=(t // tm,),
        in_specs=[row(D_MODEL), row(HEAD_PAD), row(HEAD_PAD)] + [_layer_spec(w, layer) for w in weights],
        out_specs=(row(4 * A_WIDTH), row(2 * B_WIDTH), row(hp), row(hp),
                   pl.BlockSpec((tm // ATT_TK, C_WIDTH, ATT_TK), lambda i: (i, 0, 0))),
        compiler_params=_params(("parallel",)),
        name="in_proj",
    )(h, cos_pat, sin_pat, *weights)


def _erf(x):
    return lax.erf(x)


def _gelu(x):
    return 0.5 * x * (1.0 + _erf(x * (2.0 ** -0.5)))


def _rows_to_array(rows, like):
    zero = jnp.zeros_like(like)
    return jnp.concatenate([zero if r is None else r for r in rows], axis=0)


def _mixer_kernel(hg_ref, sg_ref, lb_ref, ng_ref, lng_ref, lnb_ref, ws_ref, bs_ref, ones_ref,
                  o_ref,
                  q_s, k_s, f_s, v_s, qt_s, kt_s, oi_s, a_s, b_s, qs_s, ks_s, state_s):
    nc, cs, w = MIX_NCHUNK, A_CHUNK, A_WIDTH
    tb = nc * cs
    slab = lambda p: slice(p * nc, (p + 1) * nc)

    @pl.when(pl.program_id(1) == 0)
    def _():
        state_s[...] = jnp.zeros_like(state_s)

    lb = lb_ref[...]
    for p in range(cs):
        aq = hg_ref[slab(p), 0:w]
        f_p = lb + (1.0 - lb) * _sigmoid(hg_ref[slab(p), w:2 * w])
        q_s[p] = aq * _sigmoid(aq)
        f_s[p] = f_p
        k_s[p] = 1.0 - f_p
        v_s[p] = hg_ref[slab(p), 2 * w:3 * w]

    ones_bd = ones_ref[...]

    kd = []
    for p in range(cs):
        f_p = f_s[p]
        q_p = q_s[p]
        kd = [k_s[p]] + [f_p * x for x in kd]
        xs = jnp.concatenate([q_p * x for x in kd], axis=0).astype(BF16)
        wgt = _dot(xs, ones_bd)
        o_p = wgt[0:nc] * v_s[p]
        for d in range(1, p + 1):
            o_p = o_p + wgt[d * nc:(d + 1) * nc] * v_s[p - d]
        oi_s[p] = o_p

    pref = f_s[0]
    qt_s[0] = q_s[0] * pref
    for p in range(1, cs):
        pref = pref * f_s[p]
        qt_s[p] = q_s[p] * pref
    chunk_decay = pref
    suf = jnp.ones_like(pref)
    kt_s[cs - 1] = k_s[cs - 1]
    for p in range(cs - 2, -1, -1):
        suf = suf * f_s[p + 1]
        kt_s[p] = k_s[p] * suf

    drow = [chunk_decay[c:c + 1, :] for c in range(nc)]
    one_row = jnp.ones_like(drow[0])
    levels = []
    m = 1
    while m < nc:
        levels.append(m)
        m *= 2
    la_arrays, lb_arrays = [], []
    for m in levels:
        la = [None] * nc
        lbr = [None] * nc
        for base in range(0, nc, 2 * m):
            mid = base + m
            la[mid] = one_row
            for i in range(mid + 1, mid + m):
                la[i] = la[i - 1] * drow[i - 1]
            lbr[mid - 1] = one_row
            for j in range(mid - 2, base - 1, -1):
                lbr[j] = lbr[j + 1] * drow[j + 1]
        la_arrays.append(_rows_to_array(la, one_row))
        lb_arrays.append(_rows_to_array(lbr, one_row))
    ep = [one_row]
    for c in range(1, nc):
        ep.append(ep[-1] * drow[c - 1])
    es = [one_row] * nc
    for c in range(nc - 2, -1, -1):
        es[c] = es[c + 1] * drow[c + 1]
    total_decay = ep[-1] * drow[nc - 1]
    ep_arr = jnp.concatenate(ep, axis=0)
    es_arr = jnp.concatenate(es, axis=0)

    for p in range(cs):
        qt_p = qt_s[p]
        kt_p = kt_s[p]
        for li in range(len(levels)):
            a_s[li, p] = qt_p * la_arrays[li]
            b_s[li, p] = kt_p * lb_arrays[li]
        qs_s[p] = qt_p * ep_arr
        ks_s[p] = kt_p * es_arr

    lane = lax.broadcasted_iota(jnp.int32, (tb, w), 1)
    head_masks = [(lane >= h * A_DK) & (lane < (h + 1) * A_DK) for h in range(A_HEADS)]
    row_c = lax.broadcasted_iota(jnp.int32, (tb, tb), 0) % nc
    col_c = lax.broadcasted_iota(jnp.int32, (tb, tb), 1) % nc
    v_bf = v_s[...].reshape(tb, w).astype(BF16)
    a_tok = [a_s[li].reshape(tb, w) for li in range(len(levels))]
    b_tok = [b_s[li].reshape(tb, w).astype(BF16) for li in range(len(levels))]
    o_cross = jnp.zeros((tb, w), F32)
    for h in range(A_HEADS):
        sc = None
        for li, m in enumerate(levels):
            a_h = jnp.where(head_masks[h], a_tok[li], 0.0).astype(BF16)
            s_l = _dot_nt(a_h, b_tok[li])
            if 2 * m < nc:
                s_l = jnp.where((row_c // (2 * m)) == (col_c // (2 * m)), s_l, 0.0)
            sc = s_l if sc is None else sc + s_l
        o_h = _dot(sc.astype(BF16), v_bf)
        o_cross = o_cross + jnp.where(head_masks[h], o_h, 0.0)

    st = state_s[...]
    o_state = _dot_nt(qs_s[...].reshape(tb, w).astype(BF16), st.astype(BF16))
    kv = _dot_tn(v_bf, ks_s[...].reshape(tb, w).astype(BF16))
    rr = lax.broadcasted_iota(jnp.int32, (w, w), 0)
    cc = lax.broadcasted_iota(jnp.int32, (w, w), 1)
    state_s[...] = st * total_decay + jnp.where((rr // A_DK) == (cc // A_DK), kv, 0.0)

    o = oi_s[...].reshape(tb, w) + o_cross + o_state
    ms = _dot((o * o).astype(BF16), ones_bd) * (1.0 / A_DK)
    ag = hg_ref[:, 3 * w:4 * w]
    o_ref[:, 0:w] = (o * lax.rsqrt(ms + RMS_EPS) * ng_ref[...] * (ag * _sigmoid(ag))).astype(o_ref.dtype)

    cps = B_CHUNK // cs
    u = _gelu(sg_ref[:, 0:B_WIDTH])
    vv = _layer_norm(_gelu(sg_ref[:, B_WIDTH:2 * B_WIDTH]), lng_ref[...], lnb_ref[...])
    lane_b = lax.broadcasted_iota(jnp.int32, (B_CHUNK, B_WIDTH), 1)
    zs = []
    for ci in range(nc // cps):
        v_c = jnp.concatenate([vv[p * nc + ci * cps:p * nc + (ci + 1) * cps] for p in range(cs)],
                              axis=0).astype(BF16)
        z = bs_ref[...]
        for g in range(B_GROUPS):
            z_g = _dot(ws_ref[g], v_c)
            z = z + jnp.where((lane_b >= g * B_CH) & (lane_b < (g + 1) * B_CH), z_g, 0.0)
        zs.append(z)
    z_all = jnp.concatenate([z[p * cps:(p + 1) * cps] for p in range(cs) for z in zs], axis=0)
    o_ref[:, w:w + B_WIDTH] = (u * z_all).astype(o_ref.dtype)


def _mixers(hg, sg, sw, layer, bsz, seq):
    nb = seq // MIX_BLOCK
    nc, cs, w = MIX_NCHUNK, A_CHUNK, A_WIDTH
    nlev = int(round(math.log2(nc)))
    blk = lambda width: pl.BlockSpec((MIX_BLOCK, width), lambda b, i: (b * nb + i, 0))
    pm = lambda: pltpu.VMEM((cs, nc, w), F32)
    ow = A_WIDTH + B_WIDTH
    return pl.pallas_call(
        _mixer_kernel,
        out_shape=jax.ShapeDtypeStruct((bsz * seq, ow), BF16),
        grid=(bsz, nb),
        in_specs=[blk(4 * A_WIDTH), blk(2 * B_WIDTH),
                  ] + [_layer_spec(sw[n], layer) for n in ("lb", "hgrn_g", "sgu_g", "sgu_b", "ws", "bs")] + [
                  _const_spec((w, w))],
        out_specs=blk(ow),
        scratch_shapes=[pm(), pm(), pm(), pm(), pm(), pm(), pm(),
                        pltpu.VMEM((nlev, cs, nc, w), F32), pltpu.VMEM((nlev, cs, nc, w), F32),
                        pm(), pm(), pltpu.VMEM((w, w), F32)],
        compiler_params=_params(("parallel", "arbitrary")),
        name="mixers",
    )(hg, sg, sw["lb"], sw["hgrn_g"], sw["sgu_g"], sw["sgu_b"], sw["ws"], sw["bs"], sw["ones_bd"])


def _token_offset(i):
    r = i % MIX_BLOCK
    return (i - r) + (r % MIX_NCHUNK) * A_CHUNK + r // MIX_NCHUNK


def _attn_kernel(q_ref, k_ref, vt_ref, o_ref, sa_ref, sb_ref, acc_ref):
    tq, tk = ATT_TQ, ATT_TK
    qi = pl.program_id(2)
    qs = [q_ref[:, hh * HEAD_PAD:(hh + 1) * HEAD_PAD] for hh in range(2)]

    def produce(j, dst, mask=None):
        off = pl.multiple_of(j * tk, tk)
        for hh in range(2):
            k_t = k_ref[pl.ds(off, tk), hh * HEAD_PAD:(hh + 1) * HEAD_PAD]
            st = _dot_nt(k_t, qs[hh])
            if mask is not None:
                st = jnp.where(mask, st, NEG_BIG)
            dst[hh] = st

    def consume(j, src, ml):
        new = []
        for hh in range(2):
            m, l = ml[hh]
            st = src[hh]
            m_new = jnp.maximum(m, jnp.max(st, axis=0, keepdims=True))
            alpha = jnp.exp2(m - m_new)
            pt = jnp.exp2(st - m_new)
            l = alpha * l + jnp.sum(pt, axis=0, keepdims=True)
            vt_h = vt_ref[j, hh * C_V:(hh + 1) * C_V, :]
            acc_ref[hh] = alpha * acc_ref[hh] + _dot(vt_h, pt.astype(BF16))
            new.append((m_new, l))
        return tuple(new)

    key = _token_offset(lax.broadcasted_iota(jnp.int32, (tk, tq), 0))
    qry = _token_offset(lax.broadcasted_iota(jnp.int32, (tk, tq), 1))
    d0 = 2 * qi
    acc_ref[...] = jnp.zeros_like(acc_ref)
    produce(d0, sa_ref, key <= qry)
    ml0 = tuple((jnp.full((1, tq), NEG_BIG, F32), jnp.zeros((1, tq), F32)) for _ in range(2))

    def body(p, carry):
        ml, in_a = carry
        produce(2 * p, sb_ref)
        ml = consume(in_a, sa_ref, ml)
        produce(2 * p + 1, sa_ref)
        ml = consume(2 * p, sb_ref, ml)
        return ml, 2 * p + 1

    ml, in_a = lax.fori_loop(0, qi, body, (ml0, d0))
    produce(d0 + 1, sb_ref, key + tk <= qry)
    ml = consume(in_a, sa_ref, ml)
    ml = consume(d0 + 1, sb_ref, ml)
    for hh in range(2):
        o_ref[hh * C_V:(hh + 1) * C_V, :] = (acc_ref[hh] / ml[hh][1]).astype(o_ref.dtype)


def _attention(q, k, vt, bsz, seq):
    assert ATT_TQ == 2 * ATT_TK
    hp2 = 2 * HEAD_PAD
    nq = seq // ATT_TQ
    nk = seq // ATT_TK
    return pl.pallas_call(
        _attn_kernel,
        out_shape=jax.ShapeDtypeStruct((bsz * nq, C_WIDTH, ATT_TQ), BF16),
        grid=(bsz, C_HEADS // 2, nq),
        in_specs=[pl.BlockSpec((None, ATT_TQ, hp2), lambda b, g, i: (b, i, g)),
                  pl.BlockSpec((None, seq, hp2), lambda b, g, i: (b, 0, g)),
                  pl.BlockSpec((nk, 2 * C_V, ATT_TK), lambda b, g, i: (b, g, 0))],
        out_specs=pl.BlockSpec((None, 2 * C_V, ATT_TQ), lambda b, g, i: (b * nq + i, g, 0)),
        scratch_shapes=[pltpu.VMEM((2, ATT_TK, ATT_TQ), F32), pltpu.VMEM((2, ATT_TK, ATT_TQ), F32),
                        pltpu.VMEM((2, C_V, ATT_TQ), F32)],
        compiler_params=_params(("parallel", "parallel", "arbitrary")),
        name="mla_attention",
    )(q.reshape(bsz, seq, -1), k.reshape(bsz, seq, -1), vt)


def _channel_kernel(oab_ref, oct_ref, h_ref, p_ref, wab_ref, wc_ref, g1_ref, b1_ref,
                    wg_ref, wu_ref, wd_ref, wpg_ref, wpp_ref, g2_ref, b2_ref, o_ref, act_s, *,
                    token_major_out):
    mix = _dot(oab_ref[...], wab_ref[...]) + _dot_tn(oct_ref[...], wc_ref[...])
    h = _layer_norm(DEEPNORM_ALPHA * h_ref[...] + mix, g1_ref[...], b1_ref[...])
    hb = h.astype(BF16)
    for c in range(D_FF // FF_CHUNK):
        cols = slice(c * FF_CHUNK, (c + 1) * FF_CHUNK)
        gate = _dot(hb, wg_ref[:, cols])
        up = _dot(hb, wu_ref[:, cols])
        act_s[:, cols] = (gate * _sigmoid(gate) * up).astype(BF16)
    ffn = _dot(act_s[...], wd_ref[...])
    ple = _sigmoid(_dot(hb, wpg_ref[...])) * _dot(_read_pos_major(p_ref).astype(BF16), wpp_ref[...])
    out = _layer_norm(DEEPNORM_ALPHA * h + ffn + ple, g2_ref[...], b2_ref[...])
    if token_major_out:
        _write_token_major(o_ref, out)
    else:
        o_ref[...] = out


def _channel(o_ab, o_ct, h, p, sw, layer, token_major_out):
    t = h.shape[0]
    tm = ROW_BLOCK
    row = lambda w: pl.BlockSpec((tm, w), lambda i: (i, 0))
    if token_major_out:
        out_shape = jax.ShapeDtypeStruct((t // MIX_BLOCK, MIX_NCHUNK, A_CHUNK, D_MODEL), F32)
        out_spec = _token_block_spec(D_MODEL)
    else:
        out_shape = jax.ShapeDtypeStruct((t, D_MODEL), F32)
        out_spec = row(D_MODEL)
    names = ("w_out_ab", "w_out_c", "ln1_g", "ln1_b", "wg", "wu", "wd", "wpg", "wpp", "ln2_g", "ln2_b")
    return pl.pallas_call(
        functools.partial(_channel_kernel, token_major_out=token_major_out),
        out_shape=out_shape,
        grid=(t // tm,),
        in_specs=[row(A_WIDTH + B_WIDTH), pl.BlockSpec((None, C_WIDTH, tm), lambda i: (i, 0, 0)), row(D_MODEL),
                  _token_block_spec(PLE_DIM, layer * (t // tm))] + [_layer_spec(sw[n], layer, single_buffer=True) for n in names],
        out_specs=out_spec,
        scratch_shapes=[pltpu.VMEM((tm, D_FF), BF16)],
        compiler_params=_params(("parallel",)),
        name="channel_mix",
    )(o_ab, o_ct, h, _token_blocks(p), *[sw[n] for n in names])


def _head_pad_cols(w_nope, w_rope):
    lead = w_nope.shape[:2]
    pad = jnp.zeros(lead + (C_HEADS, HEAD_PAD - C_NOPE - C_ROPE), w_nope.dtype)
    return jnp.concatenate([w_nope, w_rope, pad], axis=-1).reshape(lead + (C_HEADS * HEAD_PAD,))


def _rot_half(w_rope):
    half = C_ROPE // 2
    return jnp.concatenate([-w_rope[..., half:], w_rope[..., :half]], axis=-1)


def _pos_major(x, axis):
    cps = B_CHUNK // A_CHUNK
    shape = x.shape
    x = x.reshape(shape[:axis] + (cps, A_CHUNK) + shape[axis + 1:])
    return jnp.swapaxes(x, axis, axis + 1).reshape(shape)


def _prep_weights(lower_bounds, w_in, hgrn_norm_g, sgu_ln_g, sgu_ln_b, sgu_w_s, sgu_b_s,
                  mla_q_norm_g, mla_w_uq, mla_kv_norm_g, mla_w_ukv, w_out, ln1_g, ln1_b,
                  w_gate_up, w_down, ple_w_gate, ple_w_proj, ln2_g, ln2_b):
    nl = w_in.shape[0]
    o_hg, o_sg = 4 * A_WIDTH, 4 * A_WIDTH + 2 * B_WIDTH
    o_kv = o_sg + C_Q_RANK
    o_kr = o_kv + C_KV_RANK
    w_kr = w_in[:, :, o_kr:o_kr + C_ROPE]
    lane_pad = lambda wr: jnp.concatenate(
        [jnp.zeros((nl, D_MODEL, C_NOPE), F32), wr, jnp.zeros((nl, D_MODEL, HEAD_PAD - C_NOPE - C_ROPE), F32)],
        axis=-1)
    wq = mla_w_uq.reshape(nl, C_Q_RANK, C_HEADS, C_NOPE + C_ROPE)
    wq_nope, wq_rope = wq[..., :C_NOPE], wq[..., C_NOPE:]
    wkv = mla_w_ukv.reshape(nl, C_KV_RANK, C_HEADS, C_NOPE + C_V)
    wk_nope, wv = wkv[..., :C_NOPE], wkv[..., C_NOPE:]
    zeros_rope = jnp.zeros((nl, C_KV_RANK, C_HEADS, C_ROPE), F32)
    tri = jnp.tril(jnp.ones((B_CHUNK, B_CHUNK), F32))
    head_id = np.arange(A_WIDTH) // A_DK
    vec = lambda g: g.reshape(nl, 1, -1)
    return dict(
        w_hg=w_in[:, :, :o_hg].astype(BF16), w_sg=w_in[:, :, o_hg:o_sg].astype(BF16),
        w_cq=w_in[:, :, o_sg:o_kv].astype(BF16), w_ckv=w_in[:, :, o_kv:o_kr].astype(BF16),
        w_krl=lane_pad(w_kr).astype(BF16), w_krr=lane_pad(_rot_half(w_kr)).astype(BF16),
        gq=vec(mla_q_norm_g), gkv=vec(mla_kv_norm_g),
        wq_l=_head_pad_cols(wq_nope, wq_rope).astype(BF16),
        wq_r=_head_pad_cols(jnp.zeros_like(wq_nope), _rot_half(wq_rope)).astype(BF16),
        wk=_head_pad_cols(wk_nope, zeros_rope).astype(BF16),
        wv=jnp.swapaxes(wv.reshape(nl, C_KV_RANK, C_WIDTH), 1, 2).astype(BF16),
        lb=vec(lower_bounds), hgrn_g=vec(hgrn_norm_g), sgu_g=vec(sgu_ln_g), sgu_b=vec(sgu_ln_b),
        ws=_pos_major(_pos_major(sgu_w_s * tri, 2), 3).astype(BF16),
        bs=_pos_major(jnp.repeat(jnp.swapaxes(sgu_b_s, 1, 2), B_CH, axis=2), 1),
        ones_bd=jnp.asarray(head_id[:, None] == head_id[None, :], BF16),
        w_out_ab=w_out[:, :A_WIDTH + B_WIDTH].astype(BF16), w_out_c=w_out[:, A_WIDTH + B_WIDTH:].astype(BF16),
        ln1_g=vec(ln1_g), ln1_b=vec(ln1_b),
        wg=w_gate_up[:, :, :D_FF].astype(BF16), wu=w_gate_up[:, :, D_FF:].astype(BF16),
        wd=w_down.astype(BF16), wpg=ple_w_gate.astype(BF16), wpp=ple_w_proj.astype(BF16),
        ln2_g=vec(ln2_g), ln2_b=vec(ln2_b),
    )


def kernel(x, p, positions, ln_in_g, ln_in_b, w_in, hgrn_lb_logits, hgrn_norm_g, sgu_ln_g, sgu_ln_b, sgu_w_s, sgu_b_s, mla_q_norm_g, mla_w_uq, mla_kv_norm_g, mla_w_ukv, w_out, ln1_g, ln1_b, w_gate_up, w_down, ple_w_gate, ple_w_proj, ln2_g, ln2_b):
    bsz, seq, d = x.shape
    t = bsz * seq
    lb_cum = jnp.cumsum(jax.nn.softmax(hgrn_lb_logits.astype(F32), axis=0), axis=0)
    lower_bounds = lb_cum - lb_cum[0]
    pos_pm = positions.reshape(bsz, seq // MIX_BLOCK, MIX_NCHUNK, A_CHUNK).swapaxes(2, 3).reshape(bsz, seq)
    cos_pat, sin_pat = _rope_tables(pos_pm)
    sw = _prep_weights(lower_bounds, w_in, hgrn_norm_g, sgu_ln_g, sgu_ln_b, sgu_w_s, sgu_b_s,
                       mla_q_norm_g, mla_w_uq, mla_kv_norm_g, mla_w_ukv, w_out, ln1_g, ln1_b,
                       w_gate_up, w_down, ple_w_gate, ple_w_proj, ln2_g, ln2_b)

    h = _ln_in(x.reshape(t, d), ln_in_g, ln_in_b)
    for i in range(DEPTH):
        hg, sg, q, k, v = _in_proj(h, cos_pat, sin_pat, sw, i)
        o_ab = _mixers(hg, sg, sw, i, bsz, seq)
        o_c = _attention(q, k, v, bsz, seq)
        h = _channel(o_ab, o_c, h, p.reshape(DEPTH * t, PLE_DIM), sw, i, token_major_out=(i == DEPTH - 1))
    return h.reshape(bsz, seq, d)
```

```python
import functools
import math

import numpy as np
import jax
import jax.numpy as jnp
from jax import lax
from jax.experimental import pallas as pl
from jax.experimental.pallas import tpu as pltpu

F32 = jnp.float32
BF16 = jnp.bfloat16

D_MODEL = 1024
DEPTH = 2
PLE_DIM = 256

A_WIDTH = 256
A_DK = 64
A_HEADS = 4
A_CHUNK = 16
B_WIDTH = 256
B_CH = 64
B_GROUPS = 4
B_CHUNK = 128
C_WIDTH = 512
C_NOPE = 64
C_ROPE = 32
C_V = 64
C_HEADS = 8
C_Q_RANK = 384
C_KV_RANK = 256
ROPE_THETA = 10000.0
HEAD_PAD = 128

D_FF = 2816
LN_EPS = 1e-5
RMS_EPS = 1e-6
DEEPNORM_ALPHA = (2 * DEPTH) ** 0.25
ATT_SCALE = (C_NOPE + C_ROPE) ** -0.5
Q_SCALE = ATT_SCALE * math.log2(math.e)

MIX_BLOCK = 256
MIX_NCHUNK = MIX_BLOCK // A_CHUNK
ATT_TQ = 512
ATT_TK = 256
ATT_HEADS = 4
ATT_ONES = 16
ROW_BLOCK = 512
FF_CHUNK = 256
VMEM_LIMIT = 56 * 1024 * 1024
NEG_BIG = -1e30


def _dot(a, b):
    return jnp.dot(a, b, preferred_element_type=F32)


def _dot_nt(a, b):
    return lax.dot_general(a, b, (((1,), (1,)), ((), ())), preferred_element_type=F32)


def _dot_tn(a, b):
    return lax.dot_general(a, b, (((0,), (0,)), ((), ())), preferred_element_type=F32)


def _layer_norm(x, g, b):
    mu = jnp.mean(x, axis=-1, keepdims=True)
    xc = x - mu
    var = jnp.mean(xc * xc, axis=-1, keepdims=True)
    return xc * lax.rsqrt(var + LN_EPS) * g + b


def _rms_norm(x, g):
    return x * lax.rsqrt(jnp.mean(x * x, axis=-1, keepdims=True) + RMS_EPS) * g


def _sigmoid(x):
    return 1.0 / (1.0 + jnp.exp(-x))


def _const_spec(shape):
    nd = len(shape)
    return pl.BlockSpec(shape, lambda *_: (0,) * nd)


def _layer_spec(arr, layer, single_buffer=False):
    shape = arr.shape[1:]
    nd = len(shape)
    kw = dict(pipeline_mode=pl.Buffered(1)) if single_buffer else {}
    return pl.BlockSpec((None,) + shape, lambda *_: (layer,) + (0,) * nd, **kw)


def _params(sem):
    return pltpu.CompilerParams(dimension_semantics=sem, vmem_limit_bytes=VMEM_LIMIT)


def _rope_table_kernel(pos_ref, freq_ref, cos_ref, sin_ref):
    ang = pos_ref[...] * freq_ref[...]
    cos_ref[...] = jnp.cos(ang)
    sin_ref[...] = jnp.sin(ang)


def _rope_tables(positions):
    t = positions.size
    half = C_ROPE // 2
    per_row = 128 // half
    inv_freq = ROPE_THETA ** (-jnp.arange(0, C_ROPE, 2, dtype=F32) / C_ROPE)
    pos_rep = jnp.repeat(positions.astype(F32).reshape(t), half).reshape(t // per_row, 128)
    freq = jnp.tile(inv_freq, per_row).reshape(1, 128)
    rows = t // per_row
    cos, sin = pl.pallas_call(
        _rope_table_kernel,
        out_shape=(jax.ShapeDtypeStruct((rows, 128), F32),) * 2,
        grid=(1,),
        in_specs=[_const_spec((rows, 128)), _const_spec((1, 128))],
        out_specs=(_const_spec((rows, 128)),) * 2,
        name="rope_tables",
    )(pos_rep, freq)
    cos = cos.reshape(t, half)
    sin = sin.reshape(t, half)
    ones = jnp.ones((t, C_NOPE), F32)
    zeros = jnp.zeros((t, C_NOPE), F32)
    zh = jnp.zeros((t, half), F32)
    pad1 = jnp.ones((t, HEAD_PAD - C_NOPE - C_ROPE), F32)
    pad0 = jnp.zeros((t, HEAD_PAD - C_NOPE - C_ROPE), F32)
    cos_pat = jnp.concatenate([ones, cos, cos, pad1], axis=1)
    sin_x1 = jnp.concatenate([zeros, -sin, zh, pad0], axis=1)
    sin_x2 = jnp.concatenate([zeros, zh, sin, pad0], axis=1)
    return cos_pat, sin_x1, sin_x2


def _read_pos_major(ref):
    return jnp.concatenate([ref[blk, :, p, :] for blk in range(ref.shape[0]) for p in range(A_CHUNK)], axis=0)


def _write_token_major(ref, val):
    for blk in range(ref.shape[0]):
        for p in range(A_CHUNK):
            r0 = blk * MIX_BLOCK + p * MIX_NCHUNK
            ref[blk, :, p, :] = val[r0:r0 + MIX_NCHUNK]


def _token_blocks(x2d):
    t, wd = x2d.shape
    return x2d.reshape(t // MIX_BLOCK, MIX_NCHUNK, A_CHUNK, wd)


def _token_block_spec(wd, first_block=0):
    nblk = ROW_BLOCK // MIX_BLOCK
    return pl.BlockSpec((nblk, MIX_NCHUNK, A_CHUNK, wd), lambda i: (first_block + i, 0, 0, 0))


def _ln_kernel(x_ref, g_ref, b_ref, o_ref):
    o_ref[...] = _layer_norm(_read_pos_major(x_ref), g_ref[...], b_ref[...])


def _ln_in(x2d, g, b):
    t, d = x2d.shape
    return pl.pallas_call(
        _ln_kernel,
        out_shape=jax.ShapeDtypeStruct((t, d), F32),
        grid=(t // ROW_BLOCK,),
        in_specs=[_token_block_spec(d), _const_spec((1, d)), _const_spec((1, d))],
        out_specs=pl.BlockSpec((ROW_BLOCK, d), lambda i: (i, 0)),
        compiler_params=_params(("parallel",)),
        name="ln_in",
    )(_token_blocks(x2d), g.reshape(1, d), b.reshape(1, d))


def _rope_lanes(x, cos, sin_x1, sin_x2):
    half = C_ROPE // 2
    right = pltpu.roll(x, HEAD_PAD - half, axis=1)
    left = pltpu.roll(x, half, axis=1)
    return x * cos + right * sin_x1 + left * sin_x2


def _inproj_kernel(h_ref, cos_ref, sin1_ref, sin2_ref, w_hg, w_sg, w_lat,
                   gq_ref, wq_ref, gkv_ref, wk_ref, wv_ref,
                   hg_out, sg_out, q_out, k_out, v_out):
    hb = h_ref[...].astype(BF16)
    hg_out[...] = _dot(hb, w_hg[...])
    sg_out[...] = _dot(hb, w_sg[...])
    lat = _dot(hb, w_lat[...])
    cos, sin1, sin2 = cos_ref[...], sin1_ref[...], sin2_ref[...]

    cqn = _rms_norm(lat[:, 0:C_Q_RANK], gq_ref[...]).astype(BF16)
    q_lin = _dot(cqn, wq_ref[...])
    for h in range(C_HEADS):
        sl = slice(h * HEAD_PAD, (h + 1) * HEAD_PAD)
        q_out[:, sl] = (_rope_lanes(q_lin[:, sl], cos, sin1, sin2) * Q_SCALE).astype(BF16)

    ckvn = _rms_norm(lat[:, C_Q_RANK:C_Q_RANK + C_KV_RANK], gkv_ref[...]).astype(BF16)
    k_lin = _dot(ckvn, wk_ref[...])
    vt = _dot_nt(wv_ref[...], ckvn).astype(BF16)
    for c in range(v_out.shape[0]):
        v_out[c] = vt[:, c * ATT_TK:(c + 1) * ATT_TK]
    k_rope = _rope_lanes(lat[:, C_Q_RANK + C_KV_RANK:], cos, sin1, sin2)
    for h in range(C_HEADS):
        sl = slice(h * HEAD_PAD, (h + 1) * HEAD_PAD)
        k_out[:, sl] = (k_lin[:, sl] + k_rope).astype(BF16)


def _in_proj(h, rope_pats, sw, layer):
    t = h.shape[0]
    tm = ROW_BLOCK
    row = lambda w: pl.BlockSpec((tm, w), lambda i: (i, 0))
    weights = [sw[n] for n in ("w_hg", "w_sg", "w_lat", "gq", "wq", "gkv", "wk", "wv")]
    hp = C_HEADS * HEAD_PAD
    return pl.pallas_call(
        _inproj_kernel,
        out_shape=(jax.ShapeDtypeStruct((t, 4 * A_WIDTH), F32),
                   jax.ShapeDtypeStruct((t, 2 * B_WIDTH), F32),
                   jax.ShapeDtypeStruct((t, hp), BF16),
                   jax.ShapeDtypeStruct((t, hp), BF16),
                   jax.ShapeDtypeStruct((t // ATT_TK, C_WIDTH, ATT_TK), BF16)),
        grid=(t // tm,),
        in_specs=[row(D_MODEL)] + [row(HEAD_PAD)] * 3 + [_layer_spec(w, layer) for w in weights],
        out_specs=(row(4 * A_WIDTH), row(2 * B_WIDTH), row(hp), row(hp),
                   pl.BlockSpec((tm // ATT_TK, C_WIDTH, ATT_TK), lambda i: (i, 0, 0))),
        compiler_params=_params(("parallel",)),
        name="in_proj",
    )(h, *rope_pats, *weights)


def _erf(x):
    return lax.erf(x)


def _gelu(x):
    return 0.5 * x * (1.0 + _erf(x * (2.0 ** -0.5)))


def _rows_to_array(rows, like):
    zero = jnp.zeros_like(like)
    return jnp.concatenate([zero if r is None else r for r in rows], axis=0)


def _mixer_kernel(hg_ref, sg_ref, lb_ref, ng_ref, lng_ref, lnb_ref, ws_ref, bs_ref, ones_ref,
                  o_ref,
                  q_s, k_s, f_s, v_s, qt_s, kt_s, oi_s, a_s, b_s, qs_s, ks_s, state_s):
    nc, cs, w = MIX_NCHUNK, A_CHUNK, A_WIDTH
    tb = nc * cs
    slab = lambda p: slice(p * nc, (p + 1) * nc)

    @pl.when(pl.program_id(1) == 0)
    def _():
        state_s[...] = jnp.zeros_like(state_s)

    lb = lb_ref[...]
    for p in range(cs):
        aq = hg_ref[slab(p), 0:w]
        f_p = lb + (1.0 - lb) * _sigmoid(hg_ref[slab(p), w:2 * w])
        q_s[p] = aq * _sigmoid(aq)
        f_s[p] = f_p
        k_s[p] = 1.0 - f_p
        v_s[p] = hg_ref[slab(p), 2 * w:3 * w]

    ones_bd = ones_ref[...]

    kd = []
    for p in range(cs):
        f_p = f_s[p]
        q_p = q_s[p]
        kd = [k_s[p]] + [f_p * x for x in kd]
        xs = jnp.concatenate([q_p * x for x in kd], axis=0).astype(BF16)
        wgt = _dot(xs, ones_bd)
        o_p = wgt[0:nc] * v_s[p]
        for d in range(1, p + 1):
            o_p = o_p + wgt[d * nc:(d + 1) * nc] * v_s[p - d]
        oi_s[p] = o_p

    pref = f_s[0]
    qt_s[0] = q_s[0] * pref
    for p in range(1, cs):
        pref = pref * f_s[p]
        qt_s[p] = q_s[p] * pref
    chunk_decay = pref
    suf = jnp.ones_like(pref)
    kt_s[cs - 1] = k_s[cs - 1]
    for p in range(cs - 2, -1, -1):
        suf = suf * f_s[p + 1]
        kt_s[p] = k_s[p] * suf

    drow = [chunk_decay[c:c + 1, :] for c in range(nc)]
    one_row = jnp.ones_like(drow[0])
    levels = []
    m = 1
    while m < nc:
        levels.append(m)
        m *= 2
    la_arrays, lb_arrays = [], []
    for m in levels:
        la = [None] * nc
        lbr = [None] * nc
        for base in range(0, nc, 2 * m):
            mid = base + m
            la[mid] = one_row
            for i in range(mid + 1, mid + m):
                la[i] = la[i - 1] * drow[i - 1]
            lbr[mid - 1] = one_row
            for j in range(mid - 2, base - 1, -1):
                lbr[j] = lbr[j + 1] * drow[j + 1]
        la_arrays.append(_rows_to_array(la, one_row))
        lb_arrays.append(_rows_to_array(lbr, one_row))
    ep = [one_row]
    for c in range(1, nc):
        ep.append(ep[-1] * drow[c - 1])
    es = [one_row] * nc
    for c in range(nc - 2, -1, -1):
        es[c] = es[c + 1] * drow[c + 1]
    total_decay = ep[-1] * drow[nc - 1]
    ep_arr = jnp.concatenate(ep, axis=0)
    es_arr = jnp.concatenate(es, axis=0)

    for p in range(cs):
        qt_p = qt_s[p]
        kt_p = kt_s[p]
        for li in range(len(levels)):
            a_s[li, p] = qt_p * la_arrays[li]
            b_s[li, p] = kt_p * lb_arrays[li]
        qs_s[p] = qt_p * ep_arr
        ks_s[p] = kt_p * es_arr

    lane = lax.broadcasted_iota(jnp.int32, (tb, w), 1)
    head_masks = [(lane >= h * A_DK) & (lane < (h + 1) * A_DK) for h in range(A_HEADS)]
    row_c = lax.broadcasted_iota(jnp.int32, (tb, tb), 0) % nc
    col_c = lax.broadcasted_iota(jnp.int32, (tb, tb), 1) % nc
    v_bf = v_s[...].reshape(tb, w).astype(BF16)
    a_tok = [a_s[li].reshape(tb, w) for li in range(len(levels))]
    b_tok = [b_s[li].reshape(tb, w).astype(BF16) for li in range(len(levels))]
    o_cross = jnp.zeros((tb, w), F32)
    for h in range(A_HEADS):
        sc = None
        for li, m in enumerate(levels):
            a_h = jnp.where(head_masks[h], a_tok[li], 0.0).astype(BF16)
            s_l = _dot_nt(a_h, b_tok[li])
            if 2 * m < nc:
                s_l = jnp.where((row_c // (2 * m)) == (col_c // (2 * m)), s_l, 0.0)
            sc = s_l if sc is None else sc + s_l
        o_h = _dot(sc.astype(BF16), v_bf)
        o_cross = o_cross + jnp.where(head_masks[h], o_h, 0.0)

    st = state_s[...]
    o_state = _dot_nt(qs_s[...].reshape(tb, w).astype(BF16), st.astype(BF16))
    kv = _dot_tn(v_bf, ks_s[...].reshape(tb, w).astype(BF16))
    rr = lax.broadcasted_iota(jnp.int32, (w, w), 0)
    cc = lax.broadcasted_iota(jnp.int32, (w, w), 1)
    state_s[...] = st * total_decay + jnp.where((rr // A_DK) == (cc // A_DK), kv, 0.0)

    o = oi_s[...].reshape(tb, w) + o_cross + o_state
    ms = _dot((o * o).astype(BF16), ones_bd) * (1.0 / A_DK)
    ag = hg_ref[:, 3 * w:4 * w]
    o_ref[:, 0:w] = (o * lax.rsqrt(ms + RMS_EPS) * ng_ref[...] * (ag * _sigmoid(ag))).astype(o_ref.dtype)

    cps = B_CHUNK // cs
    u = _gelu(sg_ref[:, 0:B_WIDTH])
    vv = _layer_norm(_gelu(sg_ref[:, B_WIDTH:2 * B_WIDTH]), lng_ref[...], lnb_ref[...])
    lane_b = lax.broadcasted_iota(jnp.int32, (B_CHUNK, B_WIDTH), 1)
    zs = []
    for ci in range(nc // cps):
        v_c = jnp.concatenate([vv[p * nc + ci * cps:p * nc + (ci + 1) * cps] for p in range(cs)],
                              axis=0).astype(BF16)
        z = bs_ref[...]
        for g in range(B_GROUPS):
            z_g = _dot(ws_ref[g], v_c)
            z = z + jnp.where((lane_b >= g * B_CH) & (lane_b < (g + 1) * B_CH), z_g, 0.0)
        zs.append(z)
    z_all = jnp.concatenate([z[p * cps:(p + 1) * cps] for p in range(cs) for z in zs], axis=0)
    o_ref[:, w:w + B_WIDTH] = (u * z_all).astype(o_ref.dtype)


def _mixers(hg, sg, sw, layer, bsz, seq):
    nb = seq // MIX_BLOCK
    nc, cs, w = MIX_NCHUNK, A_CHUNK, A_WIDTH
    nlev = int(round(math.log2(nc)))
    blk = lambda width: pl.BlockSpec((MIX_BLOCK, width), lambda b, i: (b * nb + i, 0))
    pm = lambda: pltpu.VMEM((cs, nc, w), F32)
    ow = A_WIDTH + B_WIDTH
    return pl.pallas_call(
        _mixer_kernel,
        out_shape=jax.ShapeDtypeStruct((bsz * seq, ow), BF16),
        grid=(bsz, nb),
        in_specs=[blk(4 * A_WIDTH), blk(2 * B_WIDTH),
                  ] + [_layer_spec(sw[n], layer) for n in ("lb", "hgrn_g", "sgu_g", "sgu_b", "ws", "bs")] + [
                  _const_spec((w, w))],
        out_specs=blk(ow),
        scratch_shapes=[pm(), pm(), pm(), pm(), pm(), pm(), pm(),
                        pltpu.VMEM((nlev, cs, nc, w), F32), pltpu.VMEM((nlev, cs, nc, w), F32),
                        pm(), pm(), pltpu.VMEM((w, w), F32)],
        compiler_params=_params(("parallel", "arbitrary")),
        name="mixers",
    )(hg, sg, sw["lb"], sw["hgrn_g"], sw["sgu_g"], sw["sgu_b"], sw["ws"], sw["bs"], sw["ones_bd"])


def _token_offset(i):
    r = i % MIX_BLOCK
    return (i - r) + (r % MIX_NCHUNK) * A_CHUNK + r // MIX_NCHUNK


def _attn_kernel(q_ref, k_ref, vt_ref, o_ref, sa_ref, sb_ref, acca_ref, accb_ref):
    tq, tk = ATT_TQ, ATT_TK
    qi = pl.program_id(2)
    qs = [q_ref[:, hh * HEAD_PAD:(hh + 1) * HEAD_PAD] for hh in range(ATT_HEADS)]

    def produce(j, dst, mask=None):
        off = pl.multiple_of(j * tk, tk)
        tile_max = []
        for hh in range(ATT_HEADS):
            k_t = k_ref[pl.ds(off, tk), hh * HEAD_PAD:(hh + 1) * HEAD_PAD]
            st = _dot_nt(k_t, qs[hh])
            if mask is not None:
                st = jnp.where(mask, st, NEG_BIG)
            dst[hh] = st
            tile_max.append(jnp.max(st, axis=0, keepdims=True))
        return tuple(tile_max)

    ones_rows = jnp.ones((ATT_ONES, tk), BF16)

    def consume(j, src, tile_max, acc, ms):
        new = []
        for hh in range(ATT_HEADS):
            m_new = jnp.maximum(ms[hh], tile_max[hh])
            alpha = jnp.exp2(ms[hh] - m_new)
            pt = jnp.exp2(src[hh] - m_new).astype(BF16)
            vt_h = jnp.concatenate([vt_ref[j, hh * C_V:(hh + 1) * C_V, :], ones_rows], axis=0)
            acc[hh] = alpha * acc[hh] + _dot(vt_h, pt)
            new.append(m_new)
        return tuple(new)

    key = _token_offset(lax.broadcasted_iota(jnp.int32, (tk, tq), 0))
    qry = _token_offset(lax.broadcasted_iota(jnp.int32, (tk, tq), 1))
    d0 = 2 * qi
    acca_ref[...] = jnp.zeros_like(acca_ref)
    accb_ref[...] = jnp.zeros_like(accb_ref)
    max_a = produce(d0, sa_ref, key <= qry)
    ms0 = tuple(jnp.full((1, tq), NEG_BIG, F32) for _ in range(ATT_HEADS))

    def body(p, carry):
        ms_a, ms_b, in_a, max_a = carry
        max_b = produce(2 * p, sb_ref)
        ms_a = consume(in_a, sa_ref, max_a, acca_ref, ms_a)
        max_a = produce(2 * p + 1, sa_ref)
        ms_b = consume(2 * p, sb_ref, max_b, accb_ref, ms_b)
        return ms_a, ms_b, 2 * p + 1, max_a

    ms_a, ms_b, in_a, max_a = lax.fori_loop(0, qi, body, (ms0, ms0, d0, max_a))
    max_b = produce(d0 + 1, sb_ref, key + tk <= qry)
    ms_a = consume(in_a, sa_ref, max_a, acca_ref, ms_a)
    ms_b = consume(d0 + 1, sb_ref, max_b, accb_ref, ms_b)
    for hh in range(ATT_HEADS):
        m = jnp.maximum(ms_a[hh], ms_b[hh])
        acc = acca_ref[hh] * jnp.exp2(ms_a[hh] - m) + accb_ref[hh] * jnp.exp2(ms_b[hh] - m)
        o_ref[hh * C_V:(hh + 1) * C_V, :] = (acc[0:C_V] / acc[C_V:C_V + 1]).astype(o_ref.dtype)


def _attention(q, k, vt, bsz, seq):
    assert ATT_TQ == 2 * ATT_TK
    nh = ATT_HEADS
    nq = seq // ATT_TQ
    nk = seq // ATT_TK
    return pl.pallas_call(
        _attn_kernel,
        out_shape=jax.ShapeDtypeStruct((bsz * nq, C_WIDTH, ATT_TQ), BF16),
        grid=(bsz, C_HEADS // nh, nq),
        in_specs=[pl.BlockSpec((None, ATT_TQ, nh * HEAD_PAD), lambda b, g, i: (b, i, g)),
                  pl.BlockSpec((None, seq, nh * HEAD_PAD), lambda b, g, i: (b, 0, g)),
                  pl.BlockSpec((nk, nh * C_V, ATT_TK), lambda b, g, i: (b, g, 0))],
        out_specs=pl.BlockSpec((None, nh * C_V, ATT_TQ), lambda b, g, i: (b * nq + i, g, 0)),
        scratch_shapes=[pltpu.VMEM((nh, ATT_TK, ATT_TQ), F32), pltpu.VMEM((nh, ATT_TK, ATT_TQ), F32),
                        pltpu.VMEM((nh, C_V + ATT_ONES, ATT_TQ), F32), pltpu.VMEM((nh, C_V + ATT_ONES, ATT_TQ), F32)],
        compiler_params=_params(("parallel", "parallel", "arbitrary")),
        name="mla_attention",
    )(q.reshape(bsz, seq, -1), k.reshape(bsz, seq, -1), vt)


def _channel_kernel(oab_ref, oct_ref, h_ref, p_ref, wab_ref, wc_ref, g1_ref, b1_ref,
                    wg_ref, wu_ref, wd_ref, wpg_ref, wpp_ref, g2_ref, b2_ref, o_ref, act_s, *,
                    token_major_out):
    mix = _dot(oab_ref[...], wab_ref[...]) + _dot_tn(oct_ref[...], wc_ref[...])
    h = _layer_norm(DEEPNORM_ALPHA * h_ref[...] + mix, g1_ref[...], b1_ref[...])
    hb = h.astype(BF16)
    for c in range(D_FF // FF_CHUNK):
        cols = slice(c * FF_CHUNK, (c + 1) * FF_CHUNK)
        gate = _dot(hb, wg_ref[:, cols])
        up = _dot(hb, wu_ref[:, cols])
        act_s[:, cols] = (gate * _sigmoid(gate) * up).astype(BF16)
    ffn = _dot(act_s[...], wd_ref[...])
    ple = _sigmoid(_dot(hb, wpg_ref[...])) * _dot(_read_pos_major(p_ref).astype(BF16), wpp_ref[...])
    out = _layer_norm(DEEPNORM_ALPHA * h + ffn + ple, g2_ref[...], b2_ref[...])
    if token_major_out:
        _write_token_major(o_ref, out)
    else:
        o_ref[...] = out


def _channel(o_ab, o_ct, h, p, sw, layer, token_major_out):
    t = h.shape[0]
    tm = ROW_BLOCK
    row = lambda w: pl.BlockSpec((tm, w), lambda i: (i, 0))
    if token_major_out:
        out_shape = jax.ShapeDtypeStruct((t // MIX_BLOCK, MIX_NCHUNK, A_CHUNK, D_MODEL), F32)
        out_spec = _token_block_spec(D_MODEL)
    else:
        out_shape = jax.ShapeDtypeStruct((t, D_MODEL), F32)
        out_spec = row(D_MODEL)
    names = ("w_out_ab", "w_out_c", "ln1_g", "ln1_b", "wg", "wu", "wd", "wpg", "wpp", "ln2_g", "ln2_b")
    return pl.pallas_call(
        functools.partial(_channel_kernel, token_major_out=token_major_out),
        out_shape=out_shape,
        grid=(t // tm,),
        in_specs=[row(A_WIDTH + B_WIDTH), pl.BlockSpec((None, C_WIDTH, tm), lambda i: (i, 0, 0)), row(D_MODEL),
                  _token_block_spec(PLE_DIM, layer * (t // tm))] + [_layer_spec(sw[n], layer, single_buffer=True) for n in names],
        out_specs=out_spec,
        scratch_shapes=[pltpu.VMEM((tm, D_FF), BF16)],
        compiler_params=_params(("parallel",)),
        name="channel_mix",
    )(o_ab, o_ct, h, _token_blocks(p), *[sw[n] for n in names])


def _head_pad_cols(w_nope, w_rope):
    lead = w_nope.shape[:2]
    pad = jnp.zeros(lead + (C_HEADS, HEAD_PAD - C_NOPE - C_ROPE), w_nope.dtype)
    return jnp.concatenate([w_nope, w_rope, pad], axis=-1).reshape(lead + (C_HEADS * HEAD_PAD,))


def _pos_major(x, axis):
    cps = B_CHUNK // A_CHUNK
    shape = x.shape
    x = x.reshape(shape[:axis] + (cps, A_CHUNK) + shape[axis + 1:])
    return jnp.swapaxes(x, axis, axis + 1).reshape(shape)


def _prep_weights(lower_bounds, w_in, hgrn_norm_g, sgu_ln_g, sgu_ln_b, sgu_w_s, sgu_b_s,
                  mla_q_norm_g, mla_w_uq, mla_kv_norm_g, mla_w_ukv, w_out, ln1_g, ln1_b,
                  w_gate_up, w_down, ple_w_gate, ple_w_proj, ln2_g, ln2_b):
    nl = w_in.shape[0]
    o_hg, o_sg = 4 * A_WIDTH, 4 * A_WIDTH + 2 * B_WIDTH
    o_kv = o_sg + C_Q_RANK
    o_kr = o_kv + C_KV_RANK
    w_kr = w_in[:, :, o_kr:o_kr + C_ROPE]
    lane_pad = lambda wr: jnp.concatenate(
        [jnp.zeros((nl, D_MODEL, C_NOPE), F32), wr, jnp.zeros((nl, D_MODEL, HEAD_PAD - C_NOPE - C_ROPE), F32)],
        axis=-1)
    wq = mla_w_uq.reshape(nl, C_Q_RANK, C_HEADS, C_NOPE + C_ROPE)
    wq_nope, wq_rope = wq[..., :C_NOPE], wq[..., C_NOPE:]
    wkv = mla_w_ukv.reshape(nl, C_KV_RANK, C_HEADS, C_NOPE + C_V)
    wk_nope, wv = wkv[..., :C_NOPE], wkv[..., C_NOPE:]
    zeros_rope = jnp.zeros((nl, C_KV_RANK, C_HEADS, C_ROPE), F32)
    tri = jnp.tril(jnp.ones((B_CHUNK, B_CHUNK), F32))
    head_id = np.arange(A_WIDTH) // A_DK
    vec = lambda g: g.reshape(nl, 1, -1)
    return dict(
        w_hg=w_in[:, :, :o_hg].astype(BF16), w_sg=w_in[:, :, o_hg:o_sg].astype(BF16),
        w_lat=jnp.concatenate([w_in[:, :, o_sg:o_kr], lane_pad(w_kr)], axis=-1).astype(BF16),
        gq=vec(mla_q_norm_g), gkv=vec(mla_kv_norm_g),
        wq=_head_pad_cols(wq_nope, wq_rope).astype(BF16),
        wk=_head_pad_cols(wk_nope, zeros_rope).astype(BF16),
        wv=jnp.swapaxes(wv.reshape(nl, C_KV_RANK, C_WIDTH), 1, 2).astype(BF16),
        lb=vec(lower_bounds), hgrn_g=vec(hgrn_norm_g), sgu_g=vec(sgu_ln_g), sgu_b=vec(sgu_ln_b),
        ws=_pos_major(_pos_major(sgu_w_s * tri, 2), 3).astype(BF16),
        bs=_pos_major(jnp.repeat(jnp.swapaxes(sgu_b_s, 1, 2), B_CH, axis=2), 1),
        ones_bd=jnp.asarray(head_id[:, None] == head_id[None, :], BF16),
        w_out_ab=w_out[:, :A_WIDTH + B_WIDTH].astype(BF16), w_out_c=w_out[:, A_WIDTH + B_WIDTH:].astype(BF16),
        ln1_g=vec(ln1_g), ln1_b=vec(ln1_b),
        wg=w_gate_up[:, :, :D_FF].astype(BF16), wu=w_gate_up[:, :, D_FF:].astype(BF16),
        wd=w_down.astype(BF16), wpg=ple_w_gate.astype(BF16), wpp=ple_w_proj.astype(BF16),
        ln2_g=vec(ln2_g), ln2_b=vec(ln2_b),
    )


def kernel(x, p, positions, ln_in_g, ln_in_b, w_in, hgrn_lb_logits, hgrn_norm_g, sgu_ln_g, sgu_ln_b, sgu_w_s, sgu_b_s, mla_q_norm_g, mla_w_uq, mla_kv_norm_g, mla_w_ukv, w_out, ln1_g, ln1_b, w_gate_up, w_down, ple_w_gate, ple_w_proj, ln2_g, ln2_b):
    bsz, seq, d = x.shape
    t = bsz * seq
    lb_cum = jnp.cumsum(jax.nn.softmax(hgrn_lb_logits.astype(F32), axis=0), axis=0)
    lower_bounds = lb_cum - lb_cum[0]
    pos_pm = positions.reshape(bsz, seq // MIX_BLOCK, MIX_NCHUNK, A_CHUNK).swapaxes(2, 3).reshape(bsz, seq)
    rope_pats = _rope_tables(pos_pm)
    sw = _prep_weights(lower_bounds, w_in, hgrn_norm_g, sgu_ln_g, sgu_ln_b, sgu_w_s, sgu_b_s,
                       mla_q_norm_g, mla_w_uq, mla_kv_norm_g, mla_w_ukv, w_out, ln1_g, ln1_b,
                       w_gate_up, w_down, ple_w_gate, ple_w_proj, ln2_g, ln2_b)

    h = _ln_in(x.reshape(t, d), ln_in_g, ln_in_b)
    for i in range(DEPTH):
        hg, sg, q, k, v = _in_proj(h, rope_pats, sw, i)
        o_ab = _mixers(hg, sg, sw, i, bsz, seq)
        o_c = _attention(q, k, v, bsz, seq)
        h = _channel(o_ab, o_c, h, p.reshape(DEPTH * t, PLE_DIM), sw, i, token_major_out=(i == DEPTH - 1))
    return h.reshape(bsz, seq, d)
```

```python
import functools
import math

import numpy as np
import jax
import jax.numpy as jnp
from jax import lax
from jax.experimental import pallas as pl
from jax.experimental.pallas import tpu as pltpu

F32 = jnp.float32
BF16 = jnp.bfloat16

D_MODEL = 1024
DEPTH = 2
PLE_DIM = 256

A_WIDTH = 256
A_DK = 64
A_HEADS = 4
A_CHUNK = 16
B_WIDTH = 256
B_CH = 64
B_GROUPS = 4
B_CHUNK = 128
C_WIDTH = 512
C_NOPE = 64
C_ROPE = 32
C_V = 64
C_HEADS = 8
C_Q_RANK = 384
C_KV_RANK = 256
ROPE_THETA = 10000.0
HEAD_PAD = 128

D_FF = 2816
LN_EPS = 1e-5
RMS_EPS = 1e-6
DEEPNORM_ALPHA = (2 * DEPTH) ** 0.25
ATT_SCALE = (C_NOPE + C_ROPE) ** -0.5
Q_SCALE = ATT_SCALE * math.log2(math.e)

MIX_BLOCK = 256
MIX_NCHUNK = MIX_BLOCK // A_CHUNK
ATT_TQ = 512
ATT_TK = 256
ATT_HEADS = 4
ATT_ONES = 16
ROW_BLOCK = 512
FF_CHUNK = 256
VMEM_LIMIT = 56 * 1024 * 1024
NEG_BIG = -1e30


def _dot(a, b):
    return jnp.dot(a, b, preferred_element_type=F32)


def _dot_nt(a, b):
    return lax.dot_general(a, b, (((1,), (1,)), ((), ())), preferred_element_type=F32)


def _dot_tn(a, b):
    return lax.dot_general(a, b, (((0,), (0,)), ((), ())), preferred_element_type=F32)


def _layer_norm(x, g, b):
    mu = jnp.mean(x, axis=-1, keepdims=True)
    xc = x - mu
    var = jnp.mean(xc * xc, axis=-1, keepdims=True)
    return xc * lax.rsqrt(var + LN_EPS) * g + b


def _rms_norm(x, g):
    return x * lax.rsqrt(jnp.mean(x * x, axis=-1, keepdims=True) + RMS_EPS) * g


def _sigmoid(x):
    return 1.0 / (1.0 + jnp.exp(-x))


def _const_spec(shape):
    nd = len(shape)
    return pl.BlockSpec(shape, lambda *_: (0,) * nd)


def _layer_spec(arr, layer, single_buffer=False):
    shape = arr.shape[1:]
    nd = len(shape)
    kw = dict(pipeline_mode=pl.Buffered(1)) if single_buffer else {}
    return pl.BlockSpec((None,) + shape, lambda *_: (layer,) + (0,) * nd, **kw)


def _params(sem):
    return pltpu.CompilerParams(dimension_semantics=sem, vmem_limit_bytes=VMEM_LIMIT)


def _rope_table_kernel(pos_ref, freq_ref, cos_ref, sin_ref):
    ang = pos_ref[...] * freq_ref[...]
    cos_ref[...] = jnp.cos(ang)
    sin_ref[...] = jnp.sin(ang)


def _rope_tables(positions):
    t = positions.size
    half = C_ROPE // 2
    per_row = 128 // half
    inv_freq = ROPE_THETA ** (-jnp.arange(0, C_ROPE, 2, dtype=F32) / C_ROPE)
    pos_rep = jnp.repeat(positions.astype(F32).reshape(t), half).reshape(t // per_row, 128)
    freq = jnp.tile(inv_freq, per_row).reshape(1, 128)
    rows = t // per_row
    cos, sin = pl.pallas_call(
        _rope_table_kernel,
        out_shape=(jax.ShapeDtypeStruct((rows, 128), F32),) * 2,
        grid=(1,),
        in_specs=[_const_spec((rows, 128)), _const_spec((1, 128))],
        out_specs=(_const_spec((rows, 128)),) * 2,
        name="rope_tables",
    )(pos_rep, freq)
    cos = cos.reshape(t, half)
    sin = sin.reshape(t, half)
    ones = jnp.ones((t, C_NOPE), F32)
    zeros = jnp.zeros((t, C_NOPE), F32)
    zh = jnp.zeros((t, half), F32)
    pad1 = jnp.ones((t, HEAD_PAD - C_NOPE - C_ROPE), F32)
    pad0 = jnp.zeros((t, HEAD_PAD - C_NOPE - C_ROPE), F32)
    cos_pat = jnp.concatenate([ones, cos, cos, pad1], axis=1)
    sin_x1 = jnp.concatenate([zeros, -sin, zh, pad0], axis=1)
    sin_x2 = jnp.concatenate([zeros, zh, sin, pad0], axis=1)
    return cos_pat, sin_x1, sin_x2


def _read_pos_major(ref):
    return jnp.concatenate([ref[blk, :, p, :] for blk in range(ref.shape[0]) for p in range(A_CHUNK)], axis=0)


def _write_token_major(ref, val):
    for blk in range(ref.shape[0]):
        for p in range(A_CHUNK):
            r0 = blk * MIX_BLOCK + p * MIX_NCHUNK
            ref[blk, :, p, :] = val[r0:r0 + MIX_NCHUNK]


def _token_blocks(x2d):
    t, wd = x2d.shape
    return x2d.reshape(t // MIX_BLOCK, MIX_NCHUNK, A_CHUNK, wd)


def _token_block_spec(wd, first_block=0):
    nblk = ROW_BLOCK // MIX_BLOCK
    return pl.BlockSpec((nblk, MIX_NCHUNK, A_CHUNK, wd), lambda i: (first_block + i, 0, 0, 0))


def _rope_lanes(x, cos, sin_x1, sin_x2):
    half = C_ROPE // 2
    right = pltpu.roll(x, HEAD_PAD - half, axis=1)
    left = pltpu.roll(x, half, axis=1)
    return x * cos + right * sin_x1 + left * sin_x2


def _inproj_ln_kernel(x_ref, lng_ref, lnb_ref, *rest):
    *rest, h_out = rest
    h = _layer_norm(_read_pos_major(x_ref), lng_ref[...], lnb_ref[...])
    h_out[...] = h
    _inproj_body(h, *rest)


def _inproj_kernel(h_ref, *rest):
    _inproj_body(h_ref[...], *rest)


def _inproj_body(h, cos_ref, sin1_ref, sin2_ref, w_hg, w_sg, w_lat,
                 gq_ref, wq_ref, gkv_ref, wk_ref, wv_ref,
                 hg_out, sg_out, q_out, k_out, v_out):
    hb = h.astype(BF16)
    hg_out[...] = _dot(hb, w_hg[...])
    sg_out[...] = _dot(hb, w_sg[...])
    lat = _dot(hb, w_lat[...])
    cos, sin1, sin2 = cos_ref[...], sin1_ref[...], sin2_ref[...]

    cqn = _rms_norm(lat[:, 0:C_Q_RANK], gq_ref[...]).astype(BF16)
    q_lin = _dot(cqn, wq_ref[...])
    for h in range(C_HEADS):
        sl = slice(h * HEAD_PAD, (h + 1) * HEAD_PAD)
        q_out[:, sl] = (_rope_lanes(q_lin[:, sl], cos, sin1, sin2) * Q_SCALE).astype(BF16)

    ckvn = _rms_norm(lat[:, C_Q_RANK:C_Q_RANK + C_KV_RANK], gkv_ref[...]).astype(BF16)
    k_lin = _dot(ckvn, wk_ref[...])
    vt = _dot_nt(wv_ref[...], ckvn).astype(BF16)
    for c in range(v_out.shape[0]):
        v_out[c] = vt[:, c * ATT_TK:(c + 1) * ATT_TK]
    k_rope = _rope_lanes(lat[:, C_Q_RANK + C_KV_RANK:], cos, sin1, sin2)
    for h in range(C_HEADS):
        sl = slice(h * HEAD_PAD, (h + 1) * HEAD_PAD)
        k_out[:, sl] = (k_lin[:, sl] + k_rope).astype(BF16)


def _in_proj(h, rope_pats, sw, layer, ln=None):
    t = h.shape[0]
    tm = ROW_BLOCK
    row = lambda w: pl.BlockSpec((tm, w), lambda i: (i, 0))
    weights = [sw[n] for n in ("w_hg", "w_sg", "w_lat", "gq", "wq", "gkv", "wk", "wv")]
    hp = C_HEADS * HEAD_PAD
    out_shape = [jax.ShapeDtypeStruct((t, 4 * A_WIDTH), F32),
                 jax.ShapeDtypeStruct((t, 2 * B_WIDTH), F32),
                 jax.ShapeDtypeStruct((t, hp), BF16),
                 jax.ShapeDtypeStruct((t, hp), BF16),
                 jax.ShapeDtypeStruct((t // ATT_TK, C_WIDTH, ATT_TK), BF16)]
    out_specs = [row(4 * A_WIDTH), row(2 * B_WIDTH), row(hp), row(hp),
                 pl.BlockSpec((tm // ATT_TK, C_WIDTH, ATT_TK), lambda i: (i, 0, 0))]
    common_specs = [row(HEAD_PAD)] * 3 + [_layer_spec(w, layer) for w in weights]
    if ln is None:
        body, first_specs, first_args = _inproj_kernel, [row(D_MODEL)], [h]
    else:
        body = _inproj_ln_kernel
        first_specs = [_token_block_spec(D_MODEL), _const_spec((1, D_MODEL)), _const_spec((1, D_MODEL))]
        first_args = [_token_blocks(h), ln[0].reshape(1, D_MODEL), ln[1].reshape(1, D_MODEL)]
        out_shape.append(jax.ShapeDtypeStruct((t, D_MODEL), F32))
        out_specs.append(row(D_MODEL))
    return pl.pallas_call(
        body,
        out_shape=tuple(out_shape),
        grid=(t // tm,),
        in_specs=first_specs + common_specs,
        out_specs=tuple(out_specs),
        compiler_params=_params(("parallel",)),
        name="in_proj",
    )(*first_args, *rope_pats, *weights)


def _erf(x):
    return lax.erf(x)


def _gelu(x):
    return 0.5 * x * (1.0 + _erf(x * (2.0 ** -0.5)))


def _rows_to_array(rows, like):
    zero = jnp.zeros_like(like)
    return jnp.concatenate([zero if r is None else r for r in rows], axis=0)


def _mixer_kernel(hg_ref, sg_ref, lb_ref, ng_ref, lng_ref, lnb_ref, ws_ref, bs_ref, ones_ref,
                  o_ref,
                  q_s, k_s, f_s, v_s, qt_s, kt_s, oi_s, a_s, b_s, qs_s, ks_s, state_s):
    nc, cs, w = MIX_NCHUNK, A_CHUNK, A_WIDTH
    tb = nc * cs
    slab = lambda p: slice(p * nc, (p + 1) * nc)

    @pl.when(pl.program_id(1) == 0)
    def _():
        state_s[...] = jnp.zeros_like(state_s)

    lb = lb_ref[...]
    for p in range(cs):
        aq = hg_ref[slab(p), 0:w]
        f_p = lb + (1.0 - lb) * _sigmoid(hg_ref[slab(p), w:2 * w])
        q_s[p] = aq * _sigmoid(aq)
        f_s[p] = f_p
        k_s[p] = 1.0 - f_p
        v_s[p] = hg_ref[slab(p), 2 * w:3 * w]

    ones_bd = ones_ref[...]

    kd = []
    for p in range(cs):
        f_p = f_s[p]
        q_p = q_s[p]
        kd = [k_s[p]] + [f_p * x for x in kd]
        xs = jnp.concatenate([q_p * x for x in kd], axis=0).astype(BF16)
        wgt = _dot(xs, ones_bd)
        o_p = wgt[0:nc] * v_s[p]
        for d in range(1, p + 1):
            o_p = o_p + wgt[d * nc:(d + 1) * nc] * v_s[p - d]
        oi_s[p] = o_p

    pref = f_s[0]
    qt_s[0] = q_s[0] * pref
    for p in range(1, cs):
        pref = pref * f_s[p]
        qt_s[p] = q_s[p] * pref
    chunk_decay = pref
    suf = jnp.ones_like(pref)
    kt_s[cs - 1] = k_s[cs - 1]
    for p in range(cs - 2, -1, -1):
        suf = suf * f_s[p + 1]
        kt_s[p] = k_s[p] * suf

    drow = [chunk_decay[c:c + 1, :] for c in range(nc)]
    one_row = jnp.ones_like(drow[0])
    levels = []
    m = 1
    while m < nc:
        levels.append(m)
        m *= 2
    la_arrays, lb_arrays = [], []
    for m in levels:
        la = [None] * nc
        lbr = [None] * nc
        for base in range(0, nc, 2 * m):
            mid = base + m
            la[mid] = one_row
            for i in range(mid + 1, mid + m):
                la[i] = la[i - 1] * drow[i - 1]
            lbr[mid - 1] = one_row
            for j in range(mid - 2, base - 1, -1):
                lbr[j] = lbr[j + 1] * drow[j + 1]
        la_arrays.append(_rows_to_array(la, one_row))
        lb_arrays.append(_rows_to_array(lbr, one_row))
    ep = [one_row]
    for c in range(1, nc):
        ep.append(ep[-1] * drow[c - 1])
    es = [one_row] * nc
    for c in range(nc - 2, -1, -1):
        es[c] = es[c + 1] * drow[c + 1]
    total_decay = ep[-1] * drow[nc - 1]
    ep_arr = jnp.concatenate(ep, axis=0)
    es_arr = jnp.concatenate(es, axis=0)

    for p in range(cs):
        qt_p = qt_s[p]
        kt_p = kt_s[p]
        for li in range(len(levels)):
            a_s[li, p] = qt_p * la_arrays[li]
            b_s[li, p] = kt_p * lb_arrays[li]
        qs_s[p] = qt_p * ep_arr
        ks_s[p] = kt_p * es_arr

    lane = lax.broadcasted_iota(jnp.int32, (tb, w), 1)
    head_masks = [(lane >= h * A_DK) & (lane < (h + 1) * A_DK) for h in range(A_HEADS)]
    row_c = lax.broadcasted_iota(jnp.int32, (tb, tb), 0) % nc
    col_c = lax.broadcasted_iota(jnp.int32, (tb, tb), 1) % nc
    v_bf = v_s[...].reshape(tb, w).astype(BF16)
    a_tok = [a_s[li].reshape(tb, w) for li in range(len(levels))]
    b_tok = [b_s[li].reshape(tb, w).astype(BF16) for li in range(len(levels))]
    o_cross = jnp.zeros((tb, w), F32)
    for h in range(A_HEADS):
        sc = None
        for li, m in enumerate(levels):
            a_h = jnp.where(head_masks[h], a_tok[li], 0.0).astype(BF16)
            s_l = _dot_nt(a_h, b_tok[li])
            if 2 * m < nc:
                s_l = jnp.where((row_c // (2 * m)) == (col_c // (2 * m)), s_l, 0.0)
            sc = s_l if sc is None else sc + s_l
        o_h = _dot(sc.astype(BF16), v_bf)
        o_cross = o_cross + jnp.where(head_masks[h], o_h, 0.0)

    st = state_s[...]
    o_state = _dot_nt(qs_s[...].reshape(tb, w).astype(BF16), st.astype(BF16))
    kv = _dot_tn(v_bf, ks_s[...].reshape(tb, w).astype(BF16))
    rr = lax.broadcasted_iota(jnp.int32, (w, w), 0)
    cc = lax.broadcasted_iota(jnp.int32, (w, w), 1)
    state_s[...] = st * total_decay + jnp.where((rr // A_DK) == (cc // A_DK), kv, 0.0)

    o = oi_s[...].reshape(tb, w) + o_cross + o_state
    ms = _dot((o * o).astype(BF16), ones_bd) * (1.0 / A_DK)
    ag = hg_ref[:, 3 * w:4 * w]
    o_ref[:, 0:w] = (o * lax.rsqrt(ms + RMS_EPS) * ng_ref[...] * (ag * _sigmoid(ag))).astype(o_ref.dtype)

    cps = B_CHUNK // cs
    u = _gelu(sg_ref[:, 0:B_WIDTH])
    vv = _layer_norm(_gelu(sg_ref[:, B_WIDTH:2 * B_WIDTH]), lng_ref[...], lnb_ref[...])
    lane_b = lax.broadcasted_iota(jnp.int32, (B_CHUNK, B_WIDTH), 1)
    zs = []
    for ci in range(nc // cps):
        v_c = jnp.concatenate([vv[p * nc + ci * cps:p * nc + (ci + 1) * cps] for p in range(cs)],
                              axis=0).astype(BF16)
        z = bs_ref[...]
        for g in range(B_GROUPS):
            z_g = _dot(ws_ref[g], v_c)
            z = z + jnp.where((lane_b >= g * B_CH) & (lane_b < (g + 1) * B_CH), z_g, 0.0)
        zs.append(z)
    z_all = jnp.concatenate([z[p * cps:(p + 1) * cps] for p in range(cs) for z in zs], axis=0)
    o_ref[:, w:w + B_WIDTH] = (u * z_all).astype(o_ref.dtype)


def _mixers(hg, sg, sw, layer, bsz, seq):
    nb = seq // MIX_BLOCK
    nc, cs, w = MIX_NCHUNK, A_CHUNK, A_WIDTH
    nlev = int(round(math.log2(nc)))
    blk = lambda width: pl.BlockSpec((MIX_BLOCK, width), lambda b, i: (b * nb + i, 0))
    pm = lambda: pltpu.VMEM((cs, nc, w), F32)
    ow = A_WIDTH + B_WIDTH
    return pl.pallas_call(
        _mixer_kernel,
        out_shape=jax.ShapeDtypeStruct((bsz * seq, ow), BF16),
        grid=(bsz, nb),
        in_specs=[blk(4 * A_WIDTH), blk(2 * B_WIDTH),
                  ] + [_layer_spec(sw[n], layer) for n in ("lb", "hgrn_g", "sgu_g", "sgu_b", "ws", "bs")] + [
                  _const_spec((w, w))],
        out_specs=blk(ow),
        scratch_shapes=[pm(), pm(), pm(), pm(), pm(), pm(), pm(),
                        pltpu.VMEM((nlev, cs, nc, w), F32), pltpu.VMEM((nlev, cs, nc, w), F32),
                        pm(), pm(), pltpu.VMEM((w, w), F32)],
        compiler_params=_params(("parallel", "arbitrary")),
        name="mixers",
    )(hg, sg, sw["lb"], sw["hgrn_g"], sw["sgu_g"], sw["sgu_b"], sw["ws"], sw["bs"], sw["ones_bd"])


def _token_offset(i):
    r = i % MIX_BLOCK
    return (i - r) + (r % MIX_NCHUNK) * A_CHUNK + r // MIX_NCHUNK


def _attn_kernel(q_ref, k_ref, vt_ref, o_ref, sa_ref, sb_ref, acca_ref, accb_ref):
    tq, tk = ATT_TQ, ATT_TK
    qi = pl.program_id(2)
    qs = [q_ref[:, hh * HEAD_PAD:(hh + 1) * HEAD_PAD] for hh in range(ATT_HEADS)]

    def produce(j, dst, mask=None, cols=slice(None)):
        off = pl.multiple_of(j * tk, tk)
        tile_max = []
        for hh in range(ATT_HEADS):
            k_t = k_ref[pl.ds(off, tk), hh * HEAD_PAD:(hh + 1) * HEAD_PAD]
            st = _dot_nt(k_t, qs[hh][cols])
            if mask is not None:
                st = jnp.where(mask, st, NEG_BIG)
            dst[hh, :, cols] = st
            tile_max.append(jnp.max(st, axis=0, keepdims=True))
        return tuple(tile_max)

    ones_rows = jnp.ones((ATT_ONES, tk), BF16)

    def consume(j, src, tile_max, acc, ms, cols=slice(None)):
        new = []
        for hh in range(ATT_HEADS):
            m_new = jnp.maximum(ms[hh], tile_max[hh])
            alpha = jnp.exp2(ms[hh] - m_new)
            pt = jnp.exp2(src[hh, :, cols] - m_new).astype(BF16)
            vt_h = jnp.concatenate([vt_ref[j, hh * C_V:(hh + 1) * C_V, :], ones_rows], axis=0)
            acc[hh, :, cols] = alpha * acc[hh, :, cols] + _dot(vt_h, pt)
            new.append(m_new)
        return tuple(new)

    key = _token_offset(lax.broadcasted_iota(jnp.int32, (tk, tq), 0))
    qry = _token_offset(lax.broadcasted_iota(jnp.int32, (tk, tq), 1))
    d0 = 2 * qi
    acca_ref[...] = jnp.zeros_like(acca_ref)
    accb_ref[...] = jnp.zeros_like(accb_ref)
    max_a = produce(d0, sa_ref, key <= qry)
    ms0 = tuple(jnp.full((1, tq), NEG_BIG, F32) for _ in range(ATT_HEADS))

    def body(p, carry):
        ms_a, ms_b, in_a, max_a = carry
        max_b = produce(2 * p, sb_ref)
        ms_a = consume(in_a, sa_ref, max_a, acca_ref, ms_a)
        max_a = produce(2 * p + 1, sa_ref)
        ms_b = consume(2 * p, sb_ref, max_b, accb_ref, ms_b)
        return ms_a, ms_b, 2 * p + 1, max_a

    ms_a, ms_b, in_a, max_a = lax.fori_loop(0, qi, body, (ms0, ms0, d0, max_a))
    late = slice(tq - tk, tq)
    max_b = produce(d0 + 1, sb_ref, (key <= qry)[:, 0:tk], cols=late)
    ms_a = consume(in_a, sa_ref, max_a, acca_ref, ms_a)
    ms_late = consume(d0 + 1, sb_ref, max_b, accb_ref, tuple(m[:, late] for m in ms_b), cols=late)
    ms_b = tuple(jnp.concatenate([m[:, 0:tq - tk], ml], axis=1) for m, ml in zip(ms_b, ms_late))
    for hh in range(ATT_HEADS):
        m = jnp.maximum(ms_a[hh], ms_b[hh])
        acc = acca_ref[hh] * jnp.exp2(ms_a[hh] - m) + accb_ref[hh] * jnp.exp2(ms_b[hh] - m)
        o_ref[hh * C_V:(hh + 1) * C_V, :] = (acc[0:C_V] / acc[C_V:C_V + 1]).astype(o_ref.dtype)


def _attention(q, k, vt, bsz, seq):
    assert ATT_TQ == 2 * ATT_TK
    nh = ATT_HEADS
    nq = seq // ATT_TQ
    nk = seq // ATT_TK
    return pl.pallas_call(
        _attn_kernel,
        out_shape=jax.ShapeDtypeStruct((bsz * nq, C_WIDTH, ATT_TQ), BF16),
        grid=(bsz, C_HEADS // nh, nq),
        in_specs=[pl.BlockSpec((None, ATT_TQ, nh * HEAD_PAD), lambda b, g, i: (b, i, g)),
                  pl.BlockSpec((None, seq, nh * HEAD_PAD), lambda b, g, i: (b, 0, g)),
                  pl.BlockSpec((nk, nh * C_V, ATT_TK), lambda b, g, i: (b, g, 0))],
        out_specs=pl.BlockSpec((None, nh * C_V, ATT_TQ), lambda b, g, i: (b * nq + i, g, 0)),
        scratch_shapes=[pltpu.VMEM((nh, ATT_TK, ATT_TQ), F32), pltpu.VMEM((nh, ATT_TK, ATT_TQ), F32),
                        pltpu.VMEM((nh, C_V + ATT_ONES, ATT_TQ), F32), pltpu.VMEM((nh, C_V + ATT_ONES, ATT_TQ), F32)],
        compiler_params=_params(("parallel", "parallel", "arbitrary")),
        name="mla_attention",
    )(q.reshape(bsz, seq, -1), k.reshape(bsz, seq, -1), vt)


def _channel_kernel(oab_ref, oct_ref, h_ref, p_ref, wab_ref, wc_ref, g1_ref, b1_ref,
                    wg_ref, wu_ref, wd_ref, wpg_ref, wpp_ref, g2_ref, b2_ref, o_ref, act_s, *,
                    token_major_out):
    mix = _dot(oab_ref[...], wab_ref[...]) + _dot_tn(oct_ref[...], wc_ref[...])
    h = _layer_norm(DEEPNORM_ALPHA * h_ref[...] + mix, g1_ref[...], b1_ref[...])
    hb = h.astype(BF16)
    for c in range(D_FF // FF_CHUNK):
        cols = slice(c * FF_CHUNK, (c + 1) * FF_CHUNK)
        gate = _dot(hb, wg_ref[:, cols])
        up = _dot(hb, wu_ref[:, cols])
        act_s[:, cols] = (gate * _sigmoid(gate) * up).astype(BF16)
    ffn = _dot(act_s[...], wd_ref[...])
    ple = _sigmoid(_dot(hb, wpg_ref[...])) * _dot(_read_pos_major(p_ref).astype(BF16), wpp_ref[...])
    out = _layer_norm(DEEPNORM_ALPHA * h + ffn + ple, g2_ref[...], b2_ref[...])
    if token_major_out:
        _write_token_major(o_ref, out)
    else:
        o_ref[...] = out


def _channel(o_ab, o_ct, h, p, sw, layer, token_major_out):
    t = h.shape[0]
    tm = ROW_BLOCK
    row = lambda w: pl.BlockSpec((tm, w), lambda i: (i, 0))
    if token_major_out:
        out_shape = jax.ShapeDtypeStruct((t // MIX_BLOCK, MIX_NCHUNK, A_CHUNK, D_MODEL), F32)
        out_spec = _token_block_spec(D_MODEL)
    else:
        out_shape = jax.ShapeDtypeStruct((t, D_MODEL), F32)
        out_spec = row(D_MODEL)
    names = ("w_out_ab", "w_out_c", "ln1_g", "ln1_b", "wg", "wu", "wd", "wpg", "wpp", "ln2_g", "ln2_b")
    return pl.pallas_call(
        functools.partial(_channel_kernel, token_major_out=token_major_out),
        out_shape=out_shape,
        grid=(t // tm,),
        in_specs=[row(A_WIDTH + B_WIDTH), pl.BlockSpec((None, C_WIDTH, tm), lambda i: (i, 0, 0)), row(D_MODEL),
                  _token_block_spec(PLE_DIM, layer * (t // tm))] + [_layer_spec(sw[n], layer, single_buffer=True) for n in names],
        out_specs=out_spec,
        scratch_shapes=[pltpu.VMEM((tm, D_FF), BF16)],
        compiler_params=_params(("parallel",)),
        name="channel_mix",
    )(o_ab, o_ct, h, _token_blocks(p), *[sw[n] for n in names])


def _head_pad_cols(w_nope, w_rope):
    lead = w_nope.shape[:2]
    pad = jnp.zeros(lead + (C_HEADS, HEAD_PAD - C_NOPE - C_ROPE), w_nope.dtype)
    return jnp.concatenate([w_nope, w_rope, pad], axis=-1).reshape(lead + (C_HEADS * HEAD_PAD,))


def _pos_major(x, axis):
    cps = B_CHUNK // A_CHUNK
    shape = x.shape
    x = x.reshape(shape[:axis] + (cps, A_CHUNK) + shape[axis + 1:])
    return jnp.swapaxes(x, axis, axis + 1).reshape(shape)


def _prep_weights(lower_bounds, w_in, hgrn_norm_g, sgu_ln_g, sgu_ln_b, sgu_w_s, sgu_b_s,
                  mla_q_norm_g, mla_w_uq, mla_kv_norm_g, mla_w_ukv, w_out, ln1_g, ln1_b,
                  w_gate_up, w_down, ple_w_gate, ple_w_proj, ln2_g, ln2_b):
    nl = w_in.shape[0]
    o_hg, o_sg = 4 * A_WIDTH, 4 * A_WIDTH + 2 * B_WIDTH
    o_kv = o_sg + C_Q_RANK
    o_kr = o_kv + C_KV_RANK
    w_kr = w_in[:, :, o_kr:o_kr + C_ROPE]
    lane_pad = lambda wr: jnp.concatenate(
        [jnp.zeros((nl, D_MODEL, C_NOPE), F32), wr, jnp.zeros((nl, D_MODEL, HEAD_PAD - C_NOPE - C_ROPE), F32)],
        axis=-1)
    wq = mla_w_uq.reshape(nl, C_Q_RANK, C_HEADS, C_NOPE + C_ROPE)
    wq_nope, wq_rope = wq[..., :C_NOPE], wq[..., C_NOPE:]
    wkv = mla_w_ukv.reshape(nl, C_KV_RANK, C_HEADS, C_NOPE + C_V)
    wk_nope, wv = wkv[..., :C_NOPE], wkv[..., C_NOPE:]
    zeros_rope = jnp.zeros((nl, C_KV_RANK, C_HEADS, C_ROPE), F32)
    tri = jnp.tril(jnp.ones((B_CHUNK, B_CHUNK), F32))
    head_id = np.arange(A_WIDTH) // A_DK
    vec = lambda g: g.reshape(nl, 1, -1)
    return dict(
        w_hg=w_in[:, :, :o_hg].astype(BF16), w_sg=w_in[:, :, o_hg:o_sg].astype(BF16),
        w_lat=jnp.concatenate([w_in[:, :, o_sg:o_kr], lane_pad(w_kr)], axis=-1).astype(BF16),
        gq=vec(mla_q_norm_g), gkv=vec(mla_kv_norm_g),
        wq=_head_pad_cols(wq_nope, wq_rope).astype(BF16),
        wk=_head_pad_cols(wk_nope, zeros_rope).astype(BF16),
        wv=jnp.swapaxes(wv.reshape(nl, C_KV_RANK, C_WIDTH), 1, 2).astype(BF16),
        lb=vec(lower_bounds), hgrn_g=vec(hgrn_norm_g), sgu_g=vec(sgu_ln_g), sgu_b=vec(sgu_ln_b),
        ws=_pos_major(_pos_major(sgu_w_s * tri, 2), 3).astype(BF16),
        bs=_pos_major(jnp.repeat(jnp.swapaxes(sgu_b_s, 1, 2), B_CH, axis=2), 1),
        ones_bd=jnp.asarray(head_id[:, None] == head_id[None, :], BF16),
        w_out_ab=w_out[:, :A_WIDTH + B_WIDTH].astype(BF16), w_out_c=w_out[:, A_WIDTH + B_WIDTH:].astype(BF16),
        ln1_g=vec(ln1_g), ln1_b=vec(ln1_b),
        wg=w_gate_up[:, :, :D_FF].astype(BF16), wu=w_gate_up[:, :, D_FF:].astype(BF16),
        wd=w_down.astype(BF16), wpg=ple_w_gate.astype(BF16), wpp=ple_w_proj.astype(BF16),
        ln2_g=vec(ln2_g), ln2_b=vec(ln2_b),
    )


def kernel(x, p, positions, ln_in_g, ln_in_b, w_in, hgrn_lb_logits, hgrn_norm_g, sgu_ln_g, sgu_ln_b, sgu_w_s, sgu_b_s, mla_q_norm_g, mla_w_uq, mla_kv_norm_g, mla_w_ukv, w_out, ln1_g, ln1_b, w_gate_up, w_down, ple_w_gate, ple_w_proj, ln2_g, ln2_b):
    bsz, seq, d = x.shape
    t = bsz * seq
    lb_cum = jnp.cumsum(jax.nn.softmax(hgrn_lb_logits.astype(F32), axis=0), axis=0)
    lower_bounds = lb_cum - lb_cum[0]
    pos_pm = positions.reshape(bsz, seq // MIX_BLOCK, MIX_NCHUNK, A_CHUNK).swapaxes(2, 3).reshape(bsz, seq)
    rope_pats = _rope_tables(pos_pm)
    sw = _prep_weights(lower_bounds, w_in, hgrn_norm_g, sgu_ln_g, sgu_ln_b, sgu_w_s, sgu_b_s,
                       mla_q_norm_g, mla_w_uq, mla_kv_norm_g, mla_w_ukv, w_out, ln1_g, ln1_b,
                       w_gate_up, w_down, ple_w_gate, ple_w_proj, ln2_g, ln2_b)

    h = x.reshape(t, d)
    for i in range(DEPTH):
        if i == 0:
            hg, sg, q, k, v, h = _in_proj(h, rope_pats, sw, i, ln=(ln_in_g, ln_in_b))
        else:
            hg, sg, q, k, v = _in_proj(h, rope_pats, sw, i)
        o_ab = _mixers(hg, sg, sw, i, bsz, seq)
        o_c = _attention(q, k, v, bsz, seq)
        h = _channel(o_ab, o_c, h, p.reshape(DEPTH * t, PLE_DIM), sw, i, token_major_out=(i == DEPTH - 1))
    return h.reshape(bsz, seq, d)
```

```python
import functools
import math

import numpy as np
import jax
import jax.numpy as jnp
from jax import lax
from jax.experimental import pallas as pl
from jax.experimental.pallas import tpu as pltpu

F32 = jnp.float32
BF16 = jnp.bfloat16

D_MODEL = 1024
DEPTH = 2
PLE_DIM = 256

A_WIDTH = 256
A_DK = 64
A_HEADS = 4
A_CHUNK = 16
B_WIDTH = 256
B_CH = 64
B_GROUPS = 4
B_CHUNK = 128
C_WIDTH = 512
C_NOPE = 64
C_ROPE = 32
C_V = 64
C_HEADS = 8
C_Q_RANK = 384
C_KV_RANK = 256
ROPE_THETA = 10000.0
HEAD_PAD = 128

D_FF = 2816
LN_EPS = 1e-5
RMS_EPS = 1e-6
DEEPNORM_ALPHA = (2 * DEPTH) ** 0.25
ATT_SCALE = (C_NOPE + C_ROPE) ** -0.5
Q_SCALE = ATT_SCALE * math.log2(math.e)

MIX_BLOCK = 256
MIX_NCHUNK = MIX_BLOCK // A_CHUNK
ATT_TQ = 512
ATT_TK = 256
ATT_HEADS = 4
ATT_ONES = 16
ROW_BLOCK = 512
CH_BLOCK = 512
FF_CHUNK = 256
VMEM_LIMIT = 56 * 1024 * 1024
NEG_BIG = -1e30


def _dot(a, b):
    return jnp.dot(a, b, preferred_element_type=F32)


def _dot_nt(a, b):
    return lax.dot_general(a, b, (((1,), (1,)), ((), ())), preferred_element_type=F32)


def _dot_tn(a, b):
    return lax.dot_general(a, b, (((0,), (0,)), ((), ())), preferred_element_type=F32)


def _layer_norm(x, g, b):
    mu = jnp.mean(x, axis=-1, keepdims=True)
    xc = x - mu
    var = jnp.mean(xc * xc, axis=-1, keepdims=True)
    return xc * lax.rsqrt(var + LN_EPS) * g + b


def _rms_norm(x, g):
    return x * lax.rsqrt(jnp.mean(x * x, axis=-1, keepdims=True) + RMS_EPS) * g


def _sigmoid(x):
    return 1.0 / (1.0 + jnp.exp(-x))


def _const_spec(shape):
    nd = len(shape)
    return pl.BlockSpec(shape, lambda *_: (0,) * nd)


def _layer_spec(arr, layer, single_buffer=False):
    shape = arr.shape[1:]
    nd = len(shape)
    kw = dict(pipeline_mode=pl.Buffered(1)) if single_buffer else {}
    return pl.BlockSpec((None,) + shape, lambda *_: (layer,) + (0,) * nd, **kw)


def _params(sem):
    return pltpu.CompilerParams(dimension_semantics=sem, vmem_limit_bytes=VMEM_LIMIT)


def _rope_table_kernel(pos_ref, freq_ref, cos_ref, sin1_ref, sin2_ref, c_s, s_s):
    half = C_ROPE // 2
    g = pl.program_id(0)

    @pl.when(g == 0)
    def _():
        ang = pos_ref[...] * freq_ref[...]
        c_s[...] = jnp.cos(ang)
        s_s[...] = jnp.sin(ang)

    lane = lax.broadcasted_iota(jnp.int32, c_s.shape, 1)
    on_x1 = (lane >= C_NOPE) & (lane < C_NOPE + half)
    on_x2 = (lane >= C_NOPE + half) & (lane < C_NOPE + C_ROPE)
    to_x1 = (C_NOPE - g * half) % 128
    to_x2 = (C_NOPE + half - g * half) % 128
    c, s = c_s[...], s_s[...]
    cos_ref[...] = jnp.where(on_x1, pltpu.roll(c, to_x1, axis=1), jnp.where(on_x2, pltpu.roll(c, to_x2, axis=1), 1.0))
    sin1_ref[...] = jnp.where(on_x1, -pltpu.roll(s, to_x1, axis=1), 0.0)
    sin2_ref[...] = jnp.where(on_x2, pltpu.roll(s, to_x2, axis=1), 0.0)


def _rope_tables(positions):
    t = positions.size
    half = C_ROPE // 2
    groups = 128 // half
    rows = t // groups
    inv_freq = ROPE_THETA ** (-jnp.arange(0, C_ROPE, 2, dtype=F32) / C_ROPE)
    pos_rep = jnp.repeat(positions.astype(F32).reshape(groups, rows).T, half, axis=1)
    freq = jnp.tile(inv_freq, groups).reshape(1, 128)
    table = jax.ShapeDtypeStruct((t, HEAD_PAD), F32)
    out_spec = pl.BlockSpec((rows, HEAD_PAD), lambda g: (g, 0))
    return pl.pallas_call(
        _rope_table_kernel,
        out_shape=(table,) * 3,
        grid=(groups,),
        in_specs=[_const_spec((rows, 128)), _const_spec((1, 128))],
        out_specs=(out_spec,) * 3,
        scratch_shapes=[pltpu.VMEM((rows, 128), F32), pltpu.VMEM((rows, 128), F32)],
        compiler_params=_params(("arbitrary",)),
        name="rope_tables",
    )(pos_rep, freq)


def _read_pos_major(ref):
    return jnp.concatenate([ref[blk, :, p, :] for blk in range(ref.shape[0]) for p in range(A_CHUNK)], axis=0)


def _write_token_major(ref, val):
    for blk in range(ref.shape[0]):
        for p in range(A_CHUNK):
            r0 = blk * MIX_BLOCK + p * MIX_NCHUNK
            ref[blk, :, p, :] = val[r0:r0 + MIX_NCHUNK]


def _token_blocks(x2d):
    t, wd = x2d.shape
    return x2d.reshape(t // MIX_BLOCK, MIX_NCHUNK, A_CHUNK, wd)


def _token_block_spec(wd, rows=ROW_BLOCK, first_block=0):
    nblk = rows // MIX_BLOCK
    return pl.BlockSpec((nblk, MIX_NCHUNK, A_CHUNK, wd), lambda i: (first_block + i, 0, 0, 0))


def _rope_lanes(x, cos, sin_x1, sin_x2):
    half = C_ROPE // 2
    right = pltpu.roll(x, HEAD_PAD - half, axis=1)
    left = pltpu.roll(x, half, axis=1)
    return x * cos + right * sin_x1 + left * sin_x2


def _inproj_ln_kernel(x_ref, lng_ref, lnb_ref, *rest):
    *rest, h_out = rest
    h = _layer_norm(_read_pos_major(x_ref), lng_ref[...], lnb_ref[...])
    h_out[...] = h
    _inproj_body(h, *rest)


def _inproj_kernel(h_ref, *rest):
    _inproj_body(h_ref[...], *rest)


def _inproj_body(h, cos_ref, sin1_ref, sin2_ref, w_hg, w_sg, w_lat,
                 gq_ref, wq_ref, gkv_ref, wk_ref, wv_ref,
                 hg_out, sg_out, q_out, k_out, v_out):
    hb = h.astype(BF16)
    hg_out[...] = _dot(hb, w_hg[...])
    sg_out[...] = _dot(hb, w_sg[...])
    lat = _dot(hb, w_lat[...])
    cos, sin1, sin2 = cos_ref[...], sin1_ref[...], sin2_ref[...]

    cqn = _rms_norm(lat[:, 0:C_Q_RANK], gq_ref[...]).astype(BF16)
    q_lin = _dot(cqn, wq_ref[...])
    for h in range(C_HEADS):
        sl = slice(h * HEAD_PAD, (h + 1) * HEAD_PAD)
        q_out[:, sl] = (_rope_lanes(q_lin[:, sl], cos, sin1, sin2) * Q_SCALE).astype(BF16)

    ckvn = _rms_norm(lat[:, C_Q_RANK:C_Q_RANK + C_KV_RANK], gkv_ref[...]).astype(BF16)
    k_lin = _dot(ckvn, wk_ref[...])
    vt = _dot_nt(wv_ref[...], ckvn).astype(BF16)
    for c in range(v_out.shape[0]):
        v_out[c] = vt[:, c * ATT_TK:(c + 1) * ATT_TK]
    k_rope = _rope_lanes(lat[:, C_Q_RANK + C_KV_RANK:], cos, sin1, sin2)
    for h in range(C_HEADS):
        sl = slice(h * HEAD_PAD, (h + 1) * HEAD_PAD)
        k_out[:, sl] = (k_lin[:, sl] + k_rope).astype(BF16)


def _in_proj(h, rope_pats, sw, layer, ln=None):
    t = h.shape[0]
    tm = ROW_BLOCK
    row = lambda w: pl.BlockSpec((tm, w), lambda i: (i, 0))
    weights = [sw[n] for n in ("w_hg", "w_sg", "w_lat", "gq", "wq", "gkv", "wk", "wv")]
    hp = C_HEADS * HEAD_PAD
    out_shape = [jax.ShapeDtypeStruct((t, 4 * A_WIDTH), F32),
                 jax.ShapeDtypeStruct((t, 2 * B_WIDTH), F32),
                 jax.ShapeDtypeStruct((t, hp), BF16),
                 jax.ShapeDtypeStruct((t, hp), BF16),
                 jax.ShapeDtypeStruct((t // ATT_TK, C_WIDTH, ATT_TK), BF16)]
    out_specs = [row(4 * A_WIDTH), row(2 * B_WIDTH), row(hp), row(hp),
                 pl.BlockSpec((tm // ATT_TK, C_WIDTH, ATT_TK), lambda i: (i, 0, 0))]
    common_specs = [row(HEAD_PAD)] * 3 + [_layer_spec(w, layer) for w in weights]
    if ln is None:
        body, first_specs, first_args = _inproj_kernel, [row(D_MODEL)], [h]
    else:
        body = _inproj_ln_kernel
        first_specs = [_token_block_spec(D_MODEL), _const_spec((1, D_MODEL)), _const_spec((1, D_MODEL))]
        first_args = [_token_blocks(h), ln[0].reshape(1, D_MODEL), ln[1].reshape(1, D_MODEL)]
        out_shape.append(jax.ShapeDtypeStruct((t, D_MODEL), F32))
        out_specs.append(row(D_MODEL))
    return pl.pallas_call(
        body,
        out_shape=tuple(out_shape),
        grid=(t // tm,),
        in_specs=first_specs + common_specs,
        out_specs=tuple(out_specs),
        compiler_params=_params(("parallel",)),
        name="in_proj",
    )(*first_args, *rope_pats, *weights)


def _erf(x):
    return lax.erf(x)


def _gelu(x):
    return 0.5 * x * (1.0 + _erf(x * (2.0 ** -0.5)))


def _rows_to_array(rows, like):
    zero = jnp.zeros_like(like)
    return jnp.concatenate([zero if r is None else r for r in rows], axis=0)


def _mixer_kernel(hg_ref, sg_ref, lb_ref, ng_ref, lng_ref, lnb_ref, ws_ref, bs_ref, ones_ref,
                  o_ref,
                  q_s, k_s, f_s, v_s, qt_s, kt_s, oi_s, a_s, b_s, qs_s, ks_s, state_s):
    nc, cs, w = MIX_NCHUNK, A_CHUNK, A_WIDTH
    tb = nc * cs
    slab = lambda p: slice(p * nc, (p + 1) * nc)

    @pl.when(pl.program_id(1) == 0)
    def _():
        state_s[...] = jnp.zeros_like(state_s)

    lb = lb_ref[...]
    for p in range(cs):
        aq = hg_ref[slab(p), 0:w]
        f_p = lb + (1.0 - lb) * _sigmoid(hg_ref[slab(p), w:2 * w])
        q_s[p] = aq * _sigmoid(aq)
        f_s[p] = f_p
        k_s[p] = 1.0 - f_p
        v_s[p] = hg_ref[slab(p), 2 * w:3 * w]

    ones_bd = ones_ref[...]

    kd = []
    for p in range(cs):
        f_p = f_s[p]
        q_p = q_s[p]
        kd = [k_s[p]] + [f_p * x for x in kd]
        xs = jnp.concatenate([q_p * x for x in kd], axis=0).astype(BF16)
        wgt = _dot(xs, ones_bd)
        o_p = wgt[0:nc] * v_s[p]
        for d in range(1, p + 1):
            o_p = o_p + wgt[d * nc:(d + 1) * nc] * v_s[p - d]
        oi_s[p] = o_p

    pref = f_s[0]
    qt_s[0] = q_s[0] * pref
    for p in range(1, cs):
        pref = pref * f_s[p]
        qt_s[p] = q_s[p] * pref
    chunk_decay = pref
    suf = jnp.ones_like(pref)
    kt_s[cs - 1] = k_s[cs - 1]
    for p in range(cs - 2, -1, -1):
        suf = suf * f_s[p + 1]
        kt_s[p] = k_s[p] * suf

    drow = [chunk_decay[c:c + 1, :] for c in range(nc)]
    one_row = jnp.ones_like(drow[0])
    levels = []
    m = 1
    while m < nc:
        levels.append(m)
        m *= 2
    la_arrays, lb_arrays = [], []
    for m in levels:
        la = [None] * nc
        lbr = [None] * nc
        for base in range(0, nc, 2 * m):
            mid = base + m
            la[mid] = one_row
            for i in range(mid + 1, mid + m):
                la[i] = la[i - 1] * drow[i - 1]
            lbr[mid - 1] = one_row
            for j in range(mid - 2, base - 1, -1):
                lbr[j] = lbr[j + 1] * drow[j + 1]
        la_arrays.append(_rows_to_array(la, one_row))
        lb_arrays.append(_rows_to_array(lbr, one_row))
    ep = [one_row]
    for c in range(1, nc):
        ep.append(ep[-1] * drow[c - 1])
    es = [one_row] * nc
    for c in range(nc - 2, -1, -1):
        es[c] = es[c + 1] * drow[c + 1]
    total_decay = ep[-1] * drow[nc - 1]
    ep_arr = jnp.concatenate(ep, axis=0)
    es_arr = jnp.concatenate(es, axis=0)

    for p in range(cs):
        qt_p = qt_s[p]
        kt_p = kt_s[p]
        for li in range(len(levels)):
            a_s[li, p] = qt_p * la_arrays[li]
            b_s[li, p] = kt_p * lb_arrays[li]
        qs_s[p] = qt_p * ep_arr
        ks_s[p] = kt_p * es_arr

    lane = lax.broadcasted_iota(jnp.int32, (tb, w), 1)
    head_masks = [(lane >= h * A_DK) & (lane < (h + 1) * A_DK) for h in range(A_HEADS)]
    row_c = lax.broadcasted_iota(jnp.int32, (tb, tb), 0) % nc
    col_c = lax.broadcasted_iota(jnp.int32, (tb, tb), 1) % nc
    v_bf = v_s[...].reshape(tb, w).astype(BF16)
    a_tok = [a_s[li].reshape(tb, w) for li in range(len(levels))]
    b_tok = [b_s[li].reshape(tb, w).astype(BF16) for li in range(len(levels))]
    o_cross = jnp.zeros((tb, w), F32)
    for h in range(A_HEADS):
        sc = None
        for li, m in enumerate(levels):
            a_h = jnp.where(head_masks[h], a_tok[li], 0.0).astype(BF16)
            s_l = _dot_nt(a_h, b_tok[li])
            if 2 * m < nc:
                s_l = jnp.where((row_c // (2 * m)) == (col_c // (2 * m)), s_l, 0.0)
            sc = s_l if sc is None else sc + s_l
        o_h = _dot(sc.astype(BF16), v_bf)
        o_cross = o_cross + jnp.where(head_masks[h], o_h, 0.0)

    st = state_s[...]
    o_state = _dot_nt(qs_s[...].reshape(tb, w).astype(BF16), st.astype(BF16))
    kv = _dot_tn(v_bf, ks_s[...].reshape(tb, w).astype(BF16))
    rr = lax.broadcasted_iota(jnp.int32, (w, w), 0)
    cc = lax.broadcasted_iota(jnp.int32, (w, w), 1)
    state_s[...] = st * total_decay + jnp.where((rr // A_DK) == (cc // A_DK), kv, 0.0)

    o = oi_s[...].reshape(tb, w) + o_cross + o_state
    ms = _dot((o * o).astype(BF16), ones_bd) * (1.0 / A_DK)
    ag = hg_ref[:, 3 * w:4 * w]
    o_ref[:, 0:w] = (o * lax.rsqrt(ms + RMS_EPS) * ng_ref[...] * (ag * _sigmoid(ag))).astype(o_ref.dtype)

    cps = B_CHUNK // cs
    u = _gelu(sg_ref[:, 0:B_WIDTH])
    vv = _layer_norm(_gelu(sg_ref[:, B_WIDTH:2 * B_WIDTH]), lng_ref[...], lnb_ref[...])
    lane_b = lax.broadcasted_iota(jnp.int32, (B_CHUNK, B_WIDTH), 1)
    zs = []
    for ci in range(nc // cps):
        v_c = jnp.concatenate([vv[p * nc + ci * cps:p * nc + (ci + 1) * cps] for p in range(cs)],
                              axis=0).astype(BF16)
        z = bs_ref[...]
        for g in range(B_GROUPS):
            z_g = _dot(ws_ref[g], v_c)
            z = z + jnp.where((lane_b >= g * B_CH) & (lane_b < (g + 1) * B_CH), z_g, 0.0)
        zs.append(z)
    z_all = jnp.concatenate([z[p * cps:(p + 1) * cps] for p in range(cs) for z in zs], axis=0)
    o_ref[:, w:w + B_WIDTH] = (u * z_all).astype(o_ref.dtype)


def _mixers(hg, sg, sw, layer, bsz, seq):
    nb = seq // MIX_BLOCK
    nc, cs, w = MIX_NCHUNK, A_CHUNK, A_WIDTH
    nlev = int(round(math.log2(nc)))
    blk = lambda width: pl.BlockSpec((MIX_BLOCK, width), lambda b, i: (b * nb + i, 0))
    pm = lambda: pltpu.VMEM((cs, nc, w), F32)
    ow = A_WIDTH + B_WIDTH
    return pl.pallas_call(
        _mixer_kernel,
        out_shape=jax.ShapeDtypeStruct((bsz * seq, ow), BF16),
        grid=(bsz, nb),
        in_specs=[blk(4 * A_WIDTH), blk(2 * B_WIDTH),
                  ] + [_layer_spec(sw[n], layer) for n in ("lb", "hgrn_g", "sgu_g", "sgu_b", "ws", "bs")] + [
                  _const_spec((w, w))],
        out_specs=blk(ow),
        scratch_shapes=[pm(), pm(), pm(), pm(), pm(), pm(), pm(),
                        pltpu.VMEM((nlev, cs, nc, w), F32), pltpu.VMEM((nlev, cs, nc, w), F32),
                        pm(), pm(), pltpu.VMEM((w, w), F32)],
        compiler_params=_params(("parallel", "arbitrary")),
        name="mixers",
    )(hg, sg, sw["lb"], sw["hgrn_g"], sw["sgu_g"], sw["sgu_b"], sw["ws"], sw["bs"], sw["ones_bd"])


def _token_offset(i):
    r = i % MIX_BLOCK
    return (i - r) + (r % MIX_NCHUNK) * A_CHUNK + r // MIX_NCHUNK


def _attn_kernel(q_ref, k_ref, vt_ref, o_ref, sa_ref, sb_ref, acca_ref, accb_ref):
    tq, tk = ATT_TQ, ATT_TK
    qi = pl.program_id(2)
    qs = [q_ref[:, hh * HEAD_PAD:(hh + 1) * HEAD_PAD] for hh in range(ATT_HEADS)]

    def produce(hh, j, dst, mask=None, cols=slice(None)):
        off = pl.multiple_of(j * tk, tk)
        k_t = k_ref[pl.ds(off, tk), hh * HEAD_PAD:(hh + 1) * HEAD_PAD]
        st = _dot_nt(k_t, qs[hh][cols])
        if mask is not None:
            st = jnp.where(mask, st, NEG_BIG)
        dst[hh, :, cols] = st
        return jnp.max(st, axis=0, keepdims=True)

    ones_rows = jnp.ones((ATT_ONES, tk), BF16)

    def consume(hh, j, src, tile_max, acc, m_old, cols=slice(None)):
        m_new = jnp.maximum(m_old, tile_max)
        alpha = jnp.exp2(m_old - m_new)
        pt = jnp.exp2(src[hh, :, cols] - m_new).astype(BF16)
        vt_h = jnp.concatenate([vt_ref[j, hh * C_V:(hh + 1) * C_V, :], ones_rows], axis=0)
        acc[hh, :, cols] = alpha * acc[hh, :, cols] + _dot(vt_h, pt)
        return m_new

    def overlap(prod, cons):
        maxes, ms = [], []
        for hh in range(ATT_HEADS):
            if prod is not None:
                maxes.append(produce(hh, *prod))
            if cons is not None:
                j, src, tile_max, acc, m_old = cons
                ms.append(consume(hh, j, src, tile_max[hh], acc, m_old[hh]))
        return tuple(maxes), tuple(ms)

    key = _token_offset(lax.broadcasted_iota(jnp.int32, (tk, tq), 0))
    qry = _token_offset(lax.broadcasted_iota(jnp.int32, (tk, tq), 1))
    d0 = 2 * qi
    acca_ref[...] = jnp.zeros_like(acca_ref)
    accb_ref[...] = jnp.zeros_like(accb_ref)
    max_a, _ = overlap((d0, sa_ref, key <= qry), None)
    ms0 = tuple(jnp.full((1, tq), NEG_BIG, F32) for _ in range(ATT_HEADS))

    def body(p, carry):
        ms_a, ms_b, in_a, max_a = carry
        max_b, ms_a = overlap((2 * p, sb_ref), (in_a, sa_ref, max_a, acca_ref, ms_a))
        max_a, ms_b = overlap((2 * p + 1, sa_ref), (2 * p, sb_ref, max_b, accb_ref, ms_b))
        return ms_a, ms_b, 2 * p + 1, max_a

    ms_a, ms_b, in_a, max_a = lax.fori_loop(0, qi, body, (ms0, ms0, d0, max_a))
    late = slice(tq - tk, tq)
    max_b, ms_a = overlap((d0 + 1, sb_ref, (key <= qry)[:, 0:tk], late), (in_a, sa_ref, max_a, acca_ref, ms_a))
    ms_b = tuple(
        jnp.concatenate([ms_b[hh][:, 0:tq - tk],
                         consume(hh, d0 + 1, sb_ref, max_b[hh], accb_ref, ms_b[hh][:, late], late)], axis=1)
        for hh in range(ATT_HEADS))
    for hh in range(ATT_HEADS):
        m = jnp.maximum(ms_a[hh], ms_b[hh])
        acc = acca_ref[hh] * jnp.exp2(ms_a[hh] - m) + accb_ref[hh] * jnp.exp2(ms_b[hh] - m)
        o_ref[hh * C_V:(hh + 1) * C_V, :] = (acc[0:C_V] / acc[C_V:C_V + 1]).astype(o_ref.dtype)


def _attention(q, k, vt, bsz, seq):
    assert ATT_TQ == 2 * ATT_TK
    nh = ATT_HEADS
    nq = seq // ATT_TQ
    nk = seq // ATT_TK
    return pl.pallas_call(
        _attn_kernel,
        out_shape=jax.ShapeDtypeStruct((bsz * nq, C_WIDTH, ATT_TQ), BF16),
        grid=(bsz, C_HEADS // nh, nq),
        in_specs=[pl.BlockSpec((None, ATT_TQ, nh * HEAD_PAD), lambda b, g, i: (b, i, g)),
                  pl.BlockSpec((None, seq, nh * HEAD_PAD), lambda b, g, i: (b, 0, g)),
                  pl.BlockSpec((nk, nh * C_V, ATT_TK), lambda b, g, i: (b, g, 0))],
        out_specs=pl.BlockSpec((None, nh * C_V, ATT_TQ), lambda b, g, i: (b * nq + i, g, 0)),
        scratch_shapes=[pltpu.VMEM((nh, ATT_TK, ATT_TQ), F32), pltpu.VMEM((nh, ATT_TK, ATT_TQ), F32),
                        pltpu.VMEM((nh, C_V + ATT_ONES, ATT_TQ), F32), pltpu.VMEM((nh, C_V + ATT_ONES, ATT_TQ), F32)],
        compiler_params=_params(("parallel", "parallel", "arbitrary")),
        name="mla_attention",
    )(q.reshape(bsz, seq, -1), k.reshape(bsz, seq, -1), vt)


def _channel_kernel(oab_ref, oct_ref, h_ref, p_ref, wab_ref, wc_ref, g1_ref, b1_ref,
                    wgu_ref, wd_ref, wpg_ref, wpp_ref, g2_ref, b2_ref, o_ref, act_s, *,
                    token_major_out):
    mix_c = jnp.concatenate([_dot_tn(oct_ref[i], wc_ref[...]) for i in range(oct_ref.shape[0])], axis=0)
    mix = _dot(oab_ref[...], wab_ref[...]) + mix_c
    h = _layer_norm(DEEPNORM_ALPHA * h_ref[...] + mix, g1_ref[...], b1_ref[...])
    hb = h.astype(BF16)
    for c in range(D_FF // FF_CHUNK):
        cols = slice(c * FF_CHUNK, (c + 1) * FF_CHUNK)
        gate = _dot(hb, wgu_ref[:, cols])
        up = _dot(hb, wgu_ref[:, D_FF + c * FF_CHUNK:D_FF + (c + 1) * FF_CHUNK])
        act_s[:, cols] = (gate * _sigmoid(gate) * up).astype(BF16)
    ffn = _dot(act_s[...], wd_ref[...])
    ple = _sigmoid(_dot(hb, wpg_ref[...])) * _dot(_read_pos_major(p_ref).astype(BF16), wpp_ref[...])
    out = _layer_norm(DEEPNORM_ALPHA * h + ffn + ple, g2_ref[...], b2_ref[...])
    if token_major_out:
        _write_token_major(o_ref, out)
    else:
        o_ref[...] = out


def _channel(o_ab, o_ct, h, p, sw, layer, token_major_out):
    t = h.shape[0]
    tm = CH_BLOCK
    row = lambda w: pl.BlockSpec((tm, w), lambda i: (i, 0))
    if token_major_out:
        out_shape = jax.ShapeDtypeStruct((t // MIX_BLOCK, MIX_NCHUNK, A_CHUNK, D_MODEL), F32)
        out_spec = _token_block_spec(D_MODEL, tm)
    else:
        out_shape = jax.ShapeDtypeStruct((t, D_MODEL), F32)
        out_spec = row(D_MODEL)
    names = ("w_out_ab", "w_out_c", "ln1_g", "ln1_b", "wgu", "wd", "wpg", "wpp", "ln2_g", "ln2_b")
    return pl.pallas_call(
        functools.partial(_channel_kernel, token_major_out=token_major_out),
        out_shape=out_shape,
        grid=(t // tm,),
        in_specs=[row(A_WIDTH + B_WIDTH), pl.BlockSpec((tm // ATT_TQ, C_WIDTH, ATT_TQ), lambda i: (i, 0, 0)), row(D_MODEL),
                  _token_block_spec(PLE_DIM, tm, layer * (t // tm))] + [_layer_spec(sw[n], layer, single_buffer=True) for n in names],
        out_specs=out_spec,
        scratch_shapes=[pltpu.VMEM((tm, D_FF), BF16)],
        compiler_params=_params(("parallel",)),
        name="channel_mix",
    )(o_ab, o_ct, h, _token_blocks(p), *[sw[n] for n in names])


def _head_pad_cols(w_nope, w_rope):
    lead = w_nope.shape[:2]
    pad = jnp.zeros(lead + (C_HEADS, HEAD_PAD - C_NOPE - C_ROPE), w_nope.dtype)
    return jnp.concatenate([w_nope, w_rope, pad], axis=-1).reshape(lead + (C_HEADS * HEAD_PAD,))


def _pos_major(x, axis):
    cps = B_CHUNK // A_CHUNK
    shape = x.shape
    x = x.reshape(shape[:axis] + (cps, A_CHUNK) + shape[axis + 1:])
    return jnp.swapaxes(x, axis, axis + 1).reshape(shape)


def _prep_weights(lower_bounds, w_in, hgrn_norm_g, sgu_ln_g, sgu_ln_b, sgu_w_s, sgu_b_s,
                  mla_q_norm_g, mla_w_uq, mla_kv_norm_g, mla_w_ukv, w_out, ln1_g, ln1_b,
                  w_gate_up, w_down, ple_w_gate, ple_w_proj, ln2_g, ln2_b):
    nl = w_in.shape[0]
    o_hg, o_sg = 4 * A_WIDTH, 4 * A_WIDTH + 2 * B_WIDTH
    o_kv = o_sg + C_Q_RANK
    o_kr = o_kv + C_KV_RANK
    w_kr = w_in[:, :, o_kr:o_kr + C_ROPE]
    lane_pad = lambda wr: jnp.concatenate(
        [jnp.zeros((nl, D_MODEL, C_NOPE), F32), wr, jnp.zeros((nl, D_MODEL, HEAD_PAD - C_NOPE - C_ROPE), F32)],
        axis=-1)
    wq = mla_w_uq.reshape(nl, C_Q_RANK, C_HEADS, C_NOPE + C_ROPE)
    wq_nope, wq_rope = wq[..., :C_NOPE], wq[..., C_NOPE:]
    wkv = mla_w_ukv.reshape(nl, C_KV_RANK, C_HEADS, C_NOPE + C_V)
    wk_nope, wv = wkv[..., :C_NOPE], wkv[..., C_NOPE:]
    zeros_rope = jnp.zeros((nl, C_KV_RANK, C_HEADS, C_ROPE), F32)
    tri = jnp.tril(jnp.ones((B_CHUNK, B_CHUNK), F32))
    head_id = np.arange(A_WIDTH) // A_DK
    vec = lambda g: g.reshape(nl, 1, -1)
    return dict(
        w_hg=w_in[:, :, :o_hg].astype(BF16), w_sg=w_in[:, :, o_hg:o_sg].astype(BF16),
        w_lat=jnp.concatenate([w_in[:, :, o_sg:o_kr], lane_pad(w_kr)], axis=-1).astype(BF16),
        gq=vec(mla_q_norm_g), gkv=vec(mla_kv_norm_g),
        wq=_head_pad_cols(wq_nope, wq_rope).astype(BF16),
        wk=_head_pad_cols(wk_nope, zeros_rope).astype(BF16),
        wv=jnp.swapaxes(wv.reshape(nl, C_KV_RANK, C_WIDTH), 1, 2).astype(BF16),
        lb=vec(lower_bounds), hgrn_g=vec(hgrn_norm_g), sgu_g=vec(sgu_ln_g), sgu_b=vec(sgu_ln_b),
        ws=_pos_major(_pos_major(sgu_w_s * tri, 2), 3).astype(BF16),
        bs=_pos_major(jnp.repeat(jnp.swapaxes(sgu_b_s, 1, 2), B_CH, axis=2), 1),
        ones_bd=jnp.asarray(head_id[:, None] == head_id[None, :], BF16),
        w_out_ab=w_out[:, :A_WIDTH + B_WIDTH].astype(BF16), w_out_c=w_out[:, A_WIDTH + B_WIDTH:].astype(BF16),
        ln1_g=vec(ln1_g), ln1_b=vec(ln1_b),
        wgu=w_gate_up.astype(BF16),
        wd=w_down.astype(BF16), wpg=ple_w_gate.astype(BF16), wpp=ple_w_proj.astype(BF16),
        ln2_g=vec(ln2_g), ln2_b=vec(ln2_b),
    )


def kernel(x, p, positions, ln_in_g, ln_in_b, w_in, hgrn_lb_logits, hgrn_norm_g, sgu_ln_g, sgu_ln_b, sgu_w_s, sgu_b_s, mla_q_norm_g, mla_w_uq, mla_kv_norm_g, mla_w_ukv, w_out, ln1_g, ln1_b, w_gate_up, w_down, ple_w_gate, ple_w_proj, ln2_g, ln2_b):
    bsz, seq, d = x.shape
    t = bsz * seq
    lb_cum = jnp.cumsum(jax.nn.softmax(hgrn_lb_logits.astype(F32), axis=0), axis=0)
    lower_bounds = lb_cum - lb_cum[0]
    pos_pm = positions.reshape(bsz, seq // MIX_BLOCK, MIX_NCHUNK, A_CHUNK).swapaxes(2, 3).reshape(bsz, seq)
    rope_pats = _rope_tables(pos_pm)
    sw = _prep_weights(lower_bounds, w_in, hgrn_norm_g, sgu_ln_g, sgu_ln_b, sgu_w_s, sgu_b_s,
                       mla_q_norm_g, mla_w_uq, mla_kv_norm_g, mla_w_ukv, w_out, ln1_g, ln1_b,
                       w_gate_up, w_down, ple_w_gate, ple_w_proj, ln2_g, ln2_b)

    h = x.reshape(t, d)
    for i in range(DEPTH):
        if i == 0:
            hg, sg, q, k, v, h = _in_proj(h, rope_pats, sw, i, ln=(ln_in_g, ln_in_b))
        else:
            hg, sg, q, k, v = _in_proj(h, rope_pats, sw, i)
        o_ab = _mixers(hg, sg, sw, i, bsz, seq)
        o_c = _attention(q, k, v, bsz, seq)
        h = _channel(o_ab, o_c, h, p.reshape(DEPTH * t, PLE_DIM), sw, i, token_major_out=(i == DEPTH - 1))
    return h.reshape(bsz, seq, d)
```

```python
import functools
import math

import numpy as np
import jax
import jax.numpy as jnp
from jax import lax
from jax.experimental import pallas as pl
from jax.experimental.pallas import tpu as pltpu

F32 = jnp.float32
BF16 = jnp.bfloat16

D_MODEL = 1024
DEPTH = 2
PLE_DIM = 256

A_WIDTH = 256
A_DK = 64
A_HEADS = 4
A_CHUNK = 16
B_WIDTH = 256
B_CH = 64
B_GROUPS = 4
B_CHUNK = 128
C_WIDTH = 512
C_NOPE = 64
C_ROPE = 32
C_V = 64
C_HEADS = 8
C_Q_RANK = 384
C_KV_RANK = 256
ROPE_THETA = 10000.0
HEAD_PAD = 128

D_FF = 2816
LN_EPS = 1e-5
RMS_EPS = 1e-6
DEEPNORM_ALPHA = (2 * DEPTH) ** 0.25
ATT_SCALE = (C_NOPE + C_ROPE) ** -0.5
Q_SCALE = ATT_SCALE * math.log2(math.e)

MIX_BLOCK = 256
MIX_NCHUNK = MIX_BLOCK // A_CHUNK
ATT_TQ = 512
ATT_TK = 256
ATT_HEADS = 8
ATT_ONES = 16
ROW_BLOCK = 512
CH_BLOCK = 512
FF_CHUNK = 256
VMEM_LIMIT = 56 * 1024 * 1024
NEG_BIG = -1e30


def _dot(a, b):
    return jnp.dot(a, b, preferred_element_type=F32)


def _dot_nt(a, b):
    return lax.dot_general(a, b, (((1,), (1,)), ((), ())), preferred_element_type=F32)


def _dot_tn(a, b):
    return lax.dot_general(a, b, (((0,), (0,)), ((), ())), preferred_element_type=F32)


def _layer_norm(x, g, b):
    mu = jnp.mean(x, axis=-1, keepdims=True)
    xc = x - mu
    var = jnp.mean(xc * xc, axis=-1, keepdims=True)
    return xc * lax.rsqrt(var + LN_EPS) * g + b


def _rms_norm(x, g):
    return x * lax.rsqrt(jnp.mean(x * x, axis=-1, keepdims=True) + RMS_EPS) * g


def _sigmoid(x):
    return 1.0 / (1.0 + jnp.exp(-x))


def _const_spec(shape):
    nd = len(shape)
    return pl.BlockSpec(shape, lambda *_: (0,) * nd)


def _layer_spec(arr, layer, single_buffer=False):
    shape = arr.shape[1:]
    nd = len(shape)
    kw = dict(pipeline_mode=pl.Buffered(1)) if single_buffer else {}
    return pl.BlockSpec((None,) + shape, lambda *_: (layer,) + (0,) * nd, **kw)


def _params(sem):
    return pltpu.CompilerParams(dimension_semantics=sem, vmem_limit_bytes=VMEM_LIMIT)


def _rope_table_kernel(pos_ref, freq_ref, cos_ref, sin1_ref, sin2_ref, c_s, s_s):
    half = C_ROPE // 2
    g = pl.program_id(0)

    @pl.when(g == 0)
    def _():
        ang = pos_ref[...] * freq_ref[...]
        c_s[...] = jnp.cos(ang)
        s_s[...] = jnp.sin(ang)

    lane = lax.broadcasted_iota(jnp.int32, c_s.shape, 1)
    on_x1 = (lane >= C_NOPE) & (lane < C_NOPE + half)
    on_x2 = (lane >= C_NOPE + half) & (lane < C_NOPE + C_ROPE)
    to_x1 = (C_NOPE - g * half) % 128
    to_x2 = (C_NOPE + half - g * half) % 128
    c, s = c_s[...], s_s[...]
    nope = jnp.where(lane < C_NOPE, 1.0, 0.0)
    cos_ref[...] = jnp.where(on_x1, pltpu.roll(c, to_x1, axis=1), jnp.where(on_x2, pltpu.roll(c, to_x2, axis=1), nope))
    sin1_ref[...] = jnp.where(on_x1, -pltpu.roll(s, to_x1, axis=1), 0.0)
    sin2_ref[...] = jnp.where(on_x2, pltpu.roll(s, to_x2, axis=1), 0.0)


def _rope_tables(positions):
    t = positions.size
    half = C_ROPE // 2
    groups = 128 // half
    rows = t // groups
    inv_freq = ROPE_THETA ** (-jnp.arange(0, C_ROPE, 2, dtype=F32) / C_ROPE)
    pos_rep = jnp.repeat(positions.astype(F32).reshape(groups, rows).T, half, axis=1)
    freq = jnp.tile(inv_freq, groups).reshape(1, 128)
    table = jax.ShapeDtypeStruct((t, HEAD_PAD), F32)
    out_spec = pl.BlockSpec((rows, HEAD_PAD), lambda g: (g, 0))
    return pl.pallas_call(
        _rope_table_kernel,
        out_shape=(table,) * 3,
        grid=(groups,),
        in_specs=[_const_spec((rows, 128)), _const_spec((1, 128))],
        out_specs=(out_spec,) * 3,
        scratch_shapes=[pltpu.VMEM((rows, 128), F32), pltpu.VMEM((rows, 128), F32)],
        compiler_params=_params(("arbitrary",)),
        name="rope_tables",
    )(pos_rep, freq)


def _read_pos_major(ref):
    return jnp.concatenate([ref[blk, :, p, :] for blk in range(ref.shape[0]) for p in range(A_CHUNK)], axis=0)


def _write_token_major(ref, val):
    for blk in range(ref.shape[0]):
        for p in range(A_CHUNK):
            r0 = blk * MIX_BLOCK + p * MIX_NCHUNK
            ref[blk, :, p, :] = val[r0:r0 + MIX_NCHUNK]


def _token_blocks(x2d):
    t, wd = x2d.shape
    return x2d.reshape(t // MIX_BLOCK, MIX_NCHUNK, A_CHUNK, wd)


def _token_block_spec(wd, rows=ROW_BLOCK, first_block=0):
    nblk = rows // MIX_BLOCK
    return pl.BlockSpec((nblk, MIX_NCHUNK, A_CHUNK, wd), lambda i: (first_block + i, 0, 0, 0))


def _rope_lanes(x, cos, sin_x1, sin_x2):
    half = C_ROPE // 2
    right = pltpu.roll(x, HEAD_PAD - half, axis=1)
    left = pltpu.roll(x, half, axis=1)
    return x * cos + right * sin_x1 + left * sin_x2


def _inproj_ln_kernel(x_ref, lng_ref, lnb_ref, *rest):
    *rest, h_out = rest
    h = _layer_norm(_read_pos_major(x_ref), lng_ref[...], lnb_ref[...])
    h_out[...] = h
    _inproj_body(h, *rest)


def _inproj_kernel(h_ref, *rest):
    _inproj_body(h_ref[...], *rest)


def _inproj_body(h, cos_ref, sin1_ref, sin2_ref, w_in,
                 gq_ref, wq_ref, gkv_ref, wk_ref, wv_ref,
                 hg_out, sg_out, q_out, k_out, v_out):
    hb = h.astype(BF16)
    o_sg, o_lat = 4 * A_WIDTH, 4 * A_WIDTH + 2 * B_WIDTH
    proj = _dot(hb, w_in[...])
    hg_out[...] = proj[:, 0:o_sg]
    sg_out[...] = proj[:, o_sg:o_lat]
    lat = proj[:, o_lat:]
    cos, sin1, sin2 = cos_ref[...], sin1_ref[...], sin2_ref[...]

    q_table = (cos + pltpu.roll(sin1 + sin2, C_ROPE, axis=1)) * Q_SCALE
    cqn = _rms_norm(lat[:, 0:C_Q_RANK], gq_ref[...]).astype(BF16)
    q_lin = _dot(cqn, wq_ref[...])
    for h in range(C_HEADS):
        sl = slice(h * HEAD_PAD, (h + 1) * HEAD_PAD)
        q_out[:, sl] = (q_lin[:, sl] * q_table).astype(BF16)

    ckvn = _rms_norm(lat[:, C_Q_RANK:C_Q_RANK + C_KV_RANK], gkv_ref[...]).astype(BF16)
    k_lin = _dot(ckvn, wk_ref[...])
    vt = _dot_nt(wv_ref[...], ckvn).astype(BF16)
    for c in range(v_out.shape[0]):
        v_out[c] = vt[:, c * ATT_TK:(c + 1) * ATT_TK]
    k_rope = _rope_lanes(lat[:, C_Q_RANK + C_KV_RANK:], cos, sin1, sin2)
    k_rope = k_rope + pltpu.roll(k_rope, C_ROPE, axis=1)
    for h in range(C_HEADS):
        sl = slice(h * HEAD_PAD, (h + 1) * HEAD_PAD)
        k_out[:, sl] = (k_lin[:, sl] + k_rope).astype(BF16)


def _in_proj(h, rope_pats, sw, layer, ln=None):
    t = h.shape[0]
    tm = ROW_BLOCK
    row = lambda w: pl.BlockSpec((tm, w), lambda i: (i, 0))
    weights = [sw[n] for n in ("w_in", "gq", "wq", "gkv", "wk", "wv")]
    hp = C_HEADS * HEAD_PAD
    out_shape = [jax.ShapeDtypeStruct((t, 4 * A_WIDTH), F32),
                 jax.ShapeDtypeStruct((t, 2 * B_WIDTH), F32),
                 jax.ShapeDtypeStruct((t, hp), BF16),
                 jax.ShapeDtypeStruct((t, hp), BF16),
                 jax.ShapeDtypeStruct((t // ATT_TK, C_WIDTH, ATT_TK), BF16)]
    out_specs = [row(4 * A_WIDTH), row(2 * B_WIDTH), row(hp), row(hp),
                 pl.BlockSpec((tm // ATT_TK, C_WIDTH, ATT_TK), lambda i: (i, 0, 0))]
    common_specs = [row(HEAD_PAD)] * 3 + [_layer_spec(w, layer) for w in weights]
    if ln is None:
        body, first_specs, first_args = _inproj_kernel, [row(D_MODEL)], [h]
    else:
        body = _inproj_ln_kernel
        first_specs = [_token_block_spec(D_MODEL), _const_spec((1, D_MODEL)), _const_spec((1, D_MODEL))]
        first_args = [_token_blocks(h), ln[0].reshape(1, D_MODEL), ln[1].reshape(1, D_MODEL)]
        out_shape.append(jax.ShapeDtypeStruct((t, D_MODEL), F32))
        out_specs.append(row(D_MODEL))
    return pl.pallas_call(
        body,
        out_shape=tuple(out_shape),
        grid=(t // tm,),
        in_specs=first_specs + common_specs,
        out_specs=tuple(out_specs),
        compiler_params=_params(("parallel",)),
        name="in_proj",
    )(*first_args, *rope_pats, *weights)


def _erf(x):
    return lax.erf(x)


def _gelu(x):
    return 0.5 * x * (1.0 + _erf(x * (2.0 ** -0.5)))


def _rows_to_array(rows, like):
    zero = jnp.zeros_like(like)
    return jnp.concatenate([zero if r is None else r for r in rows], axis=0)


def _mixer_kernel(hg_ref, sg_ref, lb_ref, ng_ref, lng_ref, lnb_ref, ws_ref, bs_ref, ones_ref,
                  o_ref,
                  q_s, k_s, f_s, v_s, qt_s, kt_s, oi_s, a_s, b_s, qs_s, ks_s, state_s):
    nc, cs, w = MIX_NCHUNK, A_CHUNK, A_WIDTH
    tb = nc * cs
    slab = lambda p: slice(p * nc, (p + 1) * nc)

    @pl.when(pl.program_id(1) == 0)
    def _():
        state_s[...] = jnp.zeros_like(state_s)

    lb = lb_ref[...]
    for p in range(cs):
        aq = hg_ref[slab(p), 0:w]
        f_p = lb + (1.0 - lb) * _sigmoid(hg_ref[slab(p), w:2 * w])
        q_s[p] = aq * _sigmoid(aq)
        f_s[p] = f_p
        k_s[p] = 1.0 - f_p
        v_s[p] = hg_ref[slab(p), 2 * w:3 * w]

    ones_bd = ones_ref[...]

    kd = []
    for p in range(cs):
        f_p = f_s[p]
        q_p = q_s[p]
        kd = [k_s[p]] + [f_p * x for x in kd]
        xs = jnp.concatenate([q_p * x for x in kd], axis=0).astype(BF16)
        wgt = _dot(xs, ones_bd)
        o_p = wgt[0:nc] * v_s[p]
        for d in range(1, p + 1):
            o_p = o_p + wgt[d * nc:(d + 1) * nc] * v_s[p - d]
        oi_s[p] = o_p

    pref = f_s[0]
    qt_s[0] = q_s[0] * pref
    for p in range(1, cs):
        pref = pref * f_s[p]
        qt_s[p] = q_s[p] * pref
    chunk_decay = pref
    suf = jnp.ones_like(pref)
    kt_s[cs - 1] = k_s[cs - 1]
    for p in range(cs - 2, -1, -1):
        suf = suf * f_s[p + 1]
        kt_s[p] = k_s[p] * suf

    drow = [chunk_decay[c:c + 1, :] for c in range(nc)]
    one_row = jnp.ones_like(drow[0])
    levels = []
    m = 1
    while m < nc:
        levels.append(m)
        m *= 2
    la_arrays, lb_arrays = [], []
    for m in levels:
        la = [None] * nc
        lbr = [None] * nc
        for base in range(0, nc, 2 * m):
            mid = base + m
            la[mid] = one_row
            for i in range(mid + 1, mid + m):
                la[i] = la[i - 1] * drow[i - 1]
            lbr[mid - 1] = one_row
            for j in range(mid - 2, base - 1, -1):
                lbr[j] = lbr[j + 1] * drow[j + 1]
        la_arrays.append(_rows_to_array(la, one_row))
        lb_arrays.append(_rows_to_array(lbr, one_row))
    ep = [one_row]
    for c in range(1, nc):
        ep.append(ep[-1] * drow[c - 1])
    es = [one_row] * nc
    for c in range(nc - 2, -1, -1):
        es[c] = es[c + 1] * drow[c + 1]
    total_decay = ep[-1] * drow[nc - 1]
    ep_arr = jnp.concatenate(ep, axis=0)
    es_arr = jnp.concatenate(es, axis=0)

    for p in range(cs):
        qt_p = qt_s[p]
        kt_p = kt_s[p]
        for li in range(len(levels)):
            a_s[li, p] = qt_p * la_arrays[li]
            b_s[li, p] = kt_p * lb_arrays[li]
        qs_s[p] = qt_p * ep_arr
        ks_s[p] = kt_p * es_arr

    lane = lax.broadcasted_iota(jnp.int32, (tb, w), 1)
    head_masks = [(lane >= h * A_DK) & (lane < (h + 1) * A_DK) for h in range(A_HEADS)]
    row_c = lax.broadcasted_iota(jnp.int32, (tb, tb), 0) % nc
    col_c = lax.broadcasted_iota(jnp.int32, (tb, tb), 1) % nc
    v_bf = v_s[...].reshape(tb, w).astype(BF16)
    a_tok = [a_s[li].reshape(tb, w) for li in range(len(levels))]
    b_tok = [b_s[li].reshape(tb, w).astype(BF16) for li in range(len(levels))]
    o_cross = jnp.zeros((tb, w), F32)
    for h in range(A_HEADS):
        sc = None
        for li, m in enumerate(levels):
            a_h = jnp.where(head_masks[h], a_tok[li], 0.0).astype(BF16)
            s_l = _dot_nt(a_h, b_tok[li])
            if 2 * m < nc:
                s_l = jnp.where((row_c // (2 * m)) == (col_c // (2 * m)), s_l, 0.0)
            sc = s_l if sc is None else sc + s_l
        o_h = _dot(sc.astype(BF16), v_bf)
        o_cross = o_cross + jnp.where(head_masks[h], o_h, 0.0)

    st = state_s[...]
    o_state = _dot_nt(qs_s[...].reshape(tb, w).astype(BF16), st.astype(BF16))
    kv = _dot_tn(v_bf, ks_s[...].reshape(tb, w).astype(BF16))
    rr = lax.broadcasted_iota(jnp.int32, (w, w), 0)
    cc = lax.broadcasted_iota(jnp.int32, (w, w), 1)
    state_s[...] = st * total_decay + jnp.where((rr // A_DK) == (cc // A_DK), kv, 0.0)

    o = oi_s[...].reshape(tb, w) + o_cross + o_state
    ms = _dot((o * o).astype(BF16), ones_bd) * (1.0 / A_DK)
    ag = hg_ref[:, 3 * w:4 * w]
    o_ref[:, 0:w] = (o * lax.rsqrt(ms + RMS_EPS) * ng_ref[...] * (ag * _sigmoid(ag))).astype(o_ref.dtype)

    cps = B_CHUNK // cs
    u = _gelu(sg_ref[:, 0:B_WIDTH])
    vv = _layer_norm(_gelu(sg_ref[:, B_WIDTH:2 * B_WIDTH]), lng_ref[...], lnb_ref[...])
    lane_b = lax.broadcasted_iota(jnp.int32, (B_CHUNK, B_WIDTH), 1)
    zs = []
    for ci in range(nc // cps):
        v_c = jnp.concatenate([vv[p * nc + ci * cps:p * nc + (ci + 1) * cps] for p in range(cs)],
                              axis=0).astype(BF16)
        z = bs_ref[...]
        for g in range(B_GROUPS):
            z_g = _dot(ws_ref[g], v_c)
            z = z + jnp.where((lane_b >= g * B_CH) & (lane_b < (g + 1) * B_CH), z_g, 0.0)
        zs.append(z)
    z_all = jnp.concatenate([z[p * cps:(p + 1) * cps] for p in range(cs) for z in zs], axis=0)
    o_ref[:, w:w + B_WIDTH] = (u * z_all).astype(o_ref.dtype)


def _mixers(hg, sg, sw, layer, bsz, seq):
    nb = seq // MIX_BLOCK
    nc, cs, w = MIX_NCHUNK, A_CHUNK, A_WIDTH
    nlev = int(round(math.log2(nc)))
    blk = lambda width: pl.BlockSpec((MIX_BLOCK, width), lambda b, i: (b * nb + i, 0))
    pm = lambda: pltpu.VMEM((cs, nc, w), F32)
    ow = A_WIDTH + B_WIDTH
    return pl.pallas_call(
        _mixer_kernel,
        out_shape=jax.ShapeDtypeStruct((bsz * seq, ow), BF16),
        grid=(bsz, nb),
        in_specs=[blk(4 * A_WIDTH), blk(2 * B_WIDTH),
                  ] + [_layer_spec(sw[n], layer) for n in ("lb", "hgrn_g", "sgu_g", "sgu_b", "ws", "bs")] + [
                  _const_spec((w, w))],
        out_specs=blk(ow),
        scratch_shapes=[pm(), pm(), pm(), pm(), pm(), pm(), pm(),
                        pltpu.VMEM((nlev, cs, nc, w), F32), pltpu.VMEM((nlev, cs, nc, w), F32),
                        pm(), pm(), pltpu.VMEM((w, w), F32)],
        compiler_params=_params(("parallel", "arbitrary")),
        name="mixers",
    )(hg, sg, sw["lb"], sw["hgrn_g"], sw["sgu_g"], sw["sgu_b"], sw["ws"], sw["bs"], sw["ones_bd"])


def _token_offset(i):
    r = i % MIX_BLOCK
    return (i - r) + (r % MIX_NCHUNK) * A_CHUNK + r // MIX_NCHUNK


def _attn_kernel(q_ref, k_ref, vt_ref, o_ref, sa_ref, sb_ref, acca_ref, accb_ref):
    tq, tk = ATT_TQ, ATT_TK
    qi = pl.program_id(2)
    qs = [q_ref[:, hh * HEAD_PAD:(hh + 1) * HEAD_PAD] for hh in range(ATT_HEADS)]

    def produce(hh, j, dst, mask=None, cols=slice(None)):
        off = pl.multiple_of(j * tk, tk)
        k_t = k_ref[pl.ds(off, tk), hh * HEAD_PAD:(hh + 1) * HEAD_PAD]
        st = _dot_nt(k_t, qs[hh][cols])
        if mask is not None:
            st = jnp.where(mask, st, NEG_BIG)
        dst[hh, :, cols] = st
        return jnp.max(st, axis=0, keepdims=True)

    ones_rows = jnp.ones((ATT_ONES, tk), BF16)

    def consume(hh, j, src, tile_max, acc, m_old, cols=slice(None)):
        m_new = jnp.maximum(m_old, tile_max)
        alpha = jnp.exp2(m_old - m_new)
        pt = jnp.exp2(src[hh, :, cols] - m_new).astype(BF16)
        vt_h = jnp.concatenate([vt_ref[j, hh * C_V:(hh + 1) * C_V, :], ones_rows], axis=0)
        acc[hh, :, cols] = alpha * acc[hh, :, cols] + _dot(vt_h, pt)
        return m_new

    def overlap(prod, cons):
        maxes, ms = [], []
        for hh in range(ATT_HEADS):
            if prod is not None:
                maxes.append(produce(hh, *prod))
            if cons is not None:
                j, src, tile_max, acc, m_old = cons
                ms.append(consume(hh, j, src, tile_max[hh], acc, m_old[hh]))
        return tuple(maxes), tuple(ms)

    key = _token_offset(lax.broadcasted_iota(jnp.int32, (tk, tq), 0))
    qry = _token_offset(lax.broadcasted_iota(jnp.int32, (tk, tq), 1))
    d0 = 2 * qi
    acca_ref[...] = jnp.zeros_like(acca_ref)
    accb_ref[...] = jnp.zeros_like(accb_ref)
    max_a, _ = overlap((d0, sa_ref, key <= qry), None)
    ms0 = tuple(jnp.full((1, tq), NEG_BIG, F32) for _ in range(ATT_HEADS))

    def body(p, carry):
        ms_a, ms_b, in_a, max_a = carry
        max_b, ms_a = overlap((2 * p, sb_ref), (in_a, sa_ref, max_a, acca_ref, ms_a))
        max_a, ms_b = overlap((2 * p + 1, sa_ref), (2 * p, sb_ref, max_b, accb_ref, ms_b))
        return ms_a, ms_b, 2 * p + 1, max_a

    ms_a, ms_b, in_a, max_a = lax.fori_loop(0, qi, body, (ms0, ms0, d0, max_a))
    late = slice(tq - tk, tq)
    max_b, ms_a = overlap((d0 + 1, sb_ref, (key <= qry)[:, 0:tk], late), (in_a, sa_ref, max_a, acca_ref, ms_a))
    ms_b = tuple(
        jnp.concatenate([ms_b[hh][:, 0:tq - tk],
                         consume(hh, d0 + 1, sb_ref, max_b[hh], accb_ref, ms_b[hh][:, late], late)], axis=1)
        for hh in range(ATT_HEADS))
    for hh in range(ATT_HEADS):
        m = jnp.maximum(ms_a[hh], ms_b[hh])
        acc = acca_ref[hh] * jnp.exp2(ms_a[hh] - m) + accb_ref[hh] * jnp.exp2(ms_b[hh] - m)
        o_ref[hh * C_V:(hh + 1) * C_V, :] = (acc[0:C_V] / acc[C_V:C_V + 1]).astype(o_ref.dtype)


def _attention(q, k, vt, bsz, seq):
    assert ATT_TQ == 2 * ATT_TK
    nh = ATT_HEADS
    nq = seq // ATT_TQ
    nk = seq // ATT_TK
    return pl.pallas_call(
        _attn_kernel,
        out_shape=jax.ShapeDtypeStruct((bsz * nq, C_WIDTH, ATT_TQ), BF16),
        grid=(bsz, C_HEADS // nh, nq),
        in_specs=[pl.BlockSpec((None, ATT_TQ, nh * HEAD_PAD), lambda b, g, i: (b, i, g)),
                  pl.BlockSpec((None, seq, nh * HEAD_PAD), lambda b, g, i: (b, 0, g)),
                  pl.BlockSpec((nk, nh * C_V, ATT_TK), lambda b, g, i: (b, g, 0))],
        out_specs=pl.BlockSpec((None, nh * C_V, ATT_TQ), lambda b, g, i: (b * nq + i, g, 0)),
        scratch_shapes=[pltpu.VMEM((nh, ATT_TK, ATT_TQ), F32), pltpu.VMEM((nh, ATT_TK, ATT_TQ), F32),
                        pltpu.VMEM((nh, C_V + ATT_ONES, ATT_TQ), F32), pltpu.VMEM((nh, C_V + ATT_ONES, ATT_TQ), F32)],
        compiler_params=_params(("parallel", "parallel", "arbitrary")),
        name="mla_attention",
    )(q.reshape(bsz, seq, -1), k.reshape(bsz, seq, -1), vt)


def _channel_kernel(oab_ref, oct_ref, h_ref, p_ref, wab_ref, wc_ref, g1_ref, b1_ref,
                    wgu_ref, wd_ref, wpg_ref, wpp_ref, g2_ref, b2_ref, o_ref, act_s, *,
                    token_major_out):
    mix_c = jnp.concatenate([_dot_tn(oct_ref[i], wc_ref[...]) for i in range(oct_ref.shape[0])], axis=0)
    mix = _dot(oab_ref[...], wab_ref[...]) + mix_c
    h = _layer_norm(DEEPNORM_ALPHA * h_ref[...] + mix, g1_ref[...], b1_ref[...])
    hb = h.astype(BF16)
    for c in range(D_FF // FF_CHUNK):
        cols = slice(c * FF_CHUNK, (c + 1) * FF_CHUNK)
        gate = _dot(hb, wgu_ref[:, cols])
        up = _dot(hb, wgu_ref[:, D_FF + c * FF_CHUNK:D_FF + (c + 1) * FF_CHUNK])
        act_s[:, cols] = (gate * _sigmoid(gate) * up).astype(BF16)
    ffn = _dot(act_s[...], wd_ref[...])
    ple = _sigmoid(_dot(hb, wpg_ref[...])) * _dot(_read_pos_major(p_ref).astype(BF16), wpp_ref[...])
    out = _layer_norm(DEEPNORM_ALPHA * h + ffn + ple, g2_ref[...], b2_ref[...])
    if token_major_out:
        _write_token_major(o_ref, out)
    else:
        o_ref[...] = out


def _channel(o_ab, o_ct, h, p, sw, layer, token_major_out):
    t = h.shape[0]
    tm = CH_BLOCK
    row = lambda w: pl.BlockSpec((tm, w), lambda i: (i, 0))
    if token_major_out:
        out_shape = jax.ShapeDtypeStruct((t // MIX_BLOCK, MIX_NCHUNK, A_CHUNK, D_MODEL), F32)
        out_spec = _token_block_spec(D_MODEL, tm)
    else:
        out_shape = jax.ShapeDtypeStruct((t, D_MODEL), F32)
        out_spec = row(D_MODEL)
    names = ("w_out_ab", "w_out_c", "ln1_g", "ln1_b", "wgu", "wd", "wpg", "wpp", "ln2_g", "ln2_b")
    return pl.pallas_call(
        functools.partial(_channel_kernel, token_major_out=token_major_out),
        out_shape=out_shape,
        grid=(t // tm,),
        in_specs=[row(A_WIDTH + B_WIDTH), pl.BlockSpec((tm // ATT_TQ, C_WIDTH, ATT_TQ), lambda i: (i, 0, 0)), row(D_MODEL),
                  _token_block_spec(PLE_DIM, tm, layer * (t // tm))] + [_layer_spec(sw[n], layer, single_buffer=True) for n in names],
        out_specs=out_spec,
        scratch_shapes=[pltpu.VMEM((tm, D_FF), BF16)],
        compiler_params=_params(("parallel",)),
        name="channel_mix",
    )(o_ab, o_ct, h, _token_blocks(p), *[sw[n] for n in names])


def _head_pad_cols(w_nope, w_rope):
    lead = w_nope.shape[:2]
    pad = jnp.zeros(lead + (C_HEADS, HEAD_PAD - C_NOPE - w_rope.shape[-1]), w_nope.dtype)
    return jnp.concatenate([w_nope, w_rope, pad], axis=-1).reshape(lead + (C_HEADS * HEAD_PAD,))


def _pos_major(x, axis):
    cps = B_CHUNK // A_CHUNK
    shape = x.shape
    x = x.reshape(shape[:axis] + (cps, A_CHUNK) + shape[axis + 1:])
    return jnp.swapaxes(x, axis, axis + 1).reshape(shape)


def _prep_weights(lower_bounds, w_in, hgrn_norm_g, sgu_ln_g, sgu_ln_b, sgu_w_s, sgu_b_s,
                  mla_q_norm_g, mla_w_uq, mla_kv_norm_g, mla_w_ukv, w_out, ln1_g, ln1_b,
                  w_gate_up, w_down, ple_w_gate, ple_w_proj, ln2_g, ln2_b):
    nl = w_in.shape[0]
    o_hg, o_sg = 4 * A_WIDTH, 4 * A_WIDTH + 2 * B_WIDTH
    o_kv = o_sg + C_Q_RANK
    o_kr = o_kv + C_KV_RANK
    w_kr = w_in[:, :, o_kr:o_kr + C_ROPE]
    lane_pad = lambda wr: jnp.concatenate(
        [jnp.zeros((nl, D_MODEL, C_NOPE), F32), wr, jnp.zeros((nl, D_MODEL, HEAD_PAD - C_NOPE - C_ROPE), F32)],
        axis=-1)
    wq = mla_w_uq.reshape(nl, C_Q_RANK, C_HEADS, C_NOPE + C_ROPE)
    wq_nope, wq_rope = wq[..., :C_NOPE], wq[..., C_NOPE:]
    wkv = mla_w_ukv.reshape(nl, C_KV_RANK, C_HEADS, C_NOPE + C_V)
    wk_nope, wv = wkv[..., :C_NOPE], wkv[..., C_NOPE:]
    zeros_rope = jnp.zeros((nl, C_KV_RANK, C_HEADS, C_ROPE), F32)
    tri = jnp.tril(jnp.ones((B_CHUNK, B_CHUNK), F32))
    head_id = np.arange(A_WIDTH) // A_DK
    vec = lambda g: g.reshape(nl, 1, -1)
    return dict(
        w_in=jnp.concatenate([w_in[:, :, :o_kr], lane_pad(w_kr)], axis=-1).astype(BF16),
        gq=vec(mla_q_norm_g), gkv=vec(mla_kv_norm_g),
        wq=_head_pad_cols(wq_nope, jnp.concatenate(
            [wq_rope, wq_rope[..., C_ROPE // 2:], wq_rope[..., :C_ROPE // 2]], axis=-1)).astype(BF16),
        wk=_head_pad_cols(wk_nope, zeros_rope).astype(BF16),
        wv=jnp.swapaxes(wv.reshape(nl, C_KV_RANK, C_WIDTH), 1, 2).astype(BF16),
        lb=vec(lower_bounds), hgrn_g=vec(hgrn_norm_g), sgu_g=vec(sgu_ln_g), sgu_b=vec(sgu_ln_b),
        ws=_pos_major(_pos_major(sgu_w_s * tri, 2), 3).astype(BF16),
        bs=_pos_major(jnp.repeat(jnp.swapaxes(sgu_b_s, 1, 2), B_CH, axis=2), 1),
        ones_bd=jnp.asarray(head_id[:, None] == head_id[None, :], BF16),
        w_out_ab=w_out[:, :A_WIDTH + B_WIDTH].astype(BF16), w_out_c=w_out[:, A_WIDTH + B_WIDTH:].astype(BF16),
        ln1_g=vec(ln1_g), ln1_b=vec(ln1_b),
        wgu=w_gate_up.astype(BF16),
        wd=w_down.astype(BF16), wpg=ple_w_gate.astype(BF16), wpp=ple_w_proj.astype(BF16),
        ln2_g=vec(ln2_g), ln2_b=vec(ln2_b),
    )


def kernel(x, p, positions, ln_in_g, ln_in_b, w_in, hgrn_lb_logits, hgrn_norm_g, sgu_ln_g, sgu_ln_b, sgu_w_s, sgu_b_s, mla_q_norm_g, mla_w_uq, mla_kv_norm_g, mla_w_ukv, w_out, ln1_g, ln1_b, w_gate_up, w_down, ple_w_gate, ple_w_proj, ln2_g, ln2_b):
    bsz, seq, d = x.shape
    t = bsz * seq
    lb_cum = jnp.cumsum(jax.nn.softmax(hgrn_lb_logits.astype(F32), axis=0), axis=0)
    lower_bounds = lb_cum - lb_cum[0]
    pos_pm = positions.reshape(bsz, seq // MIX_BLOCK, MIX_NCHUNK, A_CHUNK).swapaxes(2, 3).reshape(bsz, seq)
    rope_pats = _rope_tables(pos_pm)
    sw = _prep_weights(lower_bounds, w_in, hgrn_norm_g, sgu_ln_g, sgu_ln_b, sgu_w_s, sgu_b_s,
                       mla_q_norm_g, mla_w_uq, mla_kv_norm_g, mla_w_ukv, w_out, ln1_g, ln1_b,
                       w_gate_up, w_down, ple_w_gate, ple_w_proj, ln2_g, ln2_b)

    h = x.reshape(t, d)
    for i in range(DEPTH):
        if i == 0:
            hg, sg, q, k, v, h = _in_proj(h, rope_pats, sw, i, ln=(ln_in_g, ln_in_b))
        else:
            hg, sg, q, k, v = _in_proj(h, rope_pats, sw, i)
        o_ab = _mixers(hg, sg, sw, i, bsz, seq)
        o_c = _attention(q, k, v, bsz, seq)
        h = _channel(o_ab, o_c, h, p.reshape(DEPTH * t, PLE_DIM), sw, i, token_major_out=(i == DEPTH - 1))
    return h.reshape(bsz, seq, d)
```

```python
import functools
import math

import numpy as np
import jax
import jax.numpy as jnp
from jax import lax
from jax.experimental import pallas as pl
from jax.experimental.pallas import tpu as pltpu

F32 = jnp.float32
BF16 = jnp.bfloat16

D_MODEL = 1024
DEPTH = 2
PLE_DIM = 256

A_WIDTH = 256
A_DK = 64
A_HEADS = 4
A_CHUNK = 16
B_WIDTH = 256
B_CH = 64
B_GROUPS = 4
B_CHUNK = 128
C_WIDTH = 512
C_NOPE = 64
C_ROPE = 32
C_V = 64
C_HEADS = 8
C_Q_RANK = 384
C_KV_RANK = 256
ROPE_THETA = 10000.0
HEAD_PAD = 128

D_FF = 2816
LN_EPS = 1e-5
RMS_EPS = 1e-6
DEEPNORM_ALPHA = (2 * DEPTH) ** 0.25
ATT_SCALE = (C_NOPE + C_ROPE) ** -0.5
Q_SCALE = ATT_SCALE * math.log2(math.e)

MIX_BLOCK = 256
MIX_NCHUNK = MIX_BLOCK // A_CHUNK
ATT_TQ = 512
ATT_TK = 256
ATT_HEADS = 8
ATT_ONES = 16
ROW_BLOCK = 512
CH_BLOCK = 512
FF_CHUNK = 256
VMEM_LIMIT = 56 * 1024 * 1024
NEG_BIG = -1e30


def _dot(a, b):
    return jnp.dot(a, b, preferred_element_type=F32)


def _dot_nt(a, b):
    return lax.dot_general(a, b, (((1,), (1,)), ((), ())), preferred_element_type=F32)


def _dot_tn(a, b):
    return lax.dot_general(a, b, (((0,), (0,)), ((), ())), preferred_element_type=F32)


def _layer_norm(x, g, b):
    mu = jnp.mean(x, axis=-1, keepdims=True)
    xc = x - mu
    var = jnp.mean(xc * xc, axis=-1, keepdims=True)
    return xc * lax.rsqrt(var + LN_EPS) * g + b


def _rms_norm(x, g):
    return x * lax.rsqrt(jnp.mean(x * x, axis=-1, keepdims=True) + RMS_EPS) * g


def _sigmoid(x):
    return 1.0 / (1.0 + jnp.exp(-x))


def _const_spec(shape):
    nd = len(shape)
    return pl.BlockSpec(shape, lambda *_: (0,) * nd)


def _layer_spec(arr, layer, single_buffer=False):
    shape = arr.shape[1:]
    nd = len(shape)
    kw = dict(pipeline_mode=pl.Buffered(1)) if single_buffer else {}
    return pl.BlockSpec((None,) + shape, lambda *_: (layer,) + (0,) * nd, **kw)


def _params(sem):
    return pltpu.CompilerParams(dimension_semantics=sem, vmem_limit_bytes=VMEM_LIMIT)


def _rope_table_kernel(pos_ref, freq_ref, cos_ref, sin1_ref, sin2_ref, c_s, s_s):
    half = C_ROPE // 2
    g = pl.program_id(0)

    @pl.when(g == 0)
    def _():
        ang = pos_ref[...] * freq_ref[...]
        c_s[...] = jnp.cos(ang)
        s_s[...] = jnp.sin(ang)

    lane = lax.broadcasted_iota(jnp.int32, c_s.shape, 1)
    on_x1 = (lane >= C_NOPE) & (lane < C_NOPE + half)
    on_x2 = (lane >= C_NOPE + half) & (lane < C_NOPE + C_ROPE)
    to_x1 = (C_NOPE - g * half) % 128
    to_x2 = (C_NOPE + half - g * half) % 128
    c, s = c_s[...], s_s[...]
    nope = jnp.where(lane < C_NOPE, 1.0, 0.0)
    cos_ref[...] = jnp.where(on_x1, pltpu.roll(c, to_x1, axis=1), jnp.where(on_x2, pltpu.roll(c, to_x2, axis=1), nope))
    sin1_ref[...] = jnp.where(on_x1, -pltpu.roll(s, to_x1, axis=1), 0.0)
    sin2_ref[...] = jnp.where(on_x2, pltpu.roll(s, to_x2, axis=1), 0.0)


def _rope_tables(positions):
    t = positions.size
    half = C_ROPE // 2
    groups = 128 // half
    rows = t // groups
    inv_freq = ROPE_THETA ** (-jnp.arange(0, C_ROPE, 2, dtype=F32) / C_ROPE)
    pos_rep = jnp.repeat(positions.astype(F32).reshape(groups, rows).T, half, axis=1)
    freq = jnp.tile(inv_freq, groups).reshape(1, 128)
    table = jax.ShapeDtypeStruct((t, HEAD_PAD), F32)
    out_spec = pl.BlockSpec((rows, HEAD_PAD), lambda g: (g, 0))
    return pl.pallas_call(
        _rope_table_kernel,
        out_shape=(table,) * 3,
        grid=(groups,),
        in_specs=[_const_spec((rows, 128)), _const_spec((1, 128))],
        out_specs=(out_spec,) * 3,
        scratch_shapes=[pltpu.VMEM((rows, 128), F32), pltpu.VMEM((rows, 128), F32)],
        compiler_params=_params(("arbitrary",)),
        name="rope_tables",
    )(pos_rep, freq)


def _read_pos_major(ref):
    return jnp.concatenate([ref[blk, :, p, :] for blk in range(ref.shape[0]) for p in range(A_CHUNK)], axis=0)


def _write_token_major(ref, val):
    for blk in range(ref.shape[0]):
        for p in range(A_CHUNK):
            r0 = blk * MIX_BLOCK + p * MIX_NCHUNK
            ref[blk, :, p, :] = val[r0:r0 + MIX_NCHUNK]


def _token_blocks(x2d):
    t, wd = x2d.shape
    return x2d.reshape(t // MIX_BLOCK, MIX_NCHUNK, A_CHUNK, wd)


def _token_block_spec(wd, rows=ROW_BLOCK, first_block=0):
    nblk = rows // MIX_BLOCK
    return pl.BlockSpec((nblk, MIX_NCHUNK, A_CHUNK, wd), lambda i: (first_block + i, 0, 0, 0))


def _rope_lanes(x, cos, sin_x1, sin_x2):
    half = C_ROPE // 2
    right = pltpu.roll(x, HEAD_PAD - half, axis=1)
    left = pltpu.roll(x, half, axis=1)
    return x * cos + right * sin_x1 + left * sin_x2


def _inproj_ln_kernel(x_ref, lng_ref, lnb_ref, *rest):
    *rest, h_out = rest
    h = _layer_norm(_read_pos_major(x_ref), lng_ref[...], lnb_ref[...])
    h_out[...] = h
    _inproj_body(h, *rest)


def _inproj_kernel(h_ref, *rest):
    _inproj_body(h_ref[...], *rest)


def _inproj_body(h, cos_ref, sin1_ref, sin2_ref, w_in,
                 gq_ref, wq_ref, gkv_ref, wk_ref, wv_ref,
                 hg_out, sg_out, q_out, k_out, v_out):
    hb = h.astype(BF16)
    o_sg, o_lat = 4 * A_WIDTH, 4 * A_WIDTH + 2 * B_WIDTH
    proj = _dot(hb, w_in[...])
    hg_out[...] = proj[:, 0:o_sg]
    sg_out[...] = proj[:, o_sg:o_lat]
    lat = proj[:, o_lat:]
    cos, sin1, sin2 = cos_ref[...], sin1_ref[...], sin2_ref[...]

    q_table = (cos + pltpu.roll(sin1 + sin2, C_ROPE, axis=1)) * Q_SCALE
    cqn = _rms_norm(lat[:, 0:C_Q_RANK], gq_ref[...]).astype(BF16)
    q_lin = _dot(cqn, wq_ref[...])
    for h in range(C_HEADS):
        sl = slice(h * HEAD_PAD, (h + 1) * HEAD_PAD)
        q_out[:, sl] = (q_lin[:, sl] * q_table).astype(BF16)

    ckvn = _rms_norm(lat[:, C_Q_RANK:C_Q_RANK + C_KV_RANK], gkv_ref[...]).astype(BF16)
    k_lin = _dot(ckvn, wk_ref[...])
    vt = _dot_nt(wv_ref[...], ckvn).astype(BF16)
    for c in range(v_out.shape[0]):
        v_out[c] = vt[:, c * ATT_TK:(c + 1) * ATT_TK]
    k_rope = _rope_lanes(lat[:, C_Q_RANK + C_KV_RANK:], cos, sin1, sin2)
    k_rope = k_rope + pltpu.roll(k_rope, C_ROPE, axis=1)
    for h in range(C_HEADS):
        sl = slice(h * HEAD_PAD, (h + 1) * HEAD_PAD)
        k_out[:, sl] = (k_lin[:, sl] + k_rope).astype(BF16)


def _in_proj(h, rope_pats, sw, layer, ln=None):
    t = h.shape[0]
    tm = ROW_BLOCK
    row = lambda w: pl.BlockSpec((tm, w), lambda i: (i, 0))
    weights = [sw[n] for n in ("w_in", "gq", "wq", "gkv", "wk", "wv")]
    hp = C_HEADS * HEAD_PAD
    out_shape = [jax.ShapeDtypeStruct((t, 4 * A_WIDTH), F32),
                 jax.ShapeDtypeStruct((t, 2 * B_WIDTH), F32),
                 jax.ShapeDtypeStruct((t, hp), BF16),
                 jax.ShapeDtypeStruct((t, hp), BF16),
                 jax.ShapeDtypeStruct((t // ATT_TK, C_WIDTH, ATT_TK), BF16)]
    out_specs = [row(4 * A_WIDTH), row(2 * B_WIDTH), row(hp), row(hp),
                 pl.BlockSpec((tm // ATT_TK, C_WIDTH, ATT_TK), lambda i: (i, 0, 0))]
    common_specs = [row(HEAD_PAD)] * 3 + [_layer_spec(w, layer) for w in weights]
    if ln is None:
        body, first_specs, first_args = _inproj_kernel, [row(D_MODEL)], [h]
    else:
        body = _inproj_ln_kernel
        first_specs = [_token_block_spec(D_MODEL), _const_spec((1, D_MODEL)), _const_spec((1, D_MODEL))]
        first_args = [_token_blocks(h), ln[0].reshape(1, D_MODEL), ln[1].reshape(1, D_MODEL)]
        out_shape.append(jax.ShapeDtypeStruct((t, D_MODEL), F32))
        out_specs.append(row(D_MODEL))
    return pl.pallas_call(
        body,
        out_shape=tuple(out_shape),
        grid=(t // tm,),
        in_specs=first_specs + common_specs,
        out_specs=tuple(out_specs),
        compiler_params=_params(("parallel",)),
        name="in_proj",
    )(*first_args, *rope_pats, *weights)


def _erf(x):
    return lax.erf(x)


def _gelu(x):
    return 0.5 * x * (1.0 + _erf(x * (2.0 ** -0.5)))


def _rows_to_array(rows, like):
    zero = jnp.zeros_like(like)
    return jnp.concatenate([zero if r is None else r for r in rows], axis=0)


def _mixer_kernel(hg_ref, sg_ref, lb_ref, ng_ref, lng_ref, lnb_ref, ws_ref, bs_ref, ones_ref,
                  o_ref,
                  q_s, k_s, f_s, v_s, qt_s, kt_s, oi_s, a_s, b_s, qs_s, ks_s, state_s):
    nc, cs, w = MIX_NCHUNK, A_CHUNK, A_WIDTH
    tb = nc * cs
    slab = lambda p: slice(p * nc, (p + 1) * nc)

    @pl.when(pl.program_id(1) == 0)
    def _():
        state_s[...] = jnp.zeros_like(state_s)

    lb = lb_ref[...]
    for p in range(cs):
        aq = hg_ref[slab(p), 0:w]
        f_p = lb + (1.0 - lb) * _sigmoid(hg_ref[slab(p), w:2 * w])
        q_s[p] = aq * _sigmoid(aq)
        f_s[p] = f_p
        k_s[p] = 1.0 - f_p
        v_s[p] = hg_ref[slab(p), 2 * w:3 * w]

    ones_bd = ones_ref[...]

    kd = []
    for p in range(cs):
        f_p = f_s[p]
        q_p = q_s[p]
        kd = [k_s[p]] + [f_p * x for x in kd]
        xs = jnp.concatenate([q_p * x for x in kd], axis=0).astype(BF16)
        wgt = _dot(xs, ones_bd)
        o_p = wgt[0:nc] * v_s[p]
        for d in range(1, p + 1):
            o_p = o_p + wgt[d * nc:(d + 1) * nc] * v_s[p - d]
        oi_s[p] = o_p

    pref = f_s[0]
    qt_s[0] = q_s[0] * pref
    for p in range(1, cs):
        pref = pref * f_s[p]
        qt_s[p] = q_s[p] * pref
    chunk_decay = pref
    suf = jnp.ones_like(pref)
    kt_s[cs - 1] = k_s[cs - 1]
    for p in range(cs - 2, -1, -1):
        suf = suf * f_s[p + 1]
        kt_s[p] = k_s[p] * suf

    drow = [chunk_decay[c:c + 1, :] for c in range(nc)]
    one_row = jnp.ones_like(drow[0])
    levels = []
    m = 1
    while m < nc:
        levels.append(m)
        m *= 2
    la_arrays, lb_arrays = [], []
    for m in levels:
        la = [None] * nc
        lbr = [None] * nc
        for base in range(0, nc, 2 * m):
            mid = base + m
            la[mid] = one_row
            for i in range(mid + 1, mid + m):
                la[i] = la[i - 1] * drow[i - 1]
            lbr[mid - 1] = one_row
            for j in range(mid - 2, base - 1, -1):
                lbr[j] = lbr[j + 1] * drow[j + 1]
        la_arrays.append(_rows_to_array(la, one_row))
        lb_arrays.append(_rows_to_array(lbr, one_row))
    ep = [one_row]
    for c in range(1, nc):
        ep.append(ep[-1] * drow[c - 1])
    es = [one_row] * nc
    for c in range(nc - 2, -1, -1):
        es[c] = es[c + 1] * drow[c + 1]
    total_decay = ep[-1] * drow[nc - 1]
    ep_arr = jnp.concatenate(ep, axis=0)
    es_arr = jnp.concatenate(es, axis=0)

    for p in range(cs):
        qt_p = qt_s[p]
        kt_p = kt_s[p]
        for li in range(len(levels)):
            a_s[li, p] = qt_p * la_arrays[li]
            b_s[li, p] = kt_p * lb_arrays[li]
        qs_s[p] = qt_p * ep_arr
        ks_s[p] = kt_p * es_arr

    lane = lax.broadcasted_iota(jnp.int32, (tb, w), 1)
    head_masks = [(lane >= h * A_DK) & (lane < (h + 1) * A_DK) for h in range(A_HEADS)]
    row_c = lax.broadcasted_iota(jnp.int32, (tb, tb), 0) % nc
    col_c = lax.broadcasted_iota(jnp.int32, (tb, tb), 1) % nc
    v_tok = v_s[...].reshape(tb, w)
    v_bf = v_tok.astype(BF16)
    a_tok = [a_s[li].reshape(tb, w).astype(BF16) for li in range(len(levels))]
    b_tok = [b_s[li].reshape(tb, w).astype(BF16) for li in range(len(levels))]
    scores, v_heads = [], []
    for h in range(A_HEADS):
        hs = slice(h * A_DK, (h + 1) * A_DK)
        sc = None
        for li, m in enumerate(levels):
            s_l = _dot_nt(a_tok[li][:, hs], b_tok[li][:, hs])
            if 2 * m < nc:
                s_l = jnp.where((row_c // (2 * m)) == (col_c // (2 * m)), s_l, 0.0)
            sc = s_l if sc is None else sc + s_l
        scores.append(sc.astype(BF16))
        v_heads.append(jnp.where(head_masks[h], v_tok, 0.0).astype(BF16))
    o_cross = _dot(jnp.concatenate(scores, axis=1), jnp.concatenate(v_heads, axis=0))

    st = state_s[...]
    o_state = _dot_nt(qs_s[...].reshape(tb, w).astype(BF16), st.astype(BF16))
    kv = _dot_tn(v_bf, ks_s[...].reshape(tb, w).astype(BF16))
    rr = lax.broadcasted_iota(jnp.int32, (w, w), 0)
    cc = lax.broadcasted_iota(jnp.int32, (w, w), 1)
    state_s[...] = st * total_decay + jnp.where((rr // A_DK) == (cc // A_DK), kv, 0.0)

    o = oi_s[...].reshape(tb, w) + o_cross + o_state
    ms = _dot((o * o).astype(BF16), ones_bd) * (1.0 / A_DK)
    ag = hg_ref[:, 3 * w:4 * w]
    o_ref[:, 0:w] = (o * lax.rsqrt(ms + RMS_EPS) * ng_ref[...] * (ag * _sigmoid(ag))).astype(o_ref.dtype)

    cps = B_CHUNK // cs
    u = _gelu(sg_ref[:, 0:B_WIDTH])
    vv = _layer_norm(_gelu(sg_ref[:, B_WIDTH:2 * B_WIDTH]), lng_ref[...], lnb_ref[...])
    lane_b = lax.broadcasted_iota(jnp.int32, (B_CHUNK, B_WIDTH), 1)
    zs = []
    for ci in range(nc // cps):
        v_c = jnp.concatenate([vv[p * nc + ci * cps:p * nc + (ci + 1) * cps] for p in range(cs)],
                              axis=0).astype(BF16)
        z = bs_ref[...]
        for g in range(B_GROUPS):
            z_g = _dot(ws_ref[g], v_c)
            z = z + jnp.where((lane_b >= g * B_CH) & (lane_b < (g + 1) * B_CH), z_g, 0.0)
        zs.append(z)
    z_all = jnp.concatenate([z[p * cps:(p + 1) * cps] for p in range(cs) for z in zs], axis=0)
    o_ref[:, w:w + B_WIDTH] = (u * z_all).astype(o_ref.dtype)


def _mixers(hg, sg, sw, layer, bsz, seq):
    nb = seq // MIX_BLOCK
    nc, cs, w = MIX_NCHUNK, A_CHUNK, A_WIDTH
    nlev = int(round(math.log2(nc)))
    blk = lambda width: pl.BlockSpec((MIX_BLOCK, width), lambda b, i: (b * nb + i, 0))
    pm = lambda: pltpu.VMEM((cs, nc, w), F32)
    ow = A_WIDTH + B_WIDTH
    return pl.pallas_call(
        _mixer_kernel,
        out_shape=jax.ShapeDtypeStruct((bsz * seq, ow), BF16),
        grid=(bsz, nb),
        in_specs=[blk(4 * A_WIDTH), blk(2 * B_WIDTH),
                  ] + [_layer_spec(sw[n], layer) for n in ("lb", "hgrn_g", "sgu_g", "sgu_b", "ws", "bs")] + [
                  _const_spec((w, w))],
        out_specs=blk(ow),
        scratch_shapes=[pm(), pm(), pm(), pm(), pm(), pm(), pm(),
                        pltpu.VMEM((nlev, cs, nc, w), F32), pltpu.VMEM((nlev, cs, nc, w), F32),
                        pm(), pm(), pltpu.VMEM((w, w), F32)],
        compiler_params=_params(("parallel", "arbitrary")),
        name="mixers",
    )(hg, sg, sw["lb"], sw["hgrn_g"], sw["sgu_g"], sw["sgu_b"], sw["ws"], sw["bs"], sw["ones_bd"])


def _token_offset(i):
    r = i % MIX_BLOCK
    return (i - r) + (r % MIX_NCHUNK) * A_CHUNK + r // MIX_NCHUNK


def _attn_kernel(q_ref, k_ref, vt_ref, o_ref, sa_ref, sb_ref, acca_ref, accb_ref):
    tq, tk = ATT_TQ, ATT_TK
    qi = pl.program_id(2)
    qs = [q_ref[:, hh * HEAD_PAD:(hh + 1) * HEAD_PAD] for hh in range(ATT_HEADS)]

    def produce(hh, j, dst, mask=None, cols=slice(None)):
        off = pl.multiple_of(j * tk, tk)
        k_t = k_ref[pl.ds(off, tk), hh * HEAD_PAD:(hh + 1) * HEAD_PAD]
        st = _dot_nt(k_t, qs[hh][cols])
        if mask is not None:
            st = jnp.where(mask, st, NEG_BIG)
        dst[hh, :, cols] = st
        return jnp.max(st, axis=0, keepdims=True)

    ones_rows = jnp.ones((ATT_ONES, tk), BF16)

    def consume(hh, j, src, tile_max, acc, m_old, cols=slice(None)):
        m_new = jnp.maximum(m_old, tile_max)
        alpha = jnp.exp2(m_old - m_new)
        pt = jnp.exp2(src[hh, :, cols] - m_new).astype(BF16)
        vt_h = jnp.concatenate([vt_ref[j, hh * C_V:(hh + 1) * C_V, :], ones_rows], axis=0)
        acc[hh, :, cols] = alpha * acc[hh, :, cols] + _dot(vt_h, pt)
        return m_new

    def overlap(prod, cons):
        maxes, ms = [], []
        for hh in range(ATT_HEADS):
            if prod is not None:
                maxes.append(produce(hh, *prod))
            if cons is not None:
                j, src, tile_max, acc, m_old = cons
                ms.append(consume(hh, j, src, tile_max[hh], acc, m_old[hh]))
        return tuple(maxes), tuple(ms)

    key = _token_offset(lax.broadcasted_iota(jnp.int32, (tk, tq), 0))
    qry = _token_offset(lax.broadcasted_iota(jnp.int32, (tk, tq), 1))
    d0 = 2 * qi
    acca_ref[...] = jnp.zeros_like(acca_ref)
    accb_ref[...] = jnp.zeros_like(accb_ref)
    max_a, _ = overlap((d0, sa_ref, key <= qry), None)
    ms0 = tuple(jnp.full((1, tq), NEG_BIG, F32) for _ in range(ATT_HEADS))

    def body(p, carry):
        ms_a, ms_b, in_a, max_a = carry
        max_b, ms_a = overlap((2 * p, sb_ref), (in_a, sa_ref, max_a, acca_ref, ms_a))
        max_a, ms_b = overlap((2 * p + 1, sa_ref), (2 * p, sb_ref, max_b, accb_ref, ms_b))
        return ms_a, ms_b, 2 * p + 1, max_a

    ms_a, ms_b, in_a, max_a = lax.fori_loop(0, qi, body, (ms0, ms0, d0, max_a))
    late = slice(tq - tk, tq)
    max_b, ms_a = overlap((d0 + 1, sb_ref, (key <= qry)[:, 0:tk], late), (in_a, sa_ref, max_a, acca_ref, ms_a))
    ms_b = tuple(
        jnp.concatenate([ms_b[hh][:, 0:tq - tk],
                         consume(hh, d0 + 1, sb_ref, max_b[hh], accb_ref, ms_b[hh][:, late], late)], axis=1)
        for hh in range(ATT_HEADS))
    for hh in range(ATT_HEADS):
        m = jnp.maximum(ms_a[hh], ms_b[hh])
        acc = acca_ref[hh] * jnp.exp2(ms_a[hh] - m) + accb_ref[hh] * jnp.exp2(ms_b[hh] - m)
        o_ref[hh * C_V:(hh + 1) * C_V, :] = (acc[0:C_V] / acc[C_V:C_V + 1]).astype(o_ref.dtype)


def _attention(q, k, vt, bsz, seq):
    assert ATT_TQ == 2 * ATT_TK
    nh = ATT_HEADS
    nq = seq // ATT_TQ
    nk = seq // ATT_TK
    return pl.pallas_call(
        _attn_kernel,
        out_shape=jax.ShapeDtypeStruct((bsz * nq, C_WIDTH, ATT_TQ), BF16),
        grid=(bsz, C_HEADS // nh, nq),
        in_specs=[pl.BlockSpec((None, ATT_TQ, nh * HEAD_PAD), lambda b, g, i: (b, i, g)),
                  pl.BlockSpec((None, seq, nh * HEAD_PAD), lambda b, g, i: (b, 0, g)),
                  pl.BlockSpec((nk, nh * C_V, ATT_TK), lambda b, g, i: (b, g, 0))],
        out_specs=pl.BlockSpec((None, nh * C_V, ATT_TQ), lambda b, g, i: (b * nq + i, g, 0)),
        scratch_shapes=[pltpu.VMEM((nh, ATT_TK, ATT_TQ), F32), pltpu.VMEM((nh, ATT_TK, ATT_TQ), F32),
                        pltpu.VMEM((nh, C_V + ATT_ONES, ATT_TQ), F32), pltpu.VMEM((nh, C_V + ATT_ONES, ATT_TQ), F32)],
        compiler_params=_params(("parallel", "parallel", "arbitrary")),
        name="mla_attention",
    )(q.reshape(bsz, seq, -1), k.reshape(bsz, seq, -1), vt)


def _channel_kernel(oab_ref, oct_ref, h_ref, p_ref, wab_ref, wc_ref, g1_ref, b1_ref,
                    wgu_ref, wd_ref, wpg_ref, wpp_ref, g2_ref, b2_ref, o_ref, act_s, *,
                    token_major_out):
    mix_c = jnp.concatenate([_dot_tn(oct_ref[i], wc_ref[...]) for i in range(oct_ref.shape[0])], axis=0)
    mix = _dot(oab_ref[...], wab_ref[...]) + mix_c
    h = _layer_norm(DEEPNORM_ALPHA * h_ref[...] + mix, g1_ref[...], b1_ref[...])
    hb = h.astype(BF16)
    for c in range(D_FF // FF_CHUNK):
        cols = slice(c * FF_CHUNK, (c + 1) * FF_CHUNK)
        gate = _dot(hb, wgu_ref[:, cols])
        up = _dot(hb, wgu_ref[:, D_FF + c * FF_CHUNK:D_FF + (c + 1) * FF_CHUNK])
        act_s[:, cols] = (gate * _sigmoid(gate) * up).astype(BF16)
    ffn = _dot(act_s[...], wd_ref[...])
    ple = _sigmoid(_dot(hb, wpg_ref[...])) * _dot(_read_pos_major(p_ref).astype(BF16), wpp_ref[...])
    out = _layer_norm(DEEPNORM_ALPHA * h + ffn + ple, g2_ref[...], b2_ref[...])
    if token_major_out:
        _write_token_major(o_ref, out)
    else:
        o_ref[...] = out


def _channel(o_ab, o_ct, h, p, sw, layer, token_major_out):
    t = h.shape[0]
    tm = CH_BLOCK
    row = lambda w: pl.BlockSpec((tm, w), lambda i: (i, 0))
    if token_major_out:
        out_shape = jax.ShapeDtypeStruct((t // MIX_BLOCK, MIX_NCHUNK, A_CHUNK, D_MODEL), F32)
        out_spec = _token_block_spec(D_MODEL, tm)
    else:
        out_shape = jax.ShapeDtypeStruct((t, D_MODEL), F32)
        out_spec = row(D_MODEL)
    names = ("w_out_ab", "w_out_c", "ln1_g", "ln1_b", "wgu", "wd", "wpg", "wpp", "ln2_g", "ln2_b")
    return pl.pallas_call(
        functools.partial(_channel_kernel, token_major_out=token_major_out),
        out_shape=out_shape,
        grid=(t // tm,),
        in_specs=[row(A_WIDTH + B_WIDTH), pl.BlockSpec((tm // ATT_TQ, C_WIDTH, ATT_TQ), lambda i: (i, 0, 0)), row(D_MODEL),
                  _token_block_spec(PLE_DIM, tm, layer * (t // tm))] + [_layer_spec(sw[n], layer, single_buffer=True) for n in names],
        out_specs=out_spec,
        scratch_shapes=[pltpu.VMEM((tm, D_FF), BF16)],
        compiler_params=_params(("parallel",)),
        name="channel_mix",
    )(o_ab, o_ct, h, _token_blocks(p), *[sw[n] for n in names])


def _head_pad_cols(w_nope, w_rope):
    lead = w_nope.shape[:2]
    pad = jnp.zeros(lead + (C_HEADS, HEAD_PAD - C_NOPE - w_rope.shape[-1]), w_nope.dtype)
    return jnp.concatenate([w_nope, w_rope, pad], axis=-1).reshape(lead + (C_HEADS * HEAD_PAD,))


def _pos_major(x, axis):
    cps = B_CHUNK // A_CHUNK
    shape = x.shape
    x = x.reshape(shape[:axis] + (cps, A_CHUNK) + shape[axis + 1:])
    return jnp.swapaxes(x, axis, axis + 1).reshape(shape)


def _prep_weights(lower_bounds, w_in, hgrn_norm_g, sgu_ln_g, sgu_ln_b, sgu_w_s, sgu_b_s,
                  mla_q_norm_g, mla_w_uq, mla_kv_norm_g, mla_w_ukv, w_out, ln1_g, ln1_b,
                  w_gate_up, w_down, ple_w_gate, ple_w_proj, ln2_g, ln2_b):
    nl = w_in.shape[0]
    o_kr = 4 * A_WIDTH + 2 * B_WIDTH + C_Q_RANK + C_KV_RANK
    w_in_bf = w_in.astype(BF16)
    w_in_placed = jnp.concatenate(
        [w_in_bf[:, :, :o_kr], jnp.zeros((nl, D_MODEL, C_NOPE), BF16), w_in_bf[:, :, o_kr:],
         jnp.zeros((nl, D_MODEL, HEAD_PAD - C_NOPE - C_ROPE), BF16)], axis=-1)
    wq = mla_w_uq.reshape(nl, C_Q_RANK, C_HEADS, C_NOPE + C_ROPE)
    wq_nope, wq_rope = wq[..., :C_NOPE], wq[..., C_NOPE:]
    wkv = mla_w_ukv.reshape(nl, C_KV_RANK, C_HEADS, C_NOPE + C_V)
    wk_nope, wv = wkv[..., :C_NOPE], wkv[..., C_NOPE:]
    zeros_rope = jnp.zeros((nl, C_KV_RANK, C_HEADS, C_ROPE), F32)
    tri = jnp.tril(jnp.ones((B_CHUNK, B_CHUNK), F32))
    head_id = np.arange(A_WIDTH) // A_DK
    vec = lambda g: g.reshape(nl, 1, -1)
    return dict(
        w_in=w_in_placed,
        gq=vec(mla_q_norm_g), gkv=vec(mla_kv_norm_g),
        wq=_head_pad_cols(wq_nope, jnp.concatenate(
            [wq_rope, wq_rope[..., C_ROPE // 2:], wq_rope[..., :C_ROPE // 2]], axis=-1)).astype(BF16),
        wk=_head_pad_cols(wk_nope, zeros_rope).astype(BF16),
        wv=jnp.swapaxes(wv.reshape(nl, C_KV_RANK, C_WIDTH), 1, 2).astype(BF16),
        lb=vec(lower_bounds), hgrn_g=vec(hgrn_norm_g), sgu_g=vec(sgu_ln_g), sgu_b=vec(sgu_ln_b),
        ws=_pos_major(_pos_major(sgu_w_s * tri, 2), 3).astype(BF16),
        bs=_pos_major(jnp.repeat(jnp.swapaxes(sgu_b_s, 1, 2), B_CH, axis=2), 1),
        ones_bd=jnp.asarray(head_id[:, None] == head_id[None, :], BF16),
        w_out_ab=w_out[:, :A_WIDTH + B_WIDTH].astype(BF16), w_out_c=w_out[:, A_WIDTH + B_WIDTH:].astype(BF16),
        ln1_g=vec(ln1_g), ln1_b=vec(ln1_b),
        wgu=w_gate_up.astype(BF16),
        wd=w_down.astype(BF16), wpg=ple_w_gate.astype(BF16), wpp=ple_w_proj.astype(BF16),
        ln2_g=vec(ln2_g), ln2_b=vec(ln2_b),
    )


def kernel(x, p, positions, ln_in_g, ln_in_b, w_in, hgrn_lb_logits, hgrn_norm_g, sgu_ln_g, sgu_ln_b, sgu_w_s, sgu_b_s, mla_q_norm_g, mla_w_uq, mla_kv_norm_g, mla_w_ukv, w_out, ln1_g, ln1_b, w_gate_up, w_down, ple_w_gate, ple_w_proj, ln2_g, ln2_b):
    bsz, seq, d = x.shape
    t = bsz * seq
    lb_cum = jnp.cumsum(jax.nn.softmax(hgrn_lb_logits.astype(F32), axis=0), axis=0)
    lower_bounds = lb_cum - lb_cum[0]
    pos_pm = positions.reshape(bsz, seq // MIX_BLOCK, MIX_NCHUNK, A_CHUNK).swapaxes(2, 3).reshape(bsz, seq)
    rope_pats = _rope_tables(pos_pm)
    sw = _prep_weights(lower_bounds, w_in, hgrn_norm_g, sgu_ln_g, sgu_ln_b, sgu_w_s, sgu_b_s,
                       mla_q_norm_g, mla_w_uq, mla_kv_norm_g, mla_w_ukv, w_out, ln1_g, ln1_b,
                       w_gate_up, w_down, ple_w_gate, ple_w_proj, ln2_g, ln2_b)

    h = x.reshape(t, d)
    for i in range(DEPTH):
        if i == 0:
            hg, sg, q, k, v, h = _in_proj(h, rope_pats, sw, i, ln=(ln_in_g, ln_in_b))
        else:
            hg, sg, q, k, v = _in_proj(h, rope_pats, sw, i)
        o_ab = _mixers(hg, sg, sw, i, bsz, seq)
        o_c = _attention(q, k, v, bsz, seq)
        h = _channel(o_ab, o_c, h, p.reshape(DEPTH * t, PLE_DIM), sw, i, token_major_out=(i == DEPTH - 1))
    return h.reshape(bsz, seq, d)
```

```python
import functools
import math

import numpy as np
import jax
import jax.numpy as jnp
from jax import lax
from jax.experimental import pallas as pl
from jax.experimental.pallas import tpu as pltpu

F32 = jnp.float32
BF16 = jnp.bfloat16

D_MODEL = 1024
DEPTH = 2
PLE_DIM = 256

A_WIDTH = 256
A_DK = 64
A_HEADS = 4
A_CHUNK = 16
B_WIDTH = 256
B_CH = 64
B_GROUPS = 4
B_CHUNK = 128
C_WIDTH = 512
C_NOPE = 64
C_ROPE = 32
C_V = 64
C_HEADS = 8
C_Q_RANK = 384
C_KV_RANK = 256
ROPE_THETA = 10000.0
HEAD_PAD = 128

D_FF = 2816
LN_EPS = 1e-5
RMS_EPS = 1e-6
DEEPNORM_ALPHA = (2 * DEPTH) ** 0.25
ATT_SCALE = (C_NOPE + C_ROPE) ** -0.5
Q_SCALE = ATT_SCALE * math.log2(math.e)

MIX_BLOCK = 256
MIX_NCHUNK = MIX_BLOCK // A_CHUNK
ATT_TQ = 512
ATT_TK = 256
ATT_HEADS = 8
ATT_ONES = 16
ROW_BLOCK = 512
CH_BLOCK = 512
FF_CHUNK = 256
VMEM_LIMIT = 56 * 1024 * 1024
NEG_BIG = -1e30


def _dot(a, b):
    return jnp.dot(a, b, preferred_element_type=F32)


def _dot_nt(a, b):
    return lax.dot_general(a, b, (((1,), (1,)), ((), ())), preferred_element_type=F32)


def _dot_tn(a, b):
    return lax.dot_general(a, b, (((0,), (0,)), ((), ())), preferred_element_type=F32)


def _layer_norm(x, g, b):
    mu = jnp.mean(x, axis=-1, keepdims=True)
    xc = x - mu
    var = jnp.mean(xc * xc, axis=-1, keepdims=True)
    return xc * lax.rsqrt(var + LN_EPS) * g + b


def _rms_norm(x, g):
    return x * lax.rsqrt(jnp.mean(x * x, axis=-1, keepdims=True) + RMS_EPS) * g


def _sigmoid(x):
    return 1.0 / (1.0 + jnp.exp(-x))


def _const_spec(shape):
    nd = len(shape)
    return pl.BlockSpec(shape, lambda *_: (0,) * nd)


def _layer_spec(arr, layer, single_buffer=False):
    shape = arr.shape[1:]
    nd = len(shape)
    kw = dict(pipeline_mode=pl.Buffered(1)) if single_buffer else {}
    return pl.BlockSpec((None,) + shape, lambda *_: (layer,) + (0,) * nd, **kw)


def _params(sem):
    return pltpu.CompilerParams(dimension_semantics=sem, vmem_limit_bytes=VMEM_LIMIT)


def _rope_table_kernel(pos_ref, freq_ref, cos_ref, sin1_ref, sin2_ref, c_s, s_s):
    half = C_ROPE // 2
    g = pl.program_id(0)

    @pl.when(g == 0)
    def _():
        ang = pos_ref[...] * freq_ref[...]
        c_s[...] = jnp.cos(ang)
        s_s[...] = jnp.sin(ang)

    lane = lax.broadcasted_iota(jnp.int32, c_s.shape, 1)
    on_x1 = (lane >= C_NOPE) & (lane < C_NOPE + half)
    on_x2 = (lane >= C_NOPE + half) & (lane < C_NOPE + C_ROPE)
    to_x1 = (C_NOPE - g * half) % 128
    to_x2 = (C_NOPE + half - g * half) % 128
    c, s = c_s[...], s_s[...]
    nope = jnp.where(lane < C_NOPE, 1.0, 0.0)
    cos_ref[...] = jnp.where(on_x1, pltpu.roll(c, to_x1, axis=1), jnp.where(on_x2, pltpu.roll(c, to_x2, axis=1), nope))
    sin1_ref[...] = jnp.where(on_x1, -pltpu.roll(s, to_x1, axis=1), 0.0)
    sin2_ref[...] = jnp.where(on_x2, pltpu.roll(s, to_x2, axis=1), 0.0)


def _rope_tables(positions):
    t = positions.size
    half = C_ROPE // 2
    groups = 128 // half
    rows = t // groups
    inv_freq = ROPE_THETA ** (-jnp.arange(0, C_ROPE, 2, dtype=F32) / C_ROPE)
    pos_rep = jnp.repeat(positions.astype(F32).reshape(groups, rows).T, half, axis=1)
    freq = jnp.tile(inv_freq, groups).reshape(1, 128)
    table = jax.ShapeDtypeStruct((t, HEAD_PAD), F32)
    out_spec = pl.BlockSpec((rows, HEAD_PAD), lambda g: (g, 0))
    return pl.pallas_call(
        _rope_table_kernel,
        out_shape=(table,) * 3,
        grid=(groups,),
        in_specs=[_const_spec((rows, 128)), _const_spec((1, 128))],
        out_specs=(out_spec,) * 3,
        scratch_shapes=[pltpu.VMEM((rows, 128), F32), pltpu.VMEM((rows, 128), F32)],
        compiler_params=_params(("arbitrary",)),
        name="rope_tables",
    )(pos_rep, freq)


def _read_pos_major(ref):
    return jnp.concatenate([ref[blk, :, p, :] for blk in range(ref.shape[0]) for p in range(A_CHUNK)], axis=0)


def _write_token_major(ref, val):
    for blk in range(ref.shape[0]):
        for p in range(A_CHUNK):
            r0 = blk * MIX_BLOCK + p * MIX_NCHUNK
            ref[blk, :, p, :] = val[r0:r0 + MIX_NCHUNK]


def _token_blocks(x2d):
    t, wd = x2d.shape
    return x2d.reshape(t // MIX_BLOCK, MIX_NCHUNK, A_CHUNK, wd)


def _token_block_spec(wd, rows=ROW_BLOCK, first_block=0):
    nblk = rows // MIX_BLOCK
    return pl.BlockSpec((nblk, MIX_NCHUNK, A_CHUNK, wd), lambda i: (first_block + i, 0, 0, 0))


def _rope_lanes(x, cos, sin_x1, sin_x2):
    half = C_ROPE // 2
    right = pltpu.roll(x, HEAD_PAD - half, axis=1)
    left = pltpu.roll(x, half, axis=1)
    return x * cos + right * sin_x1 + left * sin_x2


def _inproj_ln_kernel(x_ref, lng_ref, lnb_ref, *rest):
    *rest, h_out = rest
    h = _layer_norm(_read_pos_major(x_ref), lng_ref[...], lnb_ref[...])
    h_out[...] = h
    _inproj_body(h, *rest)


def _inproj_kernel(h_ref, *rest):
    _inproj_body(h_ref[...], *rest)


def _inproj_body(h, cos_ref, sin1_ref, sin2_ref, w_in,
                 gq_ref, wq_ref, gkv_ref, wk_ref, wv_ref,
                 hg_out, sg_out, q_out, k_out, v_out):
    hb = h.astype(BF16)
    o_sg, o_lat = 4 * A_WIDTH, 4 * A_WIDTH + 2 * B_WIDTH
    proj = _dot(hb, w_in[...])
    hg_out[...] = proj[:, 0:o_sg]
    sg_out[...] = proj[:, o_sg:o_lat]
    lat = proj[:, o_lat:]
    cos, sin1, sin2 = cos_ref[...], sin1_ref[...], sin2_ref[...]

    q_table = (cos + pltpu.roll(sin1 + sin2, C_ROPE, axis=1)) * Q_SCALE
    cqn = _rms_norm(lat[:, 0:C_Q_RANK], gq_ref[...]).astype(BF16)
    q_lin = _dot(cqn, wq_ref[...])
    for h in range(C_HEADS):
        sl = slice(h * HEAD_PAD, (h + 1) * HEAD_PAD)
        q_out[:, sl] = (q_lin[:, sl] * q_table).astype(BF16)

    ckvn = _rms_norm(lat[:, C_Q_RANK:C_Q_RANK + C_KV_RANK], gkv_ref[...]).astype(BF16)
    k_lin = _dot(ckvn, wk_ref[...])
    vt = _dot_nt(wv_ref[...], ckvn).astype(BF16)
    for c in range(v_out.shape[0]):
        v_out[c] = vt[:, c * ATT_TK:(c + 1) * ATT_TK]
    k_rope = _rope_lanes(lat[:, C_Q_RANK + C_KV_RANK:], cos, sin1, sin2)
    k_rope = k_rope + pltpu.roll(k_rope, C_ROPE, axis=1)
    for h in range(C_HEADS):
        sl = slice(h * HEAD_PAD, (h + 1) * HEAD_PAD)
        k_out[:, sl] = (k_lin[:, sl] + k_rope).astype(BF16)


def _in_proj(h, rope_pats, sw, layer, ln=None):
    t = h.shape[0]
    tm = ROW_BLOCK
    row = lambda w: pl.BlockSpec((tm, w), lambda i: (i, 0))
    weights = [sw[n] for n in ("w_in", "gq", "wq", "gkv", "wk", "wv")]
    hp = C_HEADS * HEAD_PAD
    out_shape = [jax.ShapeDtypeStruct((t, 4 * A_WIDTH), F32),
                 jax.ShapeDtypeStruct((t, 2 * B_WIDTH), F32),
                 jax.ShapeDtypeStruct((t, hp), BF16),
                 jax.ShapeDtypeStruct((t, hp), BF16),
                 jax.ShapeDtypeStruct((t // ATT_TK, C_WIDTH, ATT_TK), BF16)]
    out_specs = [row(4 * A_WIDTH), row(2 * B_WIDTH), row(hp), row(hp),
                 pl.BlockSpec((tm // ATT_TK, C_WIDTH, ATT_TK), lambda i: (i, 0, 0))]
    common_specs = [row(HEAD_PAD)] * 3 + [_layer_spec(w, layer) for w in weights]
    if ln is None:
        body, first_specs, first_args = _inproj_kernel, [row(D_MODEL)], [h]
    else:
        body = _inproj_ln_kernel
        first_specs = [_token_block_spec(D_MODEL), _const_spec((1, D_MODEL)), _const_spec((1, D_MODEL))]
        first_args = [_token_blocks(h), ln[0].reshape(1, D_MODEL), ln[1].reshape(1, D_MODEL)]
        out_shape.append(jax.ShapeDtypeStruct((t, D_MODEL), F32))
        out_specs.append(row(D_MODEL))
    return pl.pallas_call(
        body,
        out_shape=tuple(out_shape),
        grid=(t // tm,),
        in_specs=first_specs + common_specs,
        out_specs=tuple(out_specs),
        compiler_params=_params(("parallel",)),
        name="in_proj",
    )(*first_args, *rope_pats, *weights)


def _erf(x):
    return lax.erf(x)


def _gelu(x):
    return 0.5 * x * (1.0 + _erf(x * (2.0 ** -0.5)))


def _rows_to_array(rows, like):
    zero = jnp.zeros_like(like)
    return jnp.concatenate([zero if r is None else r for r in rows], axis=0)


def _mixer_kernel(hg_ref, sg_ref, lb_ref, ng_ref, lng_ref, lnb_ref, ws_ref, bs_ref, ones_ref,
                  o_ref,
                  q_s, k_s, f_s, v_s, qt_s, kt_s, oi_s, a_s, b_s, qs_s, ks_s, state_s):
    nc, cs, w = MIX_NCHUNK, A_CHUNK, A_WIDTH
    tb = nc * cs
    slab = lambda p: slice(p * nc, (p + 1) * nc)

    @pl.when(pl.program_id(1) == 0)
    def _():
        state_s[...] = jnp.zeros_like(state_s)

    lb = lb_ref[...]
    for p in range(cs):
        aq = hg_ref[slab(p), 0:w]
        f_p = lb + (1.0 - lb) * _sigmoid(hg_ref[slab(p), w:2 * w])
        q_s[p] = aq * _sigmoid(aq)
        f_s[p] = f_p
        k_s[p] = 1.0 - f_p
        v_s[p] = hg_ref[slab(p), 2 * w:3 * w]

    ones_bd = ones_ref[...]

    kd = []
    for p in range(cs):
        f_p = f_s[p]
        q_p = q_s[p]
        kd = [k_s[p]] + [f_p * x for x in kd]
        xs = jnp.concatenate([q_p * x for x in kd], axis=0).astype(BF16)
        wgt = _dot(xs, ones_bd)
        o_p = wgt[0:nc] * v_s[p]
        for d in range(1, p + 1):
            o_p = o_p + wgt[d * nc:(d + 1) * nc] * v_s[p - d]
        oi_s[p] = o_p

    pref = f_s[0]
    qt_s[0] = q_s[0] * pref
    for p in range(1, cs):
        pref = pref * f_s[p]
        qt_s[p] = q_s[p] * pref
    chunk_decay = pref
    suf = jnp.ones_like(pref)
    kt_s[cs - 1] = k_s[cs - 1]
    for p in range(cs - 2, -1, -1):
        suf = suf * f_s[p + 1]
        kt_s[p] = k_s[p] * suf

    drow = [chunk_decay[c:c + 1, :] for c in range(nc)]
    one_row = jnp.ones_like(drow[0])
    levels = []
    m = 1
    while m < nc:
        levels.append(m)
        m *= 2
    la_arrays, lb_arrays = [], []
    for m in levels:
        la = [None] * nc
        lbr = [None] * nc
        for base in range(0, nc, 2 * m):
            mid = base + m
            la[mid] = one_row
            for i in range(mid + 1, mid + m):
                la[i] = la[i - 1] * drow[i - 1]
            lbr[mid - 1] = one_row
            for j in range(mid - 2, base - 1, -1):
                lbr[j] = lbr[j + 1] * drow[j + 1]
        la_arrays.append(_rows_to_array(la, one_row))
        lb_arrays.append(_rows_to_array(lbr, one_row))
    ep = [one_row]
    for c in range(1, nc):
        ep.append(ep[-1] * drow[c - 1])
    es = [one_row] * nc
    for c in range(nc - 2, -1, -1):
        es[c] = es[c + 1] * drow[c + 1]
    total_decay = ep[-1] * drow[nc - 1]
    ep_arr = jnp.concatenate(ep, axis=0)
    es_arr = jnp.concatenate(es, axis=0)

    for p in range(cs):
        qt_p = qt_s[p]
        kt_p = kt_s[p]
        for li in range(len(levels)):
            a_s[li, p] = qt_p * la_arrays[li]
            b_s[li, p] = kt_p * lb_arrays[li]
        qs_s[p] = qt_p * ep_arr
        ks_s[p] = kt_p * es_arr

    lane = lax.broadcasted_iota(jnp.int32, (tb, w), 1)
    head_masks = [(lane >= h * A_DK) & (lane < (h + 1) * A_DK) for h in range(A_HEADS)]
    row_c = lax.broadcasted_iota(jnp.int32, (tb, tb), 0) % nc
    col_c = lax.broadcasted_iota(jnp.int32, (tb, tb), 1) % nc
    v_tok = v_s[...].reshape(tb, w)
    v_bf = v_tok.astype(BF16)
    a_tok = [a_s[li].reshape(tb, w).astype(BF16) for li in range(len(levels))]
    b_tok = [b_s[li].reshape(tb, w).astype(BF16) for li in range(len(levels))]
    scores, v_heads = [], []
    for h in range(A_HEADS):
        hs = slice(h * A_DK, (h + 1) * A_DK)
        sc = None
        for li, m in enumerate(levels):
            s_l = _dot_nt(a_tok[li][:, hs], b_tok[li][:, hs])
            if 2 * m < nc:
                s_l = jnp.where((row_c // (2 * m)) == (col_c // (2 * m)), s_l, 0.0)
            sc = s_l if sc is None else sc + s_l
        scores.append(sc.astype(BF16))
        v_heads.append(jnp.where(head_masks[h], v_tok, 0.0).astype(BF16))
    o_cross = _dot(jnp.concatenate(scores, axis=1), jnp.concatenate(v_heads, axis=0))

    st = state_s[...]
    o_state = _dot_nt(qs_s[...].reshape(tb, w).astype(BF16), st.astype(BF16))
    kv = _dot_tn(v_bf, ks_s[...].reshape(tb, w).astype(BF16))
    rr = lax.broadcasted_iota(jnp.int32, (w, w), 0)
    cc = lax.broadcasted_iota(jnp.int32, (w, w), 1)
    state_s[...] = st * total_decay + jnp.where((rr // A_DK) == (cc // A_DK), kv, 0.0)

    o = oi_s[...].reshape(tb, w) + o_cross + o_state
    ms = _dot((o * o).astype(BF16), ones_bd) * (1.0 / A_DK)
    ag = hg_ref[:, 3 * w:4 * w]
    o_ref[:, 0:w] = (o * lax.rsqrt(ms + RMS_EPS) * ng_ref[...] * (ag * _sigmoid(ag))).astype(o_ref.dtype)

    cps = B_CHUNK // cs
    u = _gelu(sg_ref[:, 0:B_WIDTH])
    vv = _layer_norm(_gelu(sg_ref[:, B_WIDTH:2 * B_WIDTH]), lng_ref[...], lnb_ref[...])
    lane_b = lax.broadcasted_iota(jnp.int32, (B_CHUNK, B_WIDTH), 1)
    zs = []
    for ci in range(nc // cps):
        v_c = jnp.concatenate([vv[p * nc + ci * cps:p * nc + (ci + 1) * cps] for p in range(cs)],
                              axis=0).astype(BF16)
        z = bs_ref[...]
        for g in range(B_GROUPS):
            z_g = _dot(ws_ref[g], v_c)
            z = z + jnp.where((lane_b >= g * B_CH) & (lane_b < (g + 1) * B_CH), z_g, 0.0)
        zs.append(z)
    z_all = jnp.concatenate([z[p * cps:(p + 1) * cps] for p in range(cs) for z in zs], axis=0)
    o_ref[:, w:w + B_WIDTH] = (u * z_all).astype(o_ref.dtype)


def _mixers(hg, sg, sw, layer, bsz, seq):
    nb = seq // MIX_BLOCK
    nc, cs, w = MIX_NCHUNK, A_CHUNK, A_WIDTH
    nlev = int(round(math.log2(nc)))
    blk = lambda width: pl.BlockSpec((MIX_BLOCK, width), lambda b, i: (b * nb + i, 0))
    pm = lambda: pltpu.VMEM((cs, nc, w), F32)
    ow = A_WIDTH + B_WIDTH
    return pl.pallas_call(
        _mixer_kernel,
        out_shape=jax.ShapeDtypeStruct((bsz * seq, ow), BF16),
        grid=(bsz, nb),
        in_specs=[blk(4 * A_WIDTH), blk(2 * B_WIDTH),
                  ] + [_layer_spec(sw[n], layer) for n in ("lb", "hgrn_g", "sgu_g", "sgu_b", "ws", "bs")] + [
                  _const_spec((w, w))],
        out_specs=blk(ow),
        scratch_shapes=[pm(), pm(), pm(), pm(), pm(), pm(), pm(),
                        pltpu.VMEM((nlev, cs, nc, w), F32), pltpu.VMEM((nlev, cs, nc, w), F32),
                        pm(), pm(), pltpu.VMEM((w, w), F32)],
        compiler_params=_params(("parallel", "arbitrary")),
        name="mixers",
    )(hg, sg, sw["lb"], sw["hgrn_g"], sw["sgu_g"], sw["sgu_b"], sw["ws"], sw["bs"], sw["ones_bd"])


def _token_offset(i):
    r = i % MIX_BLOCK
    return (i - r) + (r % MIX_NCHUNK) * A_CHUNK + r // MIX_NCHUNK


def _attn_kernel(qx_ref, qy_ref, k_ref, vt_ref, o_ref, sa_ref, sb_ref, acc_ref):
    tq, tk = ATT_TQ, ATT_TK
    ones_rows = jnp.ones((ATT_ONES, tk), BF16)
    key = _token_offset(lax.broadcasted_iota(jnp.int32, (tk, tq), 0))
    qry = _token_offset(lax.broadcasted_iota(jnp.int32, (tk, tq), 1))
    late = slice(tq - tk, tq)

    def produce(qs, hh, j, dst, mask=None, cols=slice(None)):
        off = pl.multiple_of(j * tk, tk)
        k_t = k_ref[pl.ds(off, tk), hh * HEAD_PAD:(hh + 1) * HEAD_PAD]
        st = _dot_nt(k_t, qs[hh][cols])
        if mask is not None:
            st = jnp.where(mask, st, NEG_BIG)
        dst[hh, :, cols] = st
        return jnp.max(st, axis=0, keepdims=True)

    def consume(hh, j, src, tile_max, acc, m_old, cols=slice(None)):
        m_new = jnp.maximum(m_old, tile_max)
        alpha = jnp.exp2(m_old - m_new)
        pt = jnp.exp2(src[hh, :, cols] - m_new).astype(BF16)
        vt_h = jnp.concatenate([vt_ref[j, hh * C_V:(hh + 1) * C_V, :], ones_rows], axis=0)
        acc[hh, :, cols] = alpha * acc[hh, :, cols] + _dot(vt_h, pt)
        return m_new

    def overlap(qs, prod, cons):
        maxes, ms = [], []
        for hh in range(ATT_HEADS):
            if prod is not None:
                maxes.append(produce(qs, hh, *prod))
            if cons is not None:
                j, src, tile_max, acc, m_old = cons
                ms.append(consume(hh, j, src, tile_max[hh], acc, m_old[hh]))
        return tuple(maxes), tuple(ms)

    def clear(acc_a, acc_b):
        acc_a[...] = jnp.zeros(acc_a.shape, F32)
        acc_b[...] = jnp.zeros(acc_b.shape, F32)

    def steady(qi, qs, acc_a, acc_b, max_a):
        ms0 = tuple(jnp.full((1, tq), NEG_BIG, F32) for _ in range(ATT_HEADS))

        def body(p, carry):
            ms_a, ms_b, in_a, max_a = carry
            max_b, ms_a = overlap(qs, (2 * p, sb_ref), (in_a, sa_ref, max_a, acc_a, ms_a))
            max_a, ms_b = overlap(qs, (2 * p + 1, sa_ref), (2 * p, sb_ref, max_b, acc_b, ms_b))
            return ms_a, ms_b, 2 * p + 1, max_a

        return lax.fori_loop(0, qi, body, (ms0, ms0, 2 * qi, max_a))

    def drain_a(qi, qs, acc_a, state):
        ms_a, ms_b, in_a, max_a = state
        max_b, ms_a = overlap(qs, (2 * qi + 1, sb_ref, (key <= qry)[:, 0:tk], late),
                              (in_a, sa_ref, max_a, acc_a, ms_a))
        return ms_a, ms_b, max_b

    def drain_b(qi, acc_a, acc_b, ms_a, ms_b, max_b, out, next_block=None):
        next_max, ms_late = [], []
        for hh in range(ATT_HEADS):
            if next_block is not None:
                next_max.append(produce(next_block[1], hh, 2 * next_block[0], sa_ref, key <= qry))
            ms_late.append(consume(hh, 2 * qi + 1, sb_ref, max_b[hh], acc_b, ms_b[hh][:, late], late))
        ms_b = tuple(jnp.concatenate([ms_b[hh][:, 0:tq - tk], ms_late[hh]], axis=1) for hh in range(ATT_HEADS))
        for hh in range(ATT_HEADS):
            m = jnp.maximum(ms_a[hh], ms_b[hh])
            acc = acc_a[hh] * jnp.exp2(ms_a[hh] - m) + acc_b[hh] * jnp.exp2(ms_b[hh] - m)
            out[hh * C_V:(hh + 1) * C_V, :] = (acc[0:C_V] / acc[C_V:C_V + 1]).astype(out.dtype)
        return tuple(next_max)

    i = pl.program_id(2)
    qi_x, qi_y = i, 2 * pl.num_programs(2) - 1 - i
    qs_x = [qx_ref[:, hh * HEAD_PAD:(hh + 1) * HEAD_PAD] for hh in range(ATT_HEADS)]
    qs_y = [qy_ref[:, hh * HEAD_PAD:(hh + 1) * HEAD_PAD] for hh in range(ATT_HEADS)]
    acc_x, acc_y = (acc_ref.at[0, 0], acc_ref.at[0, 1]), (acc_ref.at[1, 0], acc_ref.at[1, 1])

    clear(*acc_x)
    clear(*acc_y)
    max_a_x, _ = overlap(qs_x, (2 * qi_x, sa_ref, key <= qry), None)
    state_x = steady(qi_x, qs_x, *acc_x, max_a_x)
    ms_a, ms_b, max_b = drain_a(qi_x, qs_x, acc_x[0], state_x)
    max_a_y = drain_b(qi_x, *acc_x, ms_a, ms_b, max_b, o_ref.at[0], next_block=(qi_y, qs_y))
    state_y = steady(qi_y, qs_y, *acc_y, max_a_y)
    ms_a, ms_b, max_b = drain_a(qi_y, qs_y, acc_y[0], state_y)
    drain_b(qi_y, *acc_y, ms_a, ms_b, max_b, o_ref.at[1])


def _attention(q, k, vt, bsz, seq):
    assert ATT_TQ == 2 * ATT_TK
    nh = ATT_HEADS
    nq = seq // ATT_TQ
    nk = seq // ATT_TK
    acc_shape = (2, 2, nh, C_V + ATT_ONES, ATT_TQ)
    return pl.pallas_call(
        _attn_kernel,
        out_shape=jax.ShapeDtypeStruct((bsz, 2, nq // 2, C_WIDTH, ATT_TQ), BF16),
        grid=(bsz, C_HEADS // nh, nq // 2),
        in_specs=[pl.BlockSpec((None, ATT_TQ, nh * HEAD_PAD), lambda b, g, i: (b, i, g)),
                  pl.BlockSpec((None, ATT_TQ, nh * HEAD_PAD), lambda b, g, i: (b, nq - 1 - i, g)),
                  pl.BlockSpec((None, seq, nh * HEAD_PAD), lambda b, g, i: (b, 0, g)),
                  pl.BlockSpec((nk, nh * C_V, ATT_TK), lambda b, g, i: (b, g, 0))],
        out_specs=pl.BlockSpec((None, 2, None, nh * C_V, ATT_TQ), lambda b, g, i: (b, 0, i, g, 0)),
        scratch_shapes=[pltpu.VMEM((nh, ATT_TK, ATT_TQ), F32), pltpu.VMEM((nh, ATT_TK, ATT_TQ), F32),
                        pltpu.VMEM(acc_shape, F32)],
        compiler_params=_params(("parallel", "parallel", "arbitrary")),
        name="mla_attention",
    )(q.reshape(bsz, seq, -1), q.reshape(bsz, seq, -1), k.reshape(bsz, seq, -1), vt)


def _attn_out_spec(seq):
    assert CH_BLOCK == ATT_TQ
    nq = seq // ATT_TQ

    def index(r):
        qb = r % nq
        return r // nq, qb // (nq // 2), jnp.where(qb < nq // 2, qb, nq - 1 - qb), 0, 0

    return pl.BlockSpec((None, None, 1, C_WIDTH, ATT_TQ), index)


def _channel_kernel(oab_ref, oct_ref, h_ref, p_ref, wab_ref, wc_ref, g1_ref, b1_ref,
                    wgu_ref, wd_ref, wpg_ref, wpp_ref, g2_ref, b2_ref, o_ref, act_s, *,
                    token_major_out):
    mix_c = jnp.concatenate([_dot_tn(oct_ref[i], wc_ref[...]) for i in range(oct_ref.shape[0])], axis=0)
    mix = _dot(oab_ref[...], wab_ref[...]) + mix_c
    h = _layer_norm(DEEPNORM_ALPHA * h_ref[...] + mix, g1_ref[...], b1_ref[...])
    hb = h.astype(BF16)
    for c in range(D_FF // FF_CHUNK):
        cols = slice(c * FF_CHUNK, (c + 1) * FF_CHUNK)
        gate = _dot(hb, wgu_ref[:, cols])
        up = _dot(hb, wgu_ref[:, D_FF + c * FF_CHUNK:D_FF + (c + 1) * FF_CHUNK])
        act_s[:, cols] = (gate * _sigmoid(gate) * up).astype(BF16)
    ffn = _dot(act_s[...], wd_ref[...])
    ple = _sigmoid(_dot(hb, wpg_ref[...])) * _dot(_read_pos_major(p_ref).astype(BF16), wpp_ref[...])
    out = _layer_norm(DEEPNORM_ALPHA * h + ffn + ple, g2_ref[...], b2_ref[...])
    if token_major_out:
        _write_token_major(o_ref, out)
    else:
        o_ref[...] = out


def _channel(o_ab, o_ct, h, p, sw, layer, seq, token_major_out):
    t = h.shape[0]
    tm = CH_BLOCK
    row = lambda w: pl.BlockSpec((tm, w), lambda i: (i, 0))
    if token_major_out:
        out_shape = jax.ShapeDtypeStruct((t // MIX_BLOCK, MIX_NCHUNK, A_CHUNK, D_MODEL), F32)
        out_spec = _token_block_spec(D_MODEL, tm)
    else:
        out_shape = jax.ShapeDtypeStruct((t, D_MODEL), F32)
        out_spec = row(D_MODEL)
    names = ("w_out_ab", "w_out_c", "ln1_g", "ln1_b", "wgu", "wd", "wpg", "wpp", "ln2_g", "ln2_b")
    return pl.pallas_call(
        functools.partial(_channel_kernel, token_major_out=token_major_out),
        out_shape=out_shape,
        grid=(t // tm,),
        in_specs=[row(A_WIDTH + B_WIDTH), _attn_out_spec(seq), row(D_MODEL),
                  _token_block_spec(PLE_DIM, tm, layer * (t // tm))] + [_layer_spec(sw[n], layer, single_buffer=True) for n in names],
        out_specs=out_spec,
        scratch_shapes=[pltpu.VMEM((tm, D_FF), BF16)],
        compiler_params=_params(("parallel",)),
        name="channel_mix",
    )(o_ab, o_ct, h, _token_blocks(p), *[sw[n] for n in names])


def _head_pad_cols(w_nope, w_rope):
    lead = w_nope.shape[:2]
    pad = jnp.zeros(lead + (C_HEADS, HEAD_PAD - C_NOPE - w_rope.shape[-1]), w_nope.dtype)
    return jnp.concatenate([w_nope, w_rope, pad], axis=-1).reshape(lead + (C_HEADS * HEAD_PAD,))


def _pos_major(x, axis):
    cps = B_CHUNK // A_CHUNK
    shape = x.shape
    x = x.reshape(shape[:axis] + (cps, A_CHUNK) + shape[axis + 1:])
    return jnp.swapaxes(x, axis, axis + 1).reshape(shape)


def _prep_weights(lower_bounds, w_in, hgrn_norm_g, sgu_ln_g, sgu_ln_b, sgu_w_s, sgu_b_s,
                  mla_q_norm_g, mla_w_uq, mla_kv_norm_g, mla_w_ukv, w_out, ln1_g, ln1_b,
                  w_gate_up, w_down, ple_w_gate, ple_w_proj, ln2_g, ln2_b):
    nl = w_in.shape[0]
    o_kr = 4 * A_WIDTH + 2 * B_WIDTH + C_Q_RANK + C_KV_RANK
    w_in_bf = w_in.astype(BF16)
    w_in_placed = jnp.concatenate(
        [w_in_bf[:, :, :o_kr], jnp.zeros((nl, D_MODEL, C_NOPE), BF16), w_in_bf[:, :, o_kr:],
         jnp.zeros((nl, D_MODEL, HEAD_PAD - C_NOPE - C_ROPE), BF16)], axis=-1)
    wq = mla_w_uq.reshape(nl, C_Q_RANK, C_HEADS, C_NOPE + C_ROPE)
    wq_nope, wq_rope = wq[..., :C_NOPE], wq[..., C_NOPE:]
    wkv = mla_w_ukv.reshape(nl, C_KV_RANK, C_HEADS, C_NOPE + C_V)
    wk_nope, wv = wkv[..., :C_NOPE], wkv[..., C_NOPE:]
    zeros_rope = jnp.zeros((nl, C_KV_RANK, C_HEADS, C_ROPE), F32)
    tri = jnp.tril(jnp.ones((B_CHUNK, B_CHUNK), F32))
    head_id = np.arange(A_WIDTH) // A_DK
    vec = lambda g: g.reshape(nl, 1, -1)
    return dict(
        w_in=w_in_placed,
        gq=vec(mla_q_norm_g), gkv=vec(mla_kv_norm_g),
        wq=_head_pad_cols(wq_nope, jnp.concatenate(
            [wq_rope, wq_rope[..., C_ROPE // 2:], wq_rope[..., :C_ROPE // 2]], axis=-1)).astype(BF16),
        wk=_head_pad_cols(wk_nope, zeros_rope).astype(BF16),
        wv=jnp.swapaxes(wv.reshape(nl, C_KV_RANK, C_WIDTH), 1, 2).astype(BF16),
        lb=vec(lower_bounds), hgrn_g=vec(hgrn_norm_g), sgu_g=vec(sgu_ln_g), sgu_b=vec(sgu_ln_b),
        ws=_pos_major(_pos_major(sgu_w_s * tri, 2), 3).astype(BF16),
        bs=_pos_major(jnp.repeat(jnp.swapaxes(sgu_b_s, 1, 2), B_CH, axis=2), 1),
        ones_bd=jnp.asarray(head_id[:, None] == head_id[None, :], BF16),
        w_out_ab=w_out[:, :A_WIDTH + B_WIDTH].astype(BF16), w_out_c=w_out[:, A_WIDTH + B_WIDTH:].astype(BF16),
        ln1_g=vec(ln1_g), ln1_b=vec(ln1_b),
        wgu=w_gate_up.astype(BF16),
        wd=w_down.astype(BF16), wpg=ple_w_gate.astype(BF16), wpp=ple_w_proj.astype(BF16),
        ln2_g=vec(ln2_g), ln2_b=vec(ln2_b),
    )


def kernel(x, p, positions, ln_in_g, ln_in_b, w_in, hgrn_lb_logits, hgrn_norm_g, sgu_ln_g, sgu_ln_b, sgu_w_s, sgu_b_s, mla_q_norm_g, mla_w_uq, mla_kv_norm_g, mla_w_ukv, w_out, ln1_g, ln1_b, w_gate_up, w_down, ple_w_gate, ple_w_proj, ln2_g, ln2_b):
    bsz, seq, d = x.shape
    t = bsz * seq
    lb_cum = jnp.cumsum(jax.nn.softmax(hgrn_lb_logits.astype(F32), axis=0), axis=0)
    lower_bounds = lb_cum - lb_cum[0]
    pos_pm = positions.reshape(bsz, seq // MIX_BLOCK, MIX_NCHUNK, A_CHUNK).swapaxes(2, 3).reshape(bsz, seq)
    rope_pats = _rope_tables(pos_pm)
    sw = _prep_weights(lower_bounds, w_in, hgrn_norm_g, sgu_ln_g, sgu_ln_b, sgu_w_s, sgu_b_s,
                       mla_q_norm_g, mla_w_uq, mla_kv_norm_g, mla_w_ukv, w_out, ln1_g, ln1_b,
                       w_gate_up, w_down, ple_w_gate, ple_w_proj, ln2_g, ln2_b)

    h = x.reshape(t, d)
    for i in range(DEPTH):
        if i == 0:
            hg, sg, q, k, v, h = _in_proj(h, rope_pats, sw, i, ln=(ln_in_g, ln_in_b))
        else:
            hg, sg, q, k, v = _in_proj(h, rope_pats, sw, i)
        o_ab = _mixers(hg, sg, sw, i, bsz, seq)
        o_c = _attention(q, k, v, bsz, seq)
        h = _channel(o_ab, o_c, h, p.reshape(DEPTH * t, PLE_DIM), sw, i, seq, token_major_out=(i == DEPTH - 1))
    return h.reshape(bsz, seq, d)
```

```python
import functools
import math

import numpy as np
import jax
import jax.numpy as jnp
from jax import lax
from jax.experimental import pallas as pl
from jax.experimental.pallas import tpu as pltpu

F32 = jnp.float32
BF16 = jnp.bfloat16

D_MODEL = 1024
DEPTH = 2
PLE_DIM = 256

A_WIDTH = 256
A_DK = 64
A_HEADS = 4
A_CHUNK = 16
B_WIDTH = 256
B_CH = 64
B_GROUPS = 4
B_CHUNK = 128
C_WIDTH = 512
C_NOPE = 64
C_ROPE = 32
C_V = 64
C_HEADS = 8
C_Q_RANK = 384
C_KV_RANK = 256
ROPE_THETA = 10000.0
LANES = 128
HEAD_PAD = LANES

D_FF = 2816
LN_EPS = 1e-5
RMS_EPS = 1e-6
DEEPNORM_ALPHA = (2 * DEPTH) ** 0.25
ATT_SCALE = (C_NOPE + C_ROPE) ** -0.5
Q_SCALE = ATT_SCALE * math.log2(math.e)

MIX_BLOCK = 256
MIX_NCHUNK = MIX_BLOCK // A_CHUNK
ATT_TQ = 512
ATT_TK = 256
ATT_HEADS = 8
ATT_ONES = 16
ROW_BLOCK = 512
CH_BLOCK = 512
FF_CHUNK = 256
VMEM_LIMIT = 56 * 1024 * 1024
NEG_BIG = -1e30


def _dot(a, b):
    return jnp.dot(a, b, preferred_element_type=F32)


def _dot_nt(a, b):
    return lax.dot_general(a, b, (((1,), (1,)), ((), ())), preferred_element_type=F32)


def _dot_tn(a, b):
    return lax.dot_general(a, b, (((0,), (0,)), ((), ())), preferred_element_type=F32)


def _layer_norm(x, g, b):
    mu = jnp.mean(x, axis=-1, keepdims=True)
    xc = x - mu
    var = jnp.mean(xc * xc, axis=-1, keepdims=True)
    return xc * lax.rsqrt(var + LN_EPS) * g + b


def _rms_norm(x, g):
    return x * lax.rsqrt(jnp.mean(x * x, axis=-1, keepdims=True) + RMS_EPS) * g


def _sigmoid(x):
    return 1.0 / (1.0 + jnp.exp(-x))


def _const_spec(shape):
    nd = len(shape)
    return pl.BlockSpec(shape, lambda *_: (0,) * nd)


def _layer_spec(arr, layer, single_buffer=False):
    shape = arr.shape[1:]
    nd = len(shape)
    kw = dict(pipeline_mode=pl.Buffered(1)) if single_buffer else {}
    return pl.BlockSpec((None,) + shape, lambda *_: (layer,) + (0,) * nd, **kw)


def _params(sem):
    return pltpu.CompilerParams(dimension_semantics=sem, vmem_limit_bytes=VMEM_LIMIT)


def _rope_table_kernel(pos_ref, freq_ref, cos_ref, sin_ref, c_s, s_s):
    half = C_ROPE // 2
    g = pl.program_id(0)

    @pl.when(g == 0)
    def _():
        ang = pos_ref[...] * freq_ref[...]
        c_s[...] = jnp.cos(ang)
        s_s[...] = jnp.sin(ang)

    lane = lax.broadcasted_iota(jnp.int32, c_s.shape, 1)
    on_x1 = (lane >= C_NOPE) & (lane < C_NOPE + half)
    on_x2 = (lane >= C_NOPE + half) & (lane < C_NOPE + C_ROPE)
    to_x1 = (C_NOPE - g * half) % LANES
    to_x2 = (C_NOPE + half - g * half) % LANES
    c, s = c_s[...], s_s[...]
    nope = jnp.where(lane < C_NOPE, 1.0, 0.0)
    cos_ref[...] = jnp.where(on_x1, pltpu.roll(c, to_x1, axis=1), jnp.where(on_x2, pltpu.roll(c, to_x2, axis=1), nope))
    sin_ref[...] = jnp.where(on_x1, -pltpu.roll(s, to_x1, axis=1), jnp.where(on_x2, pltpu.roll(s, to_x2, axis=1), 0.0))


def _rope_tables(positions):
    t = positions.size
    half = C_ROPE // 2
    groups = LANES // half
    rows = t // groups
    inv_freq = ROPE_THETA ** (-jnp.arange(0, C_ROPE, 2, dtype=F32) / C_ROPE)
    pos_rep = jnp.repeat(positions.astype(F32).reshape(groups, rows).T, half, axis=1)
    freq = jnp.tile(inv_freq, groups).reshape(1, LANES)
    table = jax.ShapeDtypeStruct((t, HEAD_PAD), F32)
    out_spec = pl.BlockSpec((rows, HEAD_PAD), lambda g: (g, 0))
    return pl.pallas_call(
        _rope_table_kernel,
        out_shape=(table,) * 2,
        grid=(groups,),
        in_specs=[_const_spec((rows, LANES)), _const_spec((1, LANES))],
        out_specs=(out_spec,) * 2,
        scratch_shapes=[pltpu.VMEM((rows, LANES), F32), pltpu.VMEM((rows, LANES), F32)],
        compiler_params=_params(("arbitrary",)),
        name="rope_tables",
    )(pos_rep, freq)


def _read_pos_major(ref):
    return jnp.concatenate([ref[blk, :, p, :] for blk in range(ref.shape[0]) for p in range(A_CHUNK)], axis=0)


def _write_token_major(ref, val):
    for blk in range(ref.shape[0]):
        for p in range(A_CHUNK):
            r0 = blk * MIX_BLOCK + p * MIX_NCHUNK
            ref[blk, :, p, :] = val[r0:r0 + MIX_NCHUNK]


def _token_blocks(x2d):
    t, wd = x2d.shape
    return x2d.reshape(t // MIX_BLOCK, MIX_NCHUNK, A_CHUNK, wd)


def _token_block_spec(wd, rows=ROW_BLOCK, first_block=0):
    nblk = rows // MIX_BLOCK
    return pl.BlockSpec((nblk, MIX_NCHUNK, A_CHUNK, wd), lambda i: (first_block + i, 0, 0, 0))


def _rope_lanes(x, cos, sin):
    half = C_ROPE // 2
    lane = lax.broadcasted_iota(jnp.int32, x.shape, 1)
    right = pltpu.roll(x, HEAD_PAD - half, axis=1)
    left = pltpu.roll(x, half, axis=1)
    return x * cos + jnp.where(lane < C_NOPE + half, right, left) * sin


def _inproj_ln_kernel(x_ref, lng_ref, lnb_ref, *rest):
    *rest, h_out = rest
    h = _layer_norm(_read_pos_major(x_ref), lng_ref[...], lnb_ref[...])
    h_out[...] = h
    _inproj_body(h, *rest)


def _inproj_kernel(h_ref, *rest):
    _inproj_body(h_ref[...], *rest)


def _inproj_body(h, cos_ref, sin_ref, w_in,
                 gq_ref, wq_ref, gkv_ref, wk_ref, wv_ref,
                 hg_out, sg_out, q_out, k_out, v_out):
    hb = h.astype(BF16)
    o_sg, o_lat = 4 * A_WIDTH, 4 * A_WIDTH + 2 * B_WIDTH
    proj = _dot(hb, w_in[...])
    hg_out[...] = proj[:, 0:o_sg]
    sg_out[...] = proj[:, o_sg:o_lat]
    lat = proj[:, o_lat:]
    cos, sin = cos_ref[...], sin_ref[...]

    q_table = (cos + pltpu.roll(sin, C_ROPE, axis=1)) * Q_SCALE
    cqn = _rms_norm(lat[:, 0:C_Q_RANK], gq_ref[...]).astype(BF16)
    q_lin = _dot(cqn, wq_ref[...])
    for h in range(C_HEADS):
        sl = slice(h * HEAD_PAD, (h + 1) * HEAD_PAD)
        q_out[:, sl] = (q_lin[:, sl] * q_table).astype(BF16)

    ckvn = _rms_norm(lat[:, C_Q_RANK:C_Q_RANK + C_KV_RANK], gkv_ref[...]).astype(BF16)
    k_lin = _dot(ckvn, wk_ref[...])
    vt = _dot_nt(wv_ref[...], ckvn).astype(BF16)
    for c in range(v_out.shape[0]):
        v_out[c] = vt[:, c * ATT_TK:(c + 1) * ATT_TK]
    k_rope = _rope_lanes(lat[:, C_Q_RANK + C_KV_RANK:], cos, sin)
    k_rope = k_rope + pltpu.roll(k_rope, C_ROPE, axis=1)
    for h in range(C_HEADS):
        sl = slice(h * HEAD_PAD, (h + 1) * HEAD_PAD)
        k_out[:, sl] = (k_lin[:, sl] + k_rope).astype(BF16)


def _in_proj(h, rope_pats, sw, layer, ln=None):
    t = h.shape[0]
    tm = ROW_BLOCK
    row = lambda w: pl.BlockSpec((tm, w), lambda i: (i, 0))
    weights = [sw[n] for n in ("w_in", "gq", "wq", "gkv", "wk", "wv")]
    hp = C_HEADS * HEAD_PAD
    out_shape = [jax.ShapeDtypeStruct((t, 4 * A_WIDTH), F32),
                 jax.ShapeDtypeStruct((t, 2 * B_WIDTH), F32),
                 jax.ShapeDtypeStruct((t, hp), BF16),
                 jax.ShapeDtypeStruct((t, hp), BF16),
                 jax.ShapeDtypeStruct((t // ATT_TK, C_WIDTH, ATT_TK), BF16)]
    out_specs = [row(4 * A_WIDTH), row(2 * B_WIDTH), row(hp), row(hp),
                 pl.BlockSpec((tm // ATT_TK, C_WIDTH, ATT_TK), lambda i: (i, 0, 0))]
    common_specs = [row(HEAD_PAD)] * 2 + [_layer_spec(w, layer) for w in weights]
    if ln is None:
        body, first_specs, first_args = _inproj_kernel, [row(D_MODEL)], [h]
    else:
        body = _inproj_ln_kernel
        first_specs = [_token_block_spec(D_MODEL), _const_spec((1, D_MODEL)), _const_spec((1, D_MODEL))]
        first_args = [_token_blocks(h), ln[0].reshape(1, D_MODEL), ln[1].reshape(1, D_MODEL)]
        out_shape.append(jax.ShapeDtypeStruct((t, D_MODEL), F32))
        out_specs.append(row(D_MODEL))
    return pl.pallas_call(
        body,
        out_shape=tuple(out_shape),
        grid=(t // tm,),
        in_specs=first_specs + common_specs,
        out_specs=tuple(out_specs),
        compiler_params=_params(("parallel",)),
        name="in_proj",
    )(*first_args, *rope_pats, *weights)


def _erf(x):
    return lax.erf(x)


def _gelu(x):
    return 0.5 * x * (1.0 + _erf(x * (2.0 ** -0.5)))


def _rows_to_array(rows, like):
    zero = jnp.zeros_like(like)
    return jnp.concatenate([zero if r is None else r for r in rows], axis=0)


def _mixer_kernel(hg_ref, sg_ref, lb_ref, ng_ref, lng_ref, lnb_ref, ws_ref, bs_ref, ones_ref,
                  o_ref,
                  q_s, k_s, f_s, v_s, qt_s, kt_s, oi_s, a_s, b_s, qs_s, ks_s, state_s):
    nc, cs, w = MIX_NCHUNK, A_CHUNK, A_WIDTH
    tb = nc * cs
    slab = lambda p: slice(p * nc, (p + 1) * nc)

    @pl.when(pl.program_id(1) == 0)
    def _():
        state_s[...] = jnp.zeros_like(state_s)

    lb = lb_ref[...]
    for p in range(cs):
        aq = hg_ref[slab(p), 0:w]
        f_p = lb + (1.0 - lb) * _sigmoid(hg_ref[slab(p), w:2 * w])
        q_s[p] = aq * _sigmoid(aq)
        f_s[p] = f_p
        k_s[p] = 1.0 - f_p
        v_s[p] = hg_ref[slab(p), 2 * w:3 * w]

    ones_bd = ones_ref[...]

    kd = []
    for p in range(cs):
        f_p = f_s[p]
        q_p = q_s[p]
        kd = [k_s[p]] + [f_p * x for x in kd]
        xs = jnp.concatenate([q_p * x for x in kd], axis=0).astype(BF16)
        wgt = _dot(xs, ones_bd)
        o_p = wgt[0:nc] * v_s[p]
        for d in range(1, p + 1):
            o_p = o_p + wgt[d * nc:(d + 1) * nc] * v_s[p - d]
        oi_s[p] = o_p

    pref = f_s[0]
    qt_s[0] = q_s[0] * pref
    for p in range(1, cs):
        pref = pref * f_s[p]
        qt_s[p] = q_s[p] * pref
    chunk_decay = pref
    suf = jnp.ones_like(pref)
    kt_s[cs - 1] = k_s[cs - 1]
    for p in range(cs - 2, -1, -1):
        suf = suf * f_s[p + 1]
        kt_s[p] = k_s[p] * suf

    drow = [chunk_decay[c:c + 1, :] for c in range(nc)]
    one_row = jnp.ones_like(drow[0])
    levels = []
    m = 1
    while m < nc:
        levels.append(m)
        m *= 2
    la_arrays, lb_arrays = [], []
    for m in levels:
        la = [None] * nc
        lbr = [None] * nc
        for base in range(0, nc, 2 * m):
            mid = base + m
            la[mid] = one_row
            for i in range(mid + 1, mid + m):
                la[i] = la[i - 1] * drow[i - 1]
            lbr[mid - 1] = one_row
            for j in range(mid - 2, base - 1, -1):
                lbr[j] = lbr[j + 1] * drow[j + 1]
        la_arrays.append(_rows_to_array(la, one_row))
        lb_arrays.append(_rows_to_array(lbr, one_row))
    ep = [one_row]
    for c in range(1, nc):
        ep.append(ep[-1] * drow[c - 1])
    es = [one_row] * nc
    for c in range(nc - 2, -1, -1):
        es[c] = es[c + 1] * drow[c + 1]
    total_decay = ep[-1] * drow[nc - 1]
    ep_arr = jnp.concatenate(ep, axis=0)
    es_arr = jnp.concatenate(es, axis=0)

    for p in range(cs):
        qt_p = qt_s[p]
        kt_p = kt_s[p]
        for li in range(len(levels)):
            a_s[li, p] = qt_p * la_arrays[li]
            b_s[li, p] = kt_p * lb_arrays[li]
        qs_s[p] = qt_p * ep_arr
        ks_s[p] = kt_p * es_arr

    lane = lax.broadcasted_iota(jnp.int32, (tb, w), 1)
    head_masks = [(lane >= h * A_DK) & (lane < (h + 1) * A_DK) for h in range(A_HEADS)]
    row_c = lax.broadcasted_iota(jnp.int32, (tb, tb), 0) % nc
    col_c = lax.broadcasted_iota(jnp.int32, (tb, tb), 1) % nc
    v_tok = v_s[...].reshape(tb, w)
    v_bf = v_tok.astype(BF16)
    a_tok = [a_s[li].reshape(tb, w).astype(BF16) for li in range(len(levels))]
    b_tok = [b_s[li].reshape(tb, w).astype(BF16) for li in range(len(levels))]
    scores, v_heads = [], []
    for h in range(A_HEADS):
        hs = slice(h * A_DK, (h + 1) * A_DK)
        sc = None
        for li, m in enumerate(levels):
            s_l = _dot_nt(a_tok[li][:, hs], b_tok[li][:, hs])
            if 2 * m < nc:
                s_l = jnp.where((row_c // (2 * m)) == (col_c // (2 * m)), s_l, 0.0)
            sc = s_l if sc is None else sc + s_l
        scores.append(sc.astype(BF16))
        v_heads.append(jnp.where(head_masks[h], v_tok, 0.0).astype(BF16))
    o_cross = _dot(jnp.concatenate(scores, axis=1), jnp.concatenate(v_heads, axis=0))

    st = state_s[...]
    o_state = _dot_nt(qs_s[...].reshape(tb, w).astype(BF16), st.astype(BF16))
    kv = _dot_tn(v_bf, ks_s[...].reshape(tb, w).astype(BF16))
    rr = lax.broadcasted_iota(jnp.int32, (w, w), 0)
    cc = lax.broadcasted_iota(jnp.int32, (w, w), 1)
    state_s[...] = st * total_decay + jnp.where((rr // A_DK) == (cc // A_DK), kv, 0.0)

    o = oi_s[...].reshape(tb, w) + o_cross + o_state
    ms = _dot((o * o).astype(BF16), ones_bd) * (1.0 / A_DK)
    ag = hg_ref[:, 3 * w:4 * w]
    o_ref[:, 0:w] = (o * lax.rsqrt(ms + RMS_EPS) * ng_ref[...] * (ag * _sigmoid(ag))).astype(o_ref.dtype)

    cps = B_CHUNK // cs
    u = _gelu(sg_ref[:, 0:B_WIDTH])
    vv = _layer_norm(_gelu(sg_ref[:, B_WIDTH:2 * B_WIDTH]), lng_ref[...], lnb_ref[...])
    lane_b = lax.broadcasted_iota(jnp.int32, (B_CHUNK, B_WIDTH), 1)
    zs = []
    for ci in range(nc // cps):
        v_c = jnp.concatenate([vv[p * nc + ci * cps:p * nc + (ci + 1) * cps] for p in range(cs)],
                              axis=0).astype(BF16)
        v_groups = jnp.concatenate(
            [jnp.where((lane_b >= g * B_CH) & (lane_b < (g + 1) * B_CH), v_c, 0.0) for g in range(B_GROUPS)], axis=0)
        zs.append(bs_ref[...] + _dot(ws_ref[...], v_groups))
    z_all = jnp.concatenate([z[p * cps:(p + 1) * cps] for p in range(cs) for z in zs], axis=0)
    o_ref[:, w:w + B_WIDTH] = (u * z_all).astype(o_ref.dtype)


def _mixers(hg, sg, sw, layer, bsz, seq):
    nb = seq // MIX_BLOCK
    nc, cs, w = MIX_NCHUNK, A_CHUNK, A_WIDTH
    nlev = int(round(math.log2(nc)))
    blk = lambda width: pl.BlockSpec((MIX_BLOCK, width), lambda b, i: (b * nb + i, 0))
    pm = lambda: pltpu.VMEM((cs, nc, w), F32)
    ow = A_WIDTH + B_WIDTH
    return pl.pallas_call(
        _mixer_kernel,
        out_shape=jax.ShapeDtypeStruct((bsz * seq, ow), BF16),
        grid=(bsz, nb),
        in_specs=[blk(4 * A_WIDTH), blk(2 * B_WIDTH),
                  ] + [_layer_spec(sw[n], layer) for n in ("lb", "hgrn_g", "sgu_g", "sgu_b", "ws", "bs")] + [
                  _const_spec((w, w))],
        out_specs=blk(ow),
        scratch_shapes=[pm(), pm(), pm(), pm(), pm(), pm(), pm(),
                        pltpu.VMEM((nlev, cs, nc, w), F32), pltpu.VMEM((nlev, cs, nc, w), F32),
                        pm(), pm(), pltpu.VMEM((w, w), F32)],
        compiler_params=_params(("parallel", "arbitrary")),
        name="mixers",
    )(hg, sg, sw["lb"], sw["hgrn_g"], sw["sgu_g"], sw["sgu_b"], sw["ws"], sw["bs"], sw["ones_bd"])


def _token_offset(i):
    r = i % MIX_BLOCK
    return (i - r) + (r % MIX_NCHUNK) * A_CHUNK + r // MIX_NCHUNK


def _attn_kernel(qx_ref, qy_ref, k_ref, vt_ref, o_ref, sa_ref, sb_ref, acc_ref):
    tq, tk = ATT_TQ, ATT_TK
    ones_rows = jnp.ones((ATT_ONES, tk), BF16)
    key = _token_offset(lax.broadcasted_iota(jnp.int32, (tk, tq), 0))
    qry = _token_offset(lax.broadcasted_iota(jnp.int32, (tk, tq), 1))
    late = slice(tq - tk, tq)

    def produce(qs, hh, j, dst, mask=None, cols=slice(None)):
        off = pl.multiple_of(j * tk, tk)
        k_t = k_ref[pl.ds(off, tk), hh * HEAD_PAD:(hh + 1) * HEAD_PAD]
        st = _dot_nt(k_t, qs[hh][cols])
        if mask is not None:
            st = jnp.where(mask, st, NEG_BIG)
        dst[hh, :, cols] = st
        return jnp.max(st, axis=0, keepdims=True)

    def consume(hh, j, src, tile_max, acc, m_old, cols=slice(None)):
        m_new = jnp.maximum(m_old, tile_max)
        alpha = jnp.exp2(m_old - m_new)
        pt = jnp.exp2(src[hh, :, cols] - m_new).astype(BF16)
        vt_h = jnp.concatenate([vt_ref[j, hh * C_V:(hh + 1) * C_V, :], ones_rows], axis=0)
        acc[hh, :, cols] = alpha * acc[hh, :, cols] + _dot(vt_h, pt)
        return m_new

    def overlap(qs, prod, cons):
        maxes, ms = [], []
        for hh in range(ATT_HEADS):
            if prod is not None:
                maxes.append(produce(qs, hh, *prod))
            if cons is not None:
                j, src, tile_max, acc, m_old = cons
                ms.append(consume(hh, j, src, tile_max[hh], acc, m_old[hh]))
        return tuple(maxes), tuple(ms)

    def clear(acc_a, acc_b):
        acc_a[...] = jnp.zeros(acc_a.shape, F32)
        acc_b[...] = jnp.zeros(acc_b.shape, F32)

    def steady(qi, qs, acc_a, acc_b, max_a):
        ms0 = tuple(jnp.full((1, tq), NEG_BIG, F32) for _ in range(ATT_HEADS))

        def body(p, carry):
            ms_a, ms_b, in_a, max_a = carry
            max_b, ms_a = overlap(qs, (2 * p, sb_ref), (in_a, sa_ref, max_a, acc_a, ms_a))
            max_a, ms_b = overlap(qs, (2 * p + 1, sa_ref), (2 * p, sb_ref, max_b, acc_b, ms_b))
            return ms_a, ms_b, 2 * p + 1, max_a

        return lax.fori_loop(0, qi, body, (ms0, ms0, 2 * qi, max_a))

    def drain_a(qi, qs, acc_a, state):
        ms_a, ms_b, in_a, max_a = state
        max_b, ms_a = overlap(qs, (2 * qi + 1, sb_ref, (key <= qry)[:, 0:tk], late),
                              (in_a, sa_ref, max_a, acc_a, ms_a))
        return ms_a, ms_b, max_b

    def drain_b(qi, acc_a, acc_b, ms_a, ms_b, max_b, out, next_block=None):
        next_max, ms_late = [], []
        for hh in range(ATT_HEADS):
            if next_block is not None:
                next_max.append(produce(next_block[1], hh, 2 * next_block[0], sa_ref, key <= qry))
            ms_late.append(consume(hh, 2 * qi + 1, sb_ref, max_b[hh], acc_b, ms_b[hh][:, late], late))
        ms_b = tuple(jnp.concatenate([ms_b[hh][:, 0:tq - tk], ms_late[hh]], axis=1) for hh in range(ATT_HEADS))
        for hh in range(ATT_HEADS):
            m = jnp.maximum(ms_a[hh], ms_b[hh])
            acc = acc_a[hh] * jnp.exp2(ms_a[hh] - m) + acc_b[hh] * jnp.exp2(ms_b[hh] - m)
            out[hh * C_V:(hh + 1) * C_V, :] = (acc[0:C_V] / acc[C_V:C_V + 1]).astype(out.dtype)
        return tuple(next_max)

    i = pl.program_id(2)
    qi_x, qi_y = i, 2 * pl.num_programs(2) - 1 - i
    qs_x = [qx_ref[:, hh * HEAD_PAD:(hh + 1) * HEAD_PAD] for hh in range(ATT_HEADS)]
    qs_y = [qy_ref[:, hh * HEAD_PAD:(hh + 1) * HEAD_PAD] for hh in range(ATT_HEADS)]
    acc_x, acc_y = (acc_ref.at[0, 0], acc_ref.at[0, 1]), (acc_ref.at[1, 0], acc_ref.at[1, 1])

    clear(*acc_x)
    clear(*acc_y)
    max_a_x, _ = overlap(qs_x, (2 * qi_x, sa_ref, key <= qry), None)
    state_x = steady(qi_x, qs_x, *acc_x, max_a_x)
    ms_a, ms_b, max_b = drain_a(qi_x, qs_x, acc_x[0], state_x)
    max_a_y = drain_b(qi_x, *acc_x, ms_a, ms_b, max_b, o_ref.at[0], next_block=(qi_y, qs_y))
    state_y = steady(qi_y, qs_y, *acc_y, max_a_y)
    ms_a, ms_b, max_b = drain_a(qi_y, qs_y, acc_y[0], state_y)
    drain_b(qi_y, *acc_y, ms_a, ms_b, max_b, o_ref.at[1])


def _attention(q, k, vt, bsz, seq):
    assert ATT_TQ == 2 * ATT_TK
    nh = ATT_HEADS
    nq = seq // ATT_TQ
    nk = seq // ATT_TK
    acc_shape = (2, 2, nh, C_V + ATT_ONES, ATT_TQ)
    return pl.pallas_call(
        _attn_kernel,
        out_shape=jax.ShapeDtypeStruct((bsz, 2, nq // 2, C_WIDTH, ATT_TQ), BF16),
        grid=(bsz, C_HEADS // nh, nq // 2),
        in_specs=[pl.BlockSpec((None, ATT_TQ, nh * HEAD_PAD), lambda b, g, i: (b, i, g)),
                  pl.BlockSpec((None, ATT_TQ, nh * HEAD_PAD), lambda b, g, i: (b, nq - 1 - i, g)),
                  pl.BlockSpec((None, seq, nh * HEAD_PAD), lambda b, g, i: (b, 0, g)),
                  pl.BlockSpec((nk, nh * C_V, ATT_TK), lambda b, g, i: (b, g, 0))],
        out_specs=pl.BlockSpec((None, 2, None, nh * C_V, ATT_TQ), lambda b, g, i: (b, 0, i, g, 0)),
        scratch_shapes=[pltpu.VMEM((nh, ATT_TK, ATT_TQ), F32), pltpu.VMEM((nh, ATT_TK, ATT_TQ), F32),
                        pltpu.VMEM(acc_shape, F32)],
        compiler_params=_params(("parallel", "parallel", "arbitrary")),
        name="mla_attention",
    )(q.reshape(bsz, seq, -1), q.reshape(bsz, seq, -1), k.reshape(bsz, seq, -1), vt)


def _attn_out_spec(seq):
    assert CH_BLOCK == ATT_TQ
    nq = seq // ATT_TQ

    def index(r):
        qb = r % nq
        return r // nq, qb // (nq // 2), jnp.where(qb < nq // 2, qb, nq - 1 - qb), 0, 0

    return pl.BlockSpec((None, None, 1, C_WIDTH, ATT_TQ), index)


def _channel_kernel(oab_ref, oct_ref, h_ref, p_ref, wab_ref, wc_ref, g1_ref, b1_ref,
                    wgu_ref, wd_ref, wpg_ref, wpp_ref, g2_ref, b2_ref, o_ref, act_s, *,
                    token_major_out):
    mix_c = jnp.concatenate([_dot_tn(oct_ref[i], wc_ref[...]) for i in range(oct_ref.shape[0])], axis=0)
    mix = _dot(oab_ref[...], wab_ref[...]) + mix_c
    h = _layer_norm(DEEPNORM_ALPHA * h_ref[...] + mix, g1_ref[...], b1_ref[...])
    hb = h.astype(BF16)
    for c in range(D_FF // FF_CHUNK):
        cols = slice(c * FF_CHUNK, (c + 1) * FF_CHUNK)
        gate = _dot(hb, wgu_ref[:, cols])
        up = _dot(hb, wgu_ref[:, D_FF + c * FF_CHUNK:D_FF + (c + 1) * FF_CHUNK])
        act_s[:, cols] = (gate * _sigmoid(gate) * up).astype(BF16)
    ffn = _dot(act_s[...], wd_ref[...])
    ple = _sigmoid(_dot(hb, wpg_ref[...])) * _dot(_read_pos_major(p_ref).astype(BF16), wpp_ref[...])
    out = _layer_norm(DEEPNORM_ALPHA * h + ffn + ple, g2_ref[...], b2_ref[...])
    if token_major_out:
        _write_token_major(o_ref, out)
    else:
        o_ref[...] = out


def _channel(o_ab, o_ct, h, p, sw, layer, seq, token_major_out):
    t = h.shape[0]
    tm = CH_BLOCK
    row = lambda w: pl.BlockSpec((tm, w), lambda i: (i, 0))
    if token_major_out:
        out_shape = jax.ShapeDtypeStruct((t // MIX_BLOCK, MIX_NCHUNK, A_CHUNK, D_MODEL), F32)
        out_spec = _token_block_spec(D_MODEL, tm)
    else:
        out_shape = jax.ShapeDtypeStruct((t, D_MODEL), F32)
        out_spec = row(D_MODEL)
    names = ("w_out_ab", "w_out_c", "ln1_g", "ln1_b", "wgu", "wd", "wpg", "wpp", "ln2_g", "ln2_b")
    return pl.pallas_call(
        functools.partial(_channel_kernel, token_major_out=token_major_out),
        out_shape=out_shape,
        grid=(t // tm,),
        in_specs=[row(A_WIDTH + B_WIDTH), _attn_out_spec(seq), row(D_MODEL),
                  _token_block_spec(PLE_DIM, tm, layer * (t // tm))] + [_layer_spec(sw[n], layer, single_buffer=True) for n in names],
        out_specs=out_spec,
        scratch_shapes=[pltpu.VMEM((tm, D_FF), BF16)],
        compiler_params=_params(("parallel",)),
        name="channel_mix",
    )(o_ab, o_ct, h, _token_blocks(p), *[sw[n] for n in names])


def _head_pad_cols(w_nope, w_rope):
    lead = w_nope.shape[:2]
    pad = jnp.zeros(lead + (C_HEADS, HEAD_PAD - C_NOPE - w_rope.shape[-1]), w_nope.dtype)
    return jnp.concatenate([w_nope, w_rope, pad], axis=-1).reshape(lead + (C_HEADS * HEAD_PAD,))


def _pos_major(x, axis):
    cps = B_CHUNK // A_CHUNK
    shape = x.shape
    x = x.reshape(shape[:axis] + (cps, A_CHUNK) + shape[axis + 1:])
    return jnp.swapaxes(x, axis, axis + 1).reshape(shape)


def _prep_weights(lower_bounds, w_in, hgrn_norm_g, sgu_ln_g, sgu_ln_b, sgu_w_s, sgu_b_s,
                  mla_q_norm_g, mla_w_uq, mla_kv_norm_g, mla_w_ukv, w_out, ln1_g, ln1_b,
                  w_gate_up, w_down, ple_w_gate, ple_w_proj, ln2_g, ln2_b):
    nl = w_in.shape[0]
    o_kr = 4 * A_WIDTH + 2 * B_WIDTH + C_Q_RANK + C_KV_RANK
    w_in_bf = w_in.astype(BF16)
    w_in_placed = jnp.concatenate(
        [w_in_bf[:, :, :o_kr], jnp.zeros((nl, D_MODEL, C_NOPE), BF16), w_in_bf[:, :, o_kr:],
         jnp.zeros((nl, D_MODEL, HEAD_PAD - C_NOPE - C_ROPE), BF16)], axis=-1)
    wq = mla_w_uq.reshape(nl, C_Q_RANK, C_HEADS, C_NOPE + C_ROPE)
    wq_nope, wq_rope = wq[..., :C_NOPE], wq[..., C_NOPE:]
    wkv = mla_w_ukv.reshape(nl, C_KV_RANK, C_HEADS, C_NOPE + C_V)
    wk_nope, wv = wkv[..., :C_NOPE], wkv[..., C_NOPE:]
    zeros_rope = jnp.zeros((nl, C_KV_RANK, C_HEADS, C_ROPE), F32)
    tri = jnp.tril(jnp.ones((B_CHUNK, B_CHUNK), F32))
    head_id = np.arange(A_WIDTH) // A_DK
    vec = lambda g: g.reshape(nl, 1, -1)
    return dict(
        w_in=w_in_placed,
        gq=vec(mla_q_norm_g), gkv=vec(mla_kv_norm_g),
        wq=_head_pad_cols(wq_nope, jnp.concatenate(
            [wq_rope, wq_rope[..., C_ROPE // 2:], wq_rope[..., :C_ROPE // 2]], axis=-1)).astype(BF16),
        wk=_head_pad_cols(wk_nope, zeros_rope).astype(BF16),
        wv=jnp.swapaxes(wv.reshape(nl, C_KV_RANK, C_WIDTH), 1, 2).astype(BF16),
        lb=vec(lower_bounds), hgrn_g=vec(hgrn_norm_g), sgu_g=vec(sgu_ln_g), sgu_b=vec(sgu_ln_b),
        ws=jnp.swapaxes(_pos_major(_pos_major(sgu_w_s * tri, 2), 3), 1, 2).reshape(
            nl, B_CHUNK, B_GROUPS * B_CHUNK).astype(BF16),
        bs=_pos_major(jnp.repeat(jnp.swapaxes(sgu_b_s, 1, 2), B_CH, axis=2), 1),
        ones_bd=jnp.asarray(head_id[:, None] == head_id[None, :], BF16),
        w_out_ab=w_out[:, :A_WIDTH + B_WIDTH].astype(BF16), w_out_c=w_out[:, A_WIDTH + B_WIDTH:].astype(BF16),
        ln1_g=vec(ln1_g), ln1_b=vec(ln1_b),
        wgu=w_gate_up.astype(BF16),
        wd=w_down.astype(BF16), wpg=ple_w_gate.astype(BF16), wpp=ple_w_proj.astype(BF16),
        ln2_g=vec(ln2_g), ln2_b=vec(ln2_b),
    )


def kernel(x, p, positions, ln_in_g, ln_in_b, w_in, hgrn_lb_logits, hgrn_norm_g, sgu_ln_g, sgu_ln_b, sgu_w_s, sgu_b_s, mla_q_norm_g, mla_w_uq, mla_kv_norm_g, mla_w_ukv, w_out, ln1_g, ln1_b, w_gate_up, w_down, ple_w_gate, ple_w_proj, ln2_g, ln2_b):
    bsz, seq, d = x.shape
    t = bsz * seq
    lb_cum = jnp.cumsum(jax.nn.softmax(hgrn_lb_logits.astype(F32), axis=0), axis=0)
    lower_bounds = lb_cum - lb_cum[0]
    pos_pm = positions.reshape(bsz, seq // MIX_BLOCK, MIX_NCHUNK, A_CHUNK).swapaxes(2, 3).reshape(bsz, seq)
    rope_pats = _rope_tables(pos_pm)
    sw = _prep_weights(lower_bounds, w_in, hgrn_norm_g, sgu_ln_g, sgu_ln_b, sgu_w_s, sgu_b_s,
                       mla_q_norm_g, mla_w_uq, mla_kv_norm_g, mla_w_ukv, w_out, ln1_g, ln1_b,
                       w_gate_up, w_down, ple_w_gate, ple_w_proj, ln2_g, ln2_b)

    h = x.reshape(t, d)
    for i in range(DEPTH):
        if i == 0:
            hg, sg, q, k, v, h = _in_proj(h, rope_pats, sw, i, ln=(ln_in_g, ln_in_b))
        else:
            hg, sg, q, k, v = _in_proj(h, rope_pats, sw, i)
        o_ab = _mixers(hg, sg, sw, i, bsz, seq)
        o_c = _attention(q, k, v, bsz, seq)
        h = _channel(o_ab, o_c, h, p.reshape(DEPTH * t, PLE_DIM), sw, i, seq, token_major_out=(i == DEPTH - 1))
    return h.reshape(bsz, seq, d)
```

```python
import functools
import math

import numpy as np
import jax
import jax.numpy as jnp
from jax import lax
from jax.experimental import pallas as pl
from jax.experimental.pallas import tpu as pltpu

F32 = jnp.float32
BF16 = jnp.bfloat16

D_MODEL = 1024
DEPTH = 2
PLE_DIM = 256

A_WIDTH = 256
A_DK = 64
A_HEADS = 4
A_CHUNK = 16
B_WIDTH = 256
B_CH = 64
B_GROUPS = 4
B_CHUNK = 128
C_WIDTH = 512
C_NOPE = 64
C_ROPE = 32
C_V = 64
C_HEADS = 8
C_Q_RANK = 384
C_KV_RANK = 256
ROPE_THETA = 10000.0
LANES = 128
HEAD_PAD = LANES

D_FF = 2816
LN_EPS = 1e-5
RMS_EPS = 1e-6
DEEPNORM_ALPHA = (2 * DEPTH) ** 0.25
ATT_SCALE = (C_NOPE + C_ROPE) ** -0.5
Q_SCALE = ATT_SCALE * math.log2(math.e)

MIX_BLOCK = 256
MIX_NCHUNK = MIX_BLOCK // A_CHUNK
ATT_TQ = 512
ATT_TK = 256
ATT_HEADS = 8
ATT_ONES = 16
ROW_BLOCK = 512
CH_BLOCK = 512
FF_CHUNK = 256
VMEM_LIMIT = 56 * 1024 * 1024
NEG_BIG = -1e30


def _dot(a, b):
    return jnp.dot(a, b, preferred_element_type=F32)


def _dot_nt(a, b):
    return lax.dot_general(a, b, (((1,), (1,)), ((), ())), preferred_element_type=F32)


def _dot_tn(a, b):
    return lax.dot_general(a, b, (((0,), (0,)), ((), ())), preferred_element_type=F32)


def _layer_norm(x, g, b):
    mu = jnp.mean(x, axis=-1, keepdims=True)
    xc = x - mu
    var = jnp.mean(xc * xc, axis=-1, keepdims=True)
    return xc * lax.rsqrt(var + LN_EPS) * g + b


def _rms_norm(x, g):
    return x * lax.rsqrt(jnp.mean(x * x, axis=-1, keepdims=True) + RMS_EPS) * g


def _sigmoid(x):
    return 1.0 / (1.0 + jnp.exp(-x))


def _const_spec(shape):
    nd = len(shape)
    return pl.BlockSpec(shape, lambda *_: (0,) * nd)


def _layer_spec(arr, layer, single_buffer=False):
    shape = arr.shape[1:]
    nd = len(shape)
    kw = dict(pipeline_mode=pl.Buffered(1)) if single_buffer else {}
    return pl.BlockSpec((None,) + shape, lambda *_: (layer,) + (0,) * nd, **kw)


def _params(sem):
    return pltpu.CompilerParams(dimension_semantics=sem, vmem_limit_bytes=VMEM_LIMIT)


def _rope_table_kernel(pos_ref, freq_ref, cos_ref, sin_ref, c_s, s_s):
    half = C_ROPE // 2
    g = pl.program_id(0)

    @pl.when(g == 0)
    def _():
        ang = pos_ref[...] * freq_ref[...]
        c_s[...] = jnp.cos(ang)
        s_s[...] = jnp.sin(ang)

    lane = lax.broadcasted_iota(jnp.int32, c_s.shape, 1)
    on_x1 = (lane >= C_NOPE) & (lane < C_NOPE + half)
    on_x2 = (lane >= C_NOPE + half) & (lane < C_NOPE + C_ROPE)
    to_x1 = (C_NOPE - g * half) % LANES
    to_x2 = (C_NOPE + half - g * half) % LANES
    c, s = c_s[...], s_s[...]
    nope = jnp.where(lane < C_NOPE, 1.0, 0.0)
    cos_ref[...] = jnp.where(on_x1, pltpu.roll(c, to_x1, axis=1), jnp.where(on_x2, pltpu.roll(c, to_x2, axis=1), nope))
    sin_ref[...] = jnp.where(on_x1, -pltpu.roll(s, to_x1, axis=1), jnp.where(on_x2, pltpu.roll(s, to_x2, axis=1), 0.0))


def _rope_tables(positions):
    t = positions.size
    half = C_ROPE // 2
    groups = LANES // half
    rows = t // groups
    inv_freq = ROPE_THETA ** (-jnp.arange(0, C_ROPE, 2, dtype=F32) / C_ROPE)
    pos_rep = jnp.repeat(positions.astype(F32).reshape(groups, rows).T, half, axis=1)
    freq = jnp.tile(inv_freq, groups).reshape(1, LANES)
    table = jax.ShapeDtypeStruct((t, HEAD_PAD), F32)
    out_spec = pl.BlockSpec((rows, HEAD_PAD), lambda g: (g, 0))
    return pl.pallas_call(
        _rope_table_kernel,
        out_shape=(table,) * 2,
        grid=(groups,),
        in_specs=[_const_spec((rows, LANES)), _const_spec((1, LANES))],
        out_specs=(out_spec,) * 2,
        scratch_shapes=[pltpu.VMEM((rows, LANES), F32), pltpu.VMEM((rows, LANES), F32)],
        compiler_params=_params(("arbitrary",)),
        name="rope_tables",
    )(pos_rep, freq)


def _read_pos_major(ref):
    return jnp.concatenate([ref[blk, :, p, :] for blk in range(ref.shape[0]) for p in range(A_CHUNK)], axis=0)


def _write_token_major(ref, val):
    for blk in range(ref.shape[0]):
        for p in range(A_CHUNK):
            r0 = blk * MIX_BLOCK + p * MIX_NCHUNK
            ref[blk, :, p, :] = val[r0:r0 + MIX_NCHUNK]


def _token_blocks(x2d):
    t, wd = x2d.shape
    return x2d.reshape(t // MIX_BLOCK, MIX_NCHUNK, A_CHUNK, wd)


def _token_block_spec(wd, rows=ROW_BLOCK, first_block=0):
    nblk = rows // MIX_BLOCK
    return pl.BlockSpec((nblk, MIX_NCHUNK, A_CHUNK, wd), lambda i: (first_block + i, 0, 0, 0))


def _rope_lanes(x, cos, sin):
    half = C_ROPE // 2
    lane = lax.broadcasted_iota(jnp.int32, x.shape, 1)
    right = pltpu.roll(x, HEAD_PAD - half, axis=1)
    left = pltpu.roll(x, half, axis=1)
    return x * cos + jnp.where(lane < C_NOPE + half, right, left) * sin


def _inproj_ln_kernel(x_ref, lng_ref, lnb_ref, *rest):
    *rest, h_out = rest
    h = _layer_norm(_read_pos_major(x_ref), lng_ref[...], lnb_ref[...])
    h_out[...] = h
    _inproj_body(h, *rest)


def _inproj_kernel(h_ref, *rest):
    _inproj_body(h_ref[...], *rest)


def _inproj_body(h, cos_ref, sin_ref, w_in,
                 gq_ref, wq_ref, gkv_ref, wk_ref, wv_ref,
                 hg_out, sg_out, q_out, k_out, v_out):
    hb = h.astype(BF16)
    o_sg, o_lat = 4 * A_WIDTH, 4 * A_WIDTH + 2 * B_WIDTH
    proj = _dot(hb, w_in[...])
    hg_out[...] = proj[:, 0:o_sg]
    sg_out[...] = proj[:, o_sg:o_lat]
    lat = proj[:, o_lat:]
    cos, sin = cos_ref[...], sin_ref[...]

    q_table = (cos + pltpu.roll(sin, C_ROPE, axis=1)) * Q_SCALE
    cqn = _rms_norm(lat[:, 0:C_Q_RANK], gq_ref[...]).astype(BF16)
    q_lin = _dot(cqn, wq_ref[...])
    for h in range(C_HEADS):
        sl = slice(h * HEAD_PAD, (h + 1) * HEAD_PAD)
        q_out[:, sl] = (q_lin[:, sl] * q_table).astype(BF16)

    ckvn = _rms_norm(lat[:, C_Q_RANK:C_Q_RANK + C_KV_RANK], gkv_ref[...]).astype(BF16)
    k_lin = _dot(ckvn, wk_ref[...])
    vt = _dot_nt(wv_ref[...], ckvn).astype(BF16)
    for c in range(v_out.shape[0]):
        v_out[c] = vt[:, c * ATT_TK:(c + 1) * ATT_TK]
    k_rope = _rope_lanes(lat[:, C_Q_RANK + C_KV_RANK:], cos, sin)
    k_rope = k_rope + pltpu.roll(k_rope, C_ROPE, axis=1)
    for h in range(C_HEADS):
        sl = slice(h * HEAD_PAD, (h + 1) * HEAD_PAD)
        k_out[:, sl] = (k_lin[:, sl] + k_rope).astype(BF16)


def _in_proj(h, rope_pats, sw, layer, ln=None):
    t = h.shape[0]
    tm = ROW_BLOCK
    row = lambda w: pl.BlockSpec((tm, w), lambda i: (i, 0))
    weights = [sw[n] for n in ("w_in", "gq", "wq", "gkv", "wk", "wv")]
    hp = C_HEADS * HEAD_PAD
    out_shape = [jax.ShapeDtypeStruct((t, 4 * A_WIDTH), F32),
                 jax.ShapeDtypeStruct((t, 2 * B_WIDTH), F32),
                 jax.ShapeDtypeStruct((t, hp), BF16),
                 jax.ShapeDtypeStruct((t, hp), BF16),
                 jax.ShapeDtypeStruct((t // ATT_TK, C_WIDTH, ATT_TK), BF16)]
    out_specs = [row(4 * A_WIDTH), row(2 * B_WIDTH), row(hp), row(hp),
                 pl.BlockSpec((tm // ATT_TK, C_WIDTH, ATT_TK), lambda i: (i, 0, 0))]
    common_specs = [row(HEAD_PAD)] * 2 + [_layer_spec(w, layer) for w in weights]
    if ln is None:
        body, first_specs, first_args = _inproj_kernel, [row(D_MODEL)], [h]
    else:
        body = _inproj_ln_kernel
        first_specs = [_token_block_spec(D_MODEL), _const_spec((1, D_MODEL)), _const_spec((1, D_MODEL))]
        first_args = [_token_blocks(h), ln[0].reshape(1, D_MODEL), ln[1].reshape(1, D_MODEL)]
        out_shape.append(jax.ShapeDtypeStruct((t, D_MODEL), F32))
        out_specs.append(row(D_MODEL))
    return pl.pallas_call(
        body,
        out_shape=tuple(out_shape),
        grid=(t // tm,),
        in_specs=first_specs + common_specs,
        out_specs=tuple(out_specs),
        compiler_params=_params(("parallel",)),
        name="in_proj",
    )(*first_args, *rope_pats, *weights)


def _erf(x):
    return lax.erf(x)


def _gelu(x):
    return 0.5 * x * (1.0 + _erf(x * (2.0 ** -0.5)))


def _rows_to_array(rows, like):
    zero = jnp.zeros_like(like)
    return jnp.concatenate([zero if r is None else r for r in rows], axis=0)


def _mixer_kernel(hg_ref, sg_ref, lb_ref, ng_ref, lng_ref, lnb_ref, ws_ref, bs_ref, ones_ref, gmask_ref,
                  o_ref,
                  q_s, k_s, f_s, v_s, qt_s, kt_s, oi_s, a_s, b_s, qs_s, ks_s, state_s):
    nc, cs, w = MIX_NCHUNK, A_CHUNK, A_WIDTH
    tb = nc * cs
    slab = lambda p: slice(p * nc, (p + 1) * nc)

    @pl.when(pl.program_id(1) == 0)
    def _():
        state_s[...] = jnp.zeros_like(state_s)

    lb = lb_ref[...]
    for p in range(cs):
        aq = hg_ref[slab(p), 0:w]
        f_p = lb + (1.0 - lb) * _sigmoid(hg_ref[slab(p), w:2 * w])
        q_s[p] = aq * _sigmoid(aq)
        f_s[p] = f_p
        k_s[p] = 1.0 - f_p
        v_s[p] = hg_ref[slab(p), 2 * w:3 * w]

    ones_bd = ones_ref[...]

    kd = []
    for p in range(cs):
        f_p = f_s[p]
        q_p = q_s[p]
        kd = [k_s[p]] + [f_p * x for x in kd]
        xs = jnp.concatenate([q_p * x for x in kd], axis=0).astype(BF16)
        wgt = _dot(xs, ones_bd)
        o_p = wgt[0:nc] * v_s[p]
        for d in range(1, p + 1):
            o_p = o_p + wgt[d * nc:(d + 1) * nc] * v_s[p - d]
        oi_s[p] = o_p

    pref = f_s[0]
    qt_s[0] = q_s[0] * pref
    for p in range(1, cs):
        pref = pref * f_s[p]
        qt_s[p] = q_s[p] * pref
    chunk_decay = pref
    suf = jnp.ones_like(pref)
    kt_s[cs - 1] = k_s[cs - 1]
    for p in range(cs - 2, -1, -1):
        suf = suf * f_s[p + 1]
        kt_s[p] = k_s[p] * suf

    drow = [chunk_decay[c:c + 1, :] for c in range(nc)]
    one_row = jnp.ones_like(drow[0])
    levels = []
    m = 1
    while m < nc:
        levels.append(m)
        m *= 2
    la_arrays, lb_arrays = [], []
    for m in levels:
        la = [None] * nc
        lbr = [None] * nc
        for base in range(0, nc, 2 * m):
            mid = base + m
            la[mid] = one_row
            for i in range(mid + 1, mid + m):
                la[i] = la[i - 1] * drow[i - 1]
            lbr[mid - 1] = one_row
            for j in range(mid - 2, base - 1, -1):
                lbr[j] = lbr[j + 1] * drow[j + 1]
        la_arrays.append(_rows_to_array(la, one_row))
        lb_arrays.append(_rows_to_array(lbr, one_row))
    ep = [one_row]
    for c in range(1, nc):
        ep.append(ep[-1] * drow[c - 1])
    es = [one_row] * nc
    for c in range(nc - 2, -1, -1):
        es[c] = es[c + 1] * drow[c + 1]
    total_decay = ep[-1] * drow[nc - 1]
    ep_arr = jnp.concatenate(ep, axis=0)
    es_arr = jnp.concatenate(es, axis=0)

    for p in range(cs):
        qt_p = qt_s[p]
        kt_p = kt_s[p]
        for li in range(len(levels)):
            a_s[li, p] = qt_p * la_arrays[li]
            b_s[li, p] = kt_p * lb_arrays[li]
        qs_s[p] = qt_p * ep_arr
        ks_s[p] = kt_p * es_arr

    lane = lax.broadcasted_iota(jnp.int32, (tb, w), 1)
    head_masks = [(lane >= h * A_DK) & (lane < (h + 1) * A_DK) for h in range(A_HEADS)]
    v_tok = v_s[...].reshape(tb, w)
    v_bf = v_tok.astype(BF16)
    a_tok = [a_s[li].reshape(tb, w).astype(BF16) for li in range(len(levels))]
    b_tok = [b_s[li].reshape(tb, w).astype(BF16) for li in range(len(levels))]
    scores, v_heads = [], []
    for h in range(A_HEADS):
        hs = slice(h * A_DK, (h + 1) * A_DK)
        sc = None
        for li, m in enumerate(levels):
            s_l = _dot_nt(a_tok[li][:, hs], b_tok[li][:, hs])
            if 2 * m < nc:
                s_l = s_l * gmask_ref[li]
            sc = s_l if sc is None else sc + s_l
        scores.append(sc.astype(BF16))
        v_heads.append(jnp.where(head_masks[h], v_tok, 0.0).astype(BF16))
    o_cross = _dot(jnp.concatenate(scores, axis=1), jnp.concatenate(v_heads, axis=0))

    st = state_s[...]
    o_state = _dot_nt(qs_s[...].reshape(tb, w).astype(BF16), st.astype(BF16))
    kv = _dot_tn(v_bf, ks_s[...].reshape(tb, w).astype(BF16))
    rr = lax.broadcasted_iota(jnp.int32, (w, w), 0)
    cc = lax.broadcasted_iota(jnp.int32, (w, w), 1)
    state_s[...] = st * total_decay + jnp.where((rr // A_DK) == (cc // A_DK), kv, 0.0)

    o = oi_s[...].reshape(tb, w) + o_cross + o_state
    ms = _dot((o * o).astype(BF16), ones_bd) * (1.0 / A_DK)
    ag = hg_ref[:, 3 * w:4 * w]
    o_ref[:, 0:w] = (o * lax.rsqrt(ms + RMS_EPS) * ng_ref[...] * (ag * _sigmoid(ag))).astype(o_ref.dtype)

    cps = B_CHUNK // cs
    u = _gelu(sg_ref[:, 0:B_WIDTH])
    vv = _layer_norm(_gelu(sg_ref[:, B_WIDTH:2 * B_WIDTH]), lng_ref[...], lnb_ref[...])
    lane_b = lax.broadcasted_iota(jnp.int32, (B_CHUNK, B_WIDTH), 1)
    zs = []
    for ci in range(nc // cps):
        v_c = jnp.concatenate([vv[p * nc + ci * cps:p * nc + (ci + 1) * cps] for p in range(cs)],
                              axis=0).astype(BF16)
        v_groups = jnp.concatenate(
            [jnp.where((lane_b >= g * B_CH) & (lane_b < (g + 1) * B_CH), v_c, 0.0) for g in range(B_GROUPS)], axis=0)
        zs.append(bs_ref[...] + _dot(ws_ref[...], v_groups))
    z_all = jnp.concatenate([z[p * cps:(p + 1) * cps] for p in range(cs) for z in zs], axis=0)
    o_ref[:, w:w + B_WIDTH] = (u * z_all).astype(o_ref.dtype)


def _group_masks():
    chunk = np.arange(MIX_BLOCK) % MIX_NCHUNK
    levels = int(round(math.log2(MIX_NCHUNK))) - 1
    return np.stack([(chunk[:, None] >> (l + 1)) == (chunk[None, :] >> (l + 1)) for l in range(levels)]
                    ).astype(np.float32)


def _mixers(hg, sg, sw, layer, bsz, seq):
    nb = seq // MIX_BLOCK
    nc, cs, w = MIX_NCHUNK, A_CHUNK, A_WIDTH
    nlev = int(round(math.log2(nc)))
    blk = lambda width: pl.BlockSpec((MIX_BLOCK, width), lambda b, i: (b * nb + i, 0))
    pm = lambda: pltpu.VMEM((cs, nc, w), F32)
    ow = A_WIDTH + B_WIDTH
    return pl.pallas_call(
        _mixer_kernel,
        out_shape=jax.ShapeDtypeStruct((bsz * seq, ow), BF16),
        grid=(bsz, nb),
        in_specs=[blk(4 * A_WIDTH), blk(2 * B_WIDTH),
                  ] + [_layer_spec(sw[n], layer) for n in ("lb", "hgrn_g", "sgu_g", "sgu_b", "ws", "bs")] + [
                  _const_spec((w, w)), _const_spec((nlev - 1, MIX_BLOCK, MIX_BLOCK))],
        out_specs=blk(ow),
        scratch_shapes=[pm(), pm(), pm(), pm(), pm(), pm(), pm(),
                        pltpu.VMEM((nlev, cs, nc, w), F32), pltpu.VMEM((nlev, cs, nc, w), F32),
                        pm(), pm(), pltpu.VMEM((w, w), F32)],
        compiler_params=_params(("parallel", "arbitrary")),
        name="mixers",
    )(hg, sg, sw["lb"], sw["hgrn_g"], sw["sgu_g"], sw["sgu_b"], sw["ws"], sw["bs"], sw["ones_bd"], _group_masks())


def _token_offset(i):
    r = i % MIX_BLOCK
    return (i - r) + (r % MIX_NCHUNK) * A_CHUNK + r // MIX_NCHUNK


def _diagonal_bias():
    key = _token_offset(np.arange(ATT_TK))[:, None]
    qry = _token_offset(np.arange(ATT_TQ))[None, :]
    return np.where(key <= qry, 0.0, NEG_BIG).astype(np.float32)


def _attn_kernel(qx_ref, qy_ref, k_ref, vt_ref, bias_ref, o_ref, sa_ref, sb_ref, acc_ref):
    tq, tk = ATT_TQ, ATT_TK
    ones_rows = jnp.ones((ATT_ONES, tk), BF16)
    late = slice(tq - tk, tq)

    def produce(qs, hh, j, dst, bias=None, cols=slice(None)):
        off = pl.multiple_of(j * tk, tk)
        k_t = k_ref[pl.ds(off, tk), hh * HEAD_PAD:(hh + 1) * HEAD_PAD]
        st = _dot_nt(k_t, qs[hh][cols])
        if bias is not None:
            st = st + bias
        dst[hh, :, cols] = st
        return jnp.max(st, axis=0, keepdims=True)

    def consume(hh, j, src, tile_max, acc, m_old, cols=slice(None)):
        m_new = jnp.maximum(m_old, tile_max)
        alpha = jnp.exp2(m_old - m_new)
        pt = jnp.exp2(src[hh, :, cols] - m_new).astype(BF16)
        vt_h = jnp.concatenate([vt_ref[j, hh * C_V:(hh + 1) * C_V, :], ones_rows], axis=0)
        acc[hh, :, cols] = alpha * acc[hh, :, cols] + _dot(vt_h, pt)
        return m_new

    def overlap(qs, prod, cons):
        maxes, ms = [], []
        for hh in range(ATT_HEADS):
            if prod is not None:
                maxes.append(produce(qs, hh, *prod))
            if cons is not None:
                j, src, tile_max, acc, m_old = cons
                ms.append(consume(hh, j, src, tile_max[hh], acc, m_old[hh]))
        return tuple(maxes), tuple(ms)

    def clear(acc_a, acc_b):
        acc_a[...] = jnp.zeros(acc_a.shape, F32)
        acc_b[...] = jnp.zeros(acc_b.shape, F32)

    def steady(qi, qs, acc_a, acc_b, max_a):
        ms0 = tuple(jnp.full((1, tq), NEG_BIG, F32) for _ in range(ATT_HEADS))

        def body(p, carry):
            ms_a, ms_b, in_a, max_a = carry
            max_b, ms_a = overlap(qs, (2 * p, sb_ref), (in_a, sa_ref, max_a, acc_a, ms_a))
            max_a, ms_b = overlap(qs, (2 * p + 1, sa_ref), (2 * p, sb_ref, max_b, acc_b, ms_b))
            return ms_a, ms_b, 2 * p + 1, max_a

        return lax.fori_loop(0, qi, body, (ms0, ms0, 2 * qi, max_a))

    def drain_a(qi, qs, acc_a, state):
        ms_a, ms_b, in_a, max_a = state
        max_b, ms_a = overlap(qs, (2 * qi + 1, sb_ref, bias_ref[:, 0:tk], late),
                              (in_a, sa_ref, max_a, acc_a, ms_a))
        return ms_a, ms_b, max_b

    def drain_b(qi, acc_a, acc_b, ms_a, ms_b, max_b, out, next_block=None):
        next_max, ms_late = [], []
        for hh in range(ATT_HEADS):
            if next_block is not None:
                next_max.append(produce(next_block[1], hh, 2 * next_block[0], sa_ref, bias_ref[...]))
            ms_late.append(consume(hh, 2 * qi + 1, sb_ref, max_b[hh], acc_b, ms_b[hh][:, late], late))
        ms_b = tuple(jnp.concatenate([ms_b[hh][:, 0:tq - tk], ms_late[hh]], axis=1) for hh in range(ATT_HEADS))
        for hh in range(ATT_HEADS):
            m = jnp.maximum(ms_a[hh], ms_b[hh])
            acc = acc_a[hh] * jnp.exp2(ms_a[hh] - m) + acc_b[hh] * jnp.exp2(ms_b[hh] - m)
            out[hh * C_V:(hh + 1) * C_V, :] = (acc[0:C_V] / acc[C_V:C_V + 1]).astype(out.dtype)
        return tuple(next_max)

    i = pl.program_id(2)
    qi_x, qi_y = i, 2 * pl.num_programs(2) - 1 - i
    qs_x = [qx_ref[:, hh * HEAD_PAD:(hh + 1) * HEAD_PAD] for hh in range(ATT_HEADS)]
    qs_y = [qy_ref[:, hh * HEAD_PAD:(hh + 1) * HEAD_PAD] for hh in range(ATT_HEADS)]
    acc_x, acc_y = (acc_ref.at[0, 0], acc_ref.at[0, 1]), (acc_ref.at[1, 0], acc_ref.at[1, 1])

    clear(*acc_x)
    clear(*acc_y)
    max_a_x, _ = overlap(qs_x, (2 * qi_x, sa_ref, bias_ref[...]), None)
    state_x = steady(qi_x, qs_x, *acc_x, max_a_x)
    ms_a, ms_b, max_b = drain_a(qi_x, qs_x, acc_x[0], state_x)
    max_a_y = drain_b(qi_x, *acc_x, ms_a, ms_b, max_b, o_ref.at[0], next_block=(qi_y, qs_y))
    state_y = steady(qi_y, qs_y, *acc_y, max_a_y)
    ms_a, ms_b, max_b = drain_a(qi_y, qs_y, acc_y[0], state_y)
    drain_b(qi_y, *acc_y, ms_a, ms_b, max_b, o_ref.at[1])


def _attention(q, k, vt, bsz, seq):
    assert ATT_TQ == 2 * ATT_TK
    nh = ATT_HEADS
    nq = seq // ATT_TQ
    nk = seq // ATT_TK
    acc_shape = (2, 2, nh, C_V + ATT_ONES, ATT_TQ)
    return pl.pallas_call(
        _attn_kernel,
        out_shape=jax.ShapeDtypeStruct((bsz, 2, nq // 2, C_WIDTH, ATT_TQ), BF16),
        grid=(bsz, C_HEADS // nh, nq // 2),
        in_specs=[pl.BlockSpec((None, ATT_TQ, nh * HEAD_PAD), lambda b, g, i: (b, i, g)),
                  pl.BlockSpec((None, ATT_TQ, nh * HEAD_PAD), lambda b, g, i: (b, nq - 1 - i, g)),
                  pl.BlockSpec((None, seq, nh * HEAD_PAD), lambda b, g, i: (b, 0, g)),
                  pl.BlockSpec((nk, nh * C_V, ATT_TK), lambda b, g, i: (b, g, 0)),
                  _const_spec((ATT_TK, ATT_TQ))],
        out_specs=pl.BlockSpec((None, 2, None, nh * C_V, ATT_TQ), lambda b, g, i: (b, 0, i, g, 0)),
        scratch_shapes=[pltpu.VMEM((nh, ATT_TK, ATT_TQ), F32), pltpu.VMEM((nh, ATT_TK, ATT_TQ), F32),
                        pltpu.VMEM(acc_shape, F32)],
        compiler_params=_params(("parallel", "parallel", "arbitrary")),
        name="mla_attention",
    )(q.reshape(bsz, seq, -1), q.reshape(bsz, seq, -1), k.reshape(bsz, seq, -1), vt, _diagonal_bias())


def _attn_out_spec(seq):
    assert CH_BLOCK == ATT_TQ
    nq = seq // ATT_TQ

    def index(r):
        qb = r % nq
        return r // nq, qb // (nq // 2), jnp.where(qb < nq // 2, qb, nq - 1 - qb), 0, 0

    return pl.BlockSpec((None, None, 1, C_WIDTH, ATT_TQ), index)


def _channel_kernel(oab_ref, oct_ref, h_ref, p_ref, wab_ref, wc_ref, g1_ref, b1_ref,
                    wgu_ref, wd_ref, wpg_ref, wpp_ref, g2_ref, b2_ref, o_ref, act_s, *,
                    token_major_out):
    mix_c = jnp.concatenate([_dot_tn(oct_ref[i], wc_ref[...]) for i in range(oct_ref.shape[0])], axis=0)
    mix = _dot(oab_ref[...], wab_ref[...]) + mix_c
    h = _layer_norm(DEEPNORM_ALPHA * h_ref[...] + mix, g1_ref[...], b1_ref[...])
    hb = h.astype(BF16)
    for c in range(D_FF // FF_CHUNK):
        cols = slice(c * FF_CHUNK, (c + 1) * FF_CHUNK)
        gate = _dot(hb, wgu_ref[:, cols])
        up = _dot(hb, wgu_ref[:, D_FF + c * FF_CHUNK:D_FF + (c + 1) * FF_CHUNK])
        act_s[:, cols] = (gate * _sigmoid(gate) * up).astype(BF16)
    ffn = _dot(act_s[...], wd_ref[...])
    ple = _sigmoid(_dot(hb, wpg_ref[...])) * _dot(_read_pos_major(p_ref).astype(BF16), wpp_ref[...])
    out = _layer_norm(DEEPNORM_ALPHA * h + ffn + ple, g2_ref[...], b2_ref[...])
    if token_major_out:
        _write_token_major(o_ref, out)
    else:
        o_ref[...] = out


def _channel(o_ab, o_ct, h, p, sw, layer, seq, token_major_out):
    t = h.shape[0]
    tm = CH_BLOCK
    row = lambda w: pl.BlockSpec((tm, w), lambda i: (i, 0))
    if token_major_out:
        out_shape = jax.ShapeDtypeStruct((t // MIX_BLOCK, MIX_NCHUNK, A_CHUNK, D_MODEL), F32)
        out_spec = _token_block_spec(D_MODEL, tm)
    else:
        out_shape = jax.ShapeDtypeStruct((t, D_MODEL), F32)
        out_spec = row(D_MODEL)
    names = ("w_out_ab", "w_out_c", "ln1_g", "ln1_b", "wgu", "wd", "wpg", "wpp", "ln2_g", "ln2_b")
    return pl.pallas_call(
        functools.partial(_channel_kernel, token_major_out=token_major_out),
        out_shape=out_shape,
        grid=(t // tm,),
        in_specs=[row(A_WIDTH + B_WIDTH), _attn_out_spec(seq), row(D_MODEL),
                  _token_block_spec(PLE_DIM, tm, layer * (t // tm))] + [_layer_spec(sw[n], layer, single_buffer=True) for n in names],
        out_specs=out_spec,
        scratch_shapes=[pltpu.VMEM((tm, D_FF), BF16)],
        compiler_params=_params(("parallel",)),
        name="channel_mix",
    )(o_ab, o_ct, h, _token_blocks(p), *[sw[n] for n in names])


def _head_pad_cols(w_nope, w_rope):
    lead = w_nope.shape[:2]
    pad = jnp.zeros(lead + (C_HEADS, HEAD_PAD - C_NOPE - w_rope.shape[-1]), w_nope.dtype)
    return jnp.concatenate([w_nope, w_rope, pad], axis=-1).reshape(lead + (C_HEADS * HEAD_PAD,))


def _pos_major(x, axis):
    cps = B_CHUNK // A_CHUNK
    shape = x.shape
    x = x.reshape(shape[:axis] + (cps, A_CHUNK) + shape[axis + 1:])
    return jnp.swapaxes(x, axis, axis + 1).reshape(shape)


def _prep_weights(lower_bounds, w_in, hgrn_norm_g, sgu_ln_g, sgu_ln_b, sgu_w_s, sgu_b_s,
                  mla_q_norm_g, mla_w_uq, mla_kv_norm_g, mla_w_ukv, w_out, ln1_g, ln1_b,
                  w_gate_up, w_down, ple_w_gate, ple_w_proj, ln2_g, ln2_b):
    nl = w_in.shape[0]
    o_kr = 4 * A_WIDTH + 2 * B_WIDTH + C_Q_RANK + C_KV_RANK
    w_in_bf = w_in.astype(BF16)
    w_in_placed = jnp.concatenate(
        [w_in_bf[:, :, :o_kr], jnp.zeros((nl, D_MODEL, C_NOPE), BF16), w_in_bf[:, :, o_kr:],
         jnp.zeros((nl, D_MODEL, HEAD_PAD - C_NOPE - C_ROPE), BF16)], axis=-1)
    wq = mla_w_uq.reshape(nl, C_Q_RANK, C_HEADS, C_NOPE + C_ROPE)
    wq_nope, wq_rope = wq[..., :C_NOPE], wq[..., C_NOPE:]
    wkv = mla_w_ukv.reshape(nl, C_KV_RANK, C_HEADS, C_NOPE + C_V)
    wk_nope, wv = wkv[..., :C_NOPE], wkv[..., C_NOPE:]
    zeros_rope = jnp.zeros((nl, C_KV_RANK, C_HEADS, C_ROPE), F32)
    tri = jnp.tril(jnp.ones((B_CHUNK, B_CHUNK), F32))
    head_id = np.arange(A_WIDTH) // A_DK
    vec = lambda g: g.reshape(nl, 1, -1)
    return dict(
        w_in=w_in_placed,
        gq=vec(mla_q_norm_g), gkv=vec(mla_kv_norm_g),
        wq=_head_pad_cols(wq_nope, jnp.concatenate(
            [wq_rope, wq_rope[..., C_ROPE // 2:], wq_rope[..., :C_ROPE // 2]], axis=-1)).astype(BF16),
        wk=_head_pad_cols(wk_nope, zeros_rope).astype(BF16),
        wv=jnp.swapaxes(wv.reshape(nl, C_KV_RANK, C_WIDTH), 1, 2).astype(BF16),
        lb=vec(lower_bounds), hgrn_g=vec(hgrn_norm_g), sgu_g=vec(sgu_ln_g), sgu_b=vec(sgu_ln_b),
        ws=jnp.swapaxes(_pos_major(_pos_major(sgu_w_s * tri, 2), 3), 1, 2).reshape(
            nl, B_CHUNK, B_GROUPS * B_CHUNK).astype(BF16),
        bs=_pos_major(jnp.repeat(jnp.swapaxes(sgu_b_s, 1, 2), B_CH, axis=2), 1),
        ones_bd=jnp.asarray(head_id[:, None] == head_id[None, :], BF16),
        w_out_ab=w_out[:, :A_WIDTH + B_WIDTH].astype(BF16), w_out_c=w_out[:, A_WIDTH + B_WIDTH:].astype(BF16),
        ln1_g=vec(ln1_g), ln1_b=vec(ln1_b),
        wgu=w_gate_up.astype(BF16),
        wd=w_down.astype(BF16), wpg=ple_w_gate.astype(BF16), wpp=ple_w_proj.astype(BF16),
        ln2_g=vec(ln2_g), ln2_b=vec(ln2_b),
    )


def kernel(x, p, positions, ln_in_g, ln_in_b, w_in, hgrn_lb_logits, hgrn_norm_g, sgu_ln_g, sgu_ln_b, sgu_w_s, sgu_b_s, mla_q_norm_g, mla_w_uq, mla_kv_norm_g, mla_w_ukv, w_out, ln1_g, ln1_b, w_gate_up, w_down, ple_w_gate, ple_w_proj, ln2_g, ln2_b):
    bsz, seq, d = x.shape
    t = bsz * seq
    lb_cum = jnp.cumsum(jax.nn.softmax(hgrn_lb_logits.astype(F32), axis=0), axis=0)
    lower_bounds = lb_cum - lb_cum[0]
    pos_pm = positions.reshape(bsz, seq // MIX_BLOCK, MIX_NCHUNK, A_CHUNK).swapaxes(2, 3).reshape(bsz, seq)
    rope_pats = _rope_tables(pos_pm)
    sw = _prep_weights(lower_bounds, w_in, hgrn_norm_g, sgu_ln_g, sgu_ln_b, sgu_w_s, sgu_b_s,
                       mla_q_norm_g, mla_w_uq, mla_kv_norm_g, mla_w_ukv, w_out, ln1_g, ln1_b,
                       w_gate_up, w_down, ple_w_gate, ple_w_proj, ln2_g, ln2_b)

    h = x.reshape(t, d)
    for i in range(DEPTH):
        if i == 0:
            hg, sg, q, k, v, h = _in_proj(h, rope_pats, sw, i, ln=(ln_in_g, ln_in_b))
        else:
            hg, sg, q, k, v = _in_proj(h, rope_pats, sw, i)
        o_ab = _mixers(hg, sg, sw, i, bsz, seq)
        o_c = _attention(q, k, v, bsz, seq)
        h = _channel(o_ab, o_c, h, p.reshape(DEPTH * t, PLE_DIM), sw, i, seq, token_major_out=(i == DEPTH - 1))
    return h.reshape(bsz, seq, d)
```

```python
import functools
import math

import numpy as np
import jax
import jax.numpy as jnp
from jax import lax
from jax.experimental import pallas as pl
from jax.experimental.pallas import tpu as pltpu

F32 = jnp.float32
BF16 = jnp.bfloat16

D_MODEL = 1024
DEPTH = 2
PLE_DIM = 256

A_WIDTH = 256
A_DK = 64
A_HEADS = 4
A_CHUNK = 16
B_WIDTH = 256
B_CH = 64
B_GROUPS = 4
B_CHUNK = 128
C_WIDTH = 512
C_NOPE = 64
C_ROPE = 32
C_V = 64
C_HEADS = 8
C_Q_RANK = 384
C_KV_RANK = 256
ROPE_THETA = 10000.0
LANES = 128
HEAD_PAD = LANES

D_FF = 2816
LN_EPS = 1e-5
RMS_EPS = 1e-6
DEEPNORM_ALPHA = (2 * DEPTH) ** 0.25
ATT_SCALE = (C_NOPE + C_ROPE) ** -0.5
Q_SCALE = ATT_SCALE * math.log2(math.e)

MIX_BLOCK = 256
MIX_NCHUNK = MIX_BLOCK // A_CHUNK
ATT_TQ = 512
ATT_TK = 256
ATT_HEADS = 8
ATT_ONES = 16
ROW_BLOCK = 1024
CH_BLOCK = 512
FF_CHUNK = 256
VMEM_LIMIT = 56 * 1024 * 1024
NEG_BIG = -1e30


def _dot(a, b):
    return jnp.dot(a, b, preferred_element_type=F32)


def _dot_nt(a, b):
    return lax.dot_general(a, b, (((1,), (1,)), ((), ())), preferred_element_type=F32)


def _dot_tn(a, b):
    return lax.dot_general(a, b, (((0,), (0,)), ((), ())), preferred_element_type=F32)


def _layer_norm(x, g, b):
    mu = jnp.mean(x, axis=-1, keepdims=True)
    xc = x - mu
    var = jnp.mean(xc * xc, axis=-1, keepdims=True)
    return xc * lax.rsqrt(var + LN_EPS) * g + b


def _rms_norm(x, g):
    return x * lax.rsqrt(jnp.mean(x * x, axis=-1, keepdims=True) + RMS_EPS) * g


def _sigmoid(x):
    return 1.0 / (1.0 + jnp.exp(-x))


def _const_spec(shape):
    nd = len(shape)
    return pl.BlockSpec(shape, lambda *_: (0,) * nd)


def _layer_spec(arr, layer, single_buffer=False):
    shape = arr.shape[1:]
    nd = len(shape)
    kw = dict(pipeline_mode=pl.Buffered(1)) if single_buffer else {}
    return pl.BlockSpec((None,) + shape, lambda *_: (layer,) + (0,) * nd, **kw)


def _params(sem):
    return pltpu.CompilerParams(dimension_semantics=sem, vmem_limit_bytes=VMEM_LIMIT)


def _rope_table_kernel(pos_ref, freq_ref, cos_ref, sin_ref, c_s, s_s):
    half = C_ROPE // 2
    g = pl.program_id(0)

    @pl.when(g == 0)
    def _():
        ang = pos_ref[...] * freq_ref[...]
        c_s[...] = jnp.cos(ang)
        s_s[...] = jnp.sin(ang)

    lane = lax.broadcasted_iota(jnp.int32, c_s.shape, 1)
    on_x1 = (lane >= C_NOPE) & (lane < C_NOPE + half)
    on_x2 = (lane >= C_NOPE + half) & (lane < C_NOPE + C_ROPE)
    to_x1 = (C_NOPE - g * half) % LANES
    to_x2 = (C_NOPE + half - g * half) % LANES
    c, s = c_s[...], s_s[...]
    nope = jnp.where(lane < C_NOPE, 1.0, 0.0)
    cos_ref[...] = jnp.where(on_x1, pltpu.roll(c, to_x1, axis=1), jnp.where(on_x2, pltpu.roll(c, to_x2, axis=1), nope))
    sin_ref[...] = jnp.where(on_x1, -pltpu.roll(s, to_x1, axis=1), jnp.where(on_x2, pltpu.roll(s, to_x2, axis=1), 0.0))


def _rope_tables(positions):
    t = positions.size
    half = C_ROPE // 2
    groups = LANES // half
    rows = t // groups
    inv_freq = ROPE_THETA ** (-jnp.arange(0, C_ROPE, 2, dtype=F32) / C_ROPE)
    pos_rep = jnp.repeat(positions.astype(F32).reshape(groups, rows).T, half, axis=1)
    freq = jnp.tile(inv_freq, groups).reshape(1, LANES)
    table = jax.ShapeDtypeStruct((t, HEAD_PAD), F32)
    out_spec = pl.BlockSpec((rows, HEAD_PAD), lambda g: (g, 0))
    return pl.pallas_call(
        _rope_table_kernel,
        out_shape=(table,) * 2,
        grid=(groups,),
        in_specs=[_const_spec((rows, LANES)), _const_spec((1, LANES))],
        out_specs=(out_spec,) * 2,
        scratch_shapes=[pltpu.VMEM((rows, LANES), F32), pltpu.VMEM((rows, LANES), F32)],
        compiler_params=_params(("arbitrary",)),
        name="rope_tables",
    )(pos_rep, freq)


def _read_pos_major(ref):
    return jnp.concatenate([ref[blk, :, p, :] for blk in range(ref.shape[0]) for p in range(A_CHUNK)], axis=0)


def _write_token_major(ref, val):
    for blk in range(ref.shape[0]):
        for p in range(A_CHUNK):
            r0 = blk * MIX_BLOCK + p * MIX_NCHUNK
            ref[blk, :, p, :] = val[r0:r0 + MIX_NCHUNK]


def _token_blocks(x2d):
    t, wd = x2d.shape
    return x2d.reshape(t // MIX_BLOCK, MIX_NCHUNK, A_CHUNK, wd)


def _token_block_spec(wd, rows=ROW_BLOCK, first_block=0):
    nblk = rows // MIX_BLOCK
    return pl.BlockSpec((nblk, MIX_NCHUNK, A_CHUNK, wd), lambda i: (first_block + i, 0, 0, 0))


def _rope_lanes(x, cos, sin):
    half = C_ROPE // 2
    lane = lax.broadcasted_iota(jnp.int32, x.shape, 1)
    right = pltpu.roll(x, HEAD_PAD - half, axis=1)
    left = pltpu.roll(x, half, axis=1)
    return x * cos + jnp.where(lane < C_NOPE + half, right, left) * sin


def _inproj_ln_kernel(x_ref, lng_ref, lnb_ref, *rest):
    *rest, h_out = rest
    h = _layer_norm(_read_pos_major(x_ref), lng_ref[...], lnb_ref[...])
    h_out[...] = h
    _inproj_body(h, *rest)


def _inproj_kernel(h_ref, *rest):
    _inproj_body(h_ref[...], *rest)


def _inproj_body(h, cos_ref, sin_ref, w_in,
                 gq_ref, wq_ref, gkv_ref, wk_ref, wv_ref,
                 hg_out, sg_out, q_out, k_out, v_out):
    hb = h.astype(BF16)
    o_sg, o_lat = 4 * A_WIDTH, 4 * A_WIDTH + 2 * B_WIDTH
    proj = _dot(hb, w_in[...])
    hg_out[...] = proj[:, 0:o_sg]
    sg_out[...] = proj[:, o_sg:o_lat]
    lat = proj[:, o_lat:]
    cos, sin = cos_ref[...], sin_ref[...]

    q_table = (cos + pltpu.roll(sin, C_ROPE, axis=1)) * Q_SCALE
    cqn = _rms_norm(lat[:, 0:C_Q_RANK], gq_ref[...]).astype(BF16)
    q_lin = _dot(cqn, wq_ref[...])
    for h in range(C_HEADS):
        sl = slice(h * HEAD_PAD, (h + 1) * HEAD_PAD)
        q_out[:, sl] = (q_lin[:, sl] * q_table).astype(BF16)

    ckvn = _rms_norm(lat[:, C_Q_RANK:C_Q_RANK + C_KV_RANK], gkv_ref[...]).astype(BF16)
    k_lin = _dot(ckvn, wk_ref[...])
    vt = _dot_nt(wv_ref[...], ckvn).astype(BF16)
    for c in range(v_out.shape[0]):
        v_out[c] = vt[:, c * ATT_TK:(c + 1) * ATT_TK]
    k_rope = _rope_lanes(lat[:, C_Q_RANK + C_KV_RANK:], cos, sin)
    k_rope = k_rope + pltpu.roll(k_rope, C_ROPE, axis=1)
    for h in range(C_HEADS):
        sl = slice(h * HEAD_PAD, (h + 1) * HEAD_PAD)
        k_out[:, sl] = (k_lin[:, sl] + k_rope).astype(BF16)


def _in_proj(h, rope_pats, sw, layer, ln=None):
    t = h.shape[0]
    tm = ROW_BLOCK
    row = lambda w: pl.BlockSpec((tm, w), lambda i: (i, 0))
    weights = [sw[n] for n in ("w_in", "gq", "wq", "gkv", "wk", "wv")]
    hp = C_HEADS * HEAD_PAD
    out_shape = [jax.ShapeDtypeStruct((t, 4 * A_WIDTH), F32),
                 jax.ShapeDtypeStruct((t, 2 * B_WIDTH), F32),
                 jax.ShapeDtypeStruct((t, hp), BF16),
                 jax.ShapeDtypeStruct((t, hp), BF16),
                 jax.ShapeDtypeStruct((t // ATT_TK, C_WIDTH, ATT_TK), BF16)]
    out_specs = [row(4 * A_WIDTH), row(2 * B_WIDTH), row(hp), row(hp),
                 pl.BlockSpec((tm // ATT_TK, C_WIDTH, ATT_TK), lambda i: (i, 0, 0))]
    common_specs = [row(HEAD_PAD)] * 2 + [_layer_spec(w, layer) for w in weights]
    if ln is None:
        body, first_specs, first_args = _inproj_kernel, [row(D_MODEL)], [h]
    else:
        body = _inproj_ln_kernel
        first_specs = [_token_block_spec(D_MODEL), _const_spec((1, D_MODEL)), _const_spec((1, D_MODEL))]
        first_args = [_token_blocks(h), ln[0].reshape(1, D_MODEL), ln[1].reshape(1, D_MODEL)]
        out_shape.append(jax.ShapeDtypeStruct((t, D_MODEL), F32))
        out_specs.append(row(D_MODEL))
    return pl.pallas_call(
        body,
        out_shape=tuple(out_shape),
        grid=(t // tm,),
        in_specs=first_specs + common_specs,
        out_specs=tuple(out_specs),
        compiler_params=_params(("parallel",)),
        name="in_proj",
    )(*first_args, *rope_pats, *weights)


def _erf(x):
    return lax.erf(x)


def _gelu(x):
    return 0.5 * x * (1.0 + _erf(x * (2.0 ** -0.5)))


def _rows_to_array(rows, like):
    zero = jnp.zeros_like(like)
    return jnp.concatenate([zero if r is None else r for r in rows], axis=0)


def _mixer_kernel(hg_ref, sg_ref, lb_ref, ng_ref, lng_ref, lnb_ref, ws_ref, bs_ref, ones_ref, gmask_ref,
                  o_ref,
                  q_s, k_s, f_s, v_s, qt_s, kt_s, oi_s, a_s, b_s, qs_s, ks_s, state_s):
    nc, cs, w = MIX_NCHUNK, A_CHUNK, A_WIDTH
    tb = nc * cs
    slab = lambda p: slice(p * nc, (p + 1) * nc)

    @pl.when(pl.program_id(1) == 0)
    def _():
        state_s[...] = jnp.zeros_like(state_s)

    lb = lb_ref[...]
    for p in range(cs):
        aq = hg_ref[slab(p), 0:w]
        f_p = lb + (1.0 - lb) * _sigmoid(hg_ref[slab(p), w:2 * w])
        q_s[p] = aq * _sigmoid(aq)
        f_s[p] = f_p
        k_s[p] = 1.0 - f_p
        v_s[p] = hg_ref[slab(p), 2 * w:3 * w]

    ones_bd = ones_ref[...]

    kd = []
    for p in range(cs):
        f_p = f_s[p]
        q_p = q_s[p]
        kd = [k_s[p]] + [f_p * x for x in kd]
        xs = jnp.concatenate([q_p * x for x in kd], axis=0).astype(BF16)
        wgt = _dot(xs, ones_bd)
        o_p = wgt[0:nc] * v_s[p]
        for d in range(1, p + 1):
            o_p = o_p + wgt[d * nc:(d + 1) * nc] * v_s[p - d]
        oi_s[p] = o_p

    pref = f_s[0]
    qt_s[0] = q_s[0] * pref
    for p in range(1, cs):
        pref = pref * f_s[p]
        qt_s[p] = q_s[p] * pref
    chunk_decay = pref
    suf = jnp.ones_like(pref)
    kt_s[cs - 1] = k_s[cs - 1]
    for p in range(cs - 2, -1, -1):
        suf = suf * f_s[p + 1]
        kt_s[p] = k_s[p] * suf

    drow = [chunk_decay[c:c + 1, :] for c in range(nc)]
    one_row = jnp.ones_like(drow[0])
    levels = []
    m = 1
    while m < nc:
        levels.append(m)
        m *= 2
    la_arrays, lb_arrays = [], []
    for m in levels:
        la = [None] * nc
        lbr = [None] * nc
        for base in range(0, nc, 2 * m):
            mid = base + m
            la[mid] = one_row
            for i in range(mid + 1, mid + m):
                la[i] = la[i - 1] * drow[i - 1]
            lbr[mid - 1] = one_row
            for j in range(mid - 2, base - 1, -1):
                lbr[j] = lbr[j + 1] * drow[j + 1]
        la_arrays.append(_rows_to_array(la, one_row))
        lb_arrays.append(_rows_to_array(lbr, one_row))
    ep = [one_row]
    for c in range(1, nc):
        ep.append(ep[-1] * drow[c - 1])
    es = [one_row] * nc
    for c in range(nc - 2, -1, -1):
        es[c] = es[c + 1] * drow[c + 1]
    total_decay = ep[-1] * drow[nc - 1]
    ep_arr = jnp.concatenate(ep, axis=0)
    es_arr = jnp.concatenate(es, axis=0)

    for p in range(cs):
        qt_p = qt_s[p]
        kt_p = kt_s[p]
        for li in range(len(levels)):
            a_s[li, p] = qt_p * la_arrays[li]
            b_s[li, p] = kt_p * lb_arrays[li]
        qs_s[p] = qt_p * ep_arr
        ks_s[p] = kt_p * es_arr

    lane = lax.broadcasted_iota(jnp.int32, (tb, w), 1)
    head_masks = [(lane >= h * A_DK) & (lane < (h + 1) * A_DK) for h in range(A_HEADS)]
    v_tok = v_s[...].reshape(tb, w)
    v_bf = v_tok.astype(BF16)
    a_tok = [a_s[li].reshape(tb, w).astype(BF16) for li in range(len(levels))]
    b_tok = [b_s[li].reshape(tb, w).astype(BF16) for li in range(len(levels))]
    scores, v_heads = [], []
    for h in range(A_HEADS):
        hs = slice(h * A_DK, (h + 1) * A_DK)
        sc = None
        for li, m in enumerate(levels):
            s_l = _dot_nt(a_tok[li][:, hs], b_tok[li][:, hs])
            if 2 * m < nc:
                s_l = s_l * gmask_ref[li]
            sc = s_l if sc is None else sc + s_l
        scores.append(sc.astype(BF16))
        v_heads.append(jnp.where(head_masks[h], v_tok, 0.0).astype(BF16))
    o_cross = _dot(jnp.concatenate(scores, axis=1), jnp.concatenate(v_heads, axis=0))

    st = state_s[...]
    o_state = _dot_nt(qs_s[...].reshape(tb, w).astype(BF16), st.astype(BF16))
    kv = _dot_tn(v_bf, ks_s[...].reshape(tb, w).astype(BF16))
    rr = lax.broadcasted_iota(jnp.int32, (w, w), 0)
    cc = lax.broadcasted_iota(jnp.int32, (w, w), 1)
    state_s[...] = st * total_decay + jnp.where((rr // A_DK) == (cc // A_DK), kv, 0.0)

    o = oi_s[...].reshape(tb, w) + o_cross + o_state
    ms = _dot((o * o).astype(BF16), ones_bd) * (1.0 / A_DK)
    ag = hg_ref[:, 3 * w:4 * w]
    o_ref[:, 0:w] = (o * lax.rsqrt(ms + RMS_EPS) * ng_ref[...] * (ag * _sigmoid(ag))).astype(o_ref.dtype)

    cps = B_CHUNK // cs
    u = _gelu(sg_ref[:, 0:B_WIDTH])
    vv = _layer_norm(_gelu(sg_ref[:, B_WIDTH:2 * B_WIDTH]), lng_ref[...], lnb_ref[...])
    lane_b = lax.broadcasted_iota(jnp.int32, (B_CHUNK, B_WIDTH), 1)
    zs = []
    for ci in range(nc // cps):
        v_c = jnp.concatenate([vv[p * nc + ci * cps:p * nc + (ci + 1) * cps] for p in range(cs)],
                              axis=0).astype(BF16)
        v_groups = jnp.concatenate(
            [jnp.where((lane_b >= g * B_CH) & (lane_b < (g + 1) * B_CH), v_c, 0.0) for g in range(B_GROUPS)], axis=0)
        zs.append(bs_ref[...] + _dot(ws_ref[...], v_groups))
    z_all = jnp.concatenate([z[p * cps:(p + 1) * cps] for p in range(cs) for z in zs], axis=0)
    o_ref[:, w:w + B_WIDTH] = (u * z_all).astype(o_ref.dtype)


def _group_masks():
    chunk = np.arange(MIX_BLOCK) % MIX_NCHUNK
    levels = int(round(math.log2(MIX_NCHUNK))) - 1
    return np.stack([(chunk[:, None] >> (l + 1)) == (chunk[None, :] >> (l + 1)) for l in range(levels)]
                    ).astype(np.float32)


def _mixers(hg, sg, sw, layer, bsz, seq):
    nb = seq // MIX_BLOCK
    nc, cs, w = MIX_NCHUNK, A_CHUNK, A_WIDTH
    nlev = int(round(math.log2(nc)))
    blk = lambda width: pl.BlockSpec((MIX_BLOCK, width), lambda b, i: (b * nb + i, 0))
    pm = lambda: pltpu.VMEM((cs, nc, w), F32)
    ow = A_WIDTH + B_WIDTH
    return pl.pallas_call(
        _mixer_kernel,
        out_shape=jax.ShapeDtypeStruct((bsz * seq, ow), BF16),
        grid=(bsz, nb),
        in_specs=[blk(4 * A_WIDTH), blk(2 * B_WIDTH),
                  ] + [_layer_spec(sw[n], layer) for n in ("lb", "hgrn_g", "sgu_g", "sgu_b", "ws", "bs")] + [
                  _const_spec((w, w)), _const_spec((nlev - 1, MIX_BLOCK, MIX_BLOCK))],
        out_specs=blk(ow),
        scratch_shapes=[pm(), pm(), pm(), pm(), pm(), pm(), pm(),
                        pltpu.VMEM((nlev, cs, nc, w), F32), pltpu.VMEM((nlev, cs, nc, w), F32),
                        pm(), pm(), pltpu.VMEM((w, w), F32)],
        compiler_params=_params(("parallel", "arbitrary")),
        name="mixers",
    )(hg, sg, sw["lb"], sw["hgrn_g"], sw["sgu_g"], sw["sgu_b"], sw["ws"], sw["bs"], sw["ones_bd"], _group_masks())


def _token_offset(i):
    r = i % MIX_BLOCK
    return (i - r) + (r % MIX_NCHUNK) * A_CHUNK + r // MIX_NCHUNK


def _diagonal_bias():
    key = _token_offset(np.arange(ATT_TK))[:, None]
    qry = _token_offset(np.arange(ATT_TQ))[None, :]
    return np.where(key <= qry, 0.0, NEG_BIG).astype(np.float32)


def _attn_kernel(qx_ref, qy_ref, k_ref, vt_ref, bias_ref, o_ref, sa_ref, sb_ref, acc_ref):
    tq, tk = ATT_TQ, ATT_TK
    ones_rows = jnp.ones((ATT_ONES, tk), BF16)
    late = slice(tq - tk, tq)

    def produce(qs, hh, j, dst, bias=None, cols=slice(None)):
        off = pl.multiple_of(j * tk, tk)
        k_t = k_ref[pl.ds(off, tk), hh * HEAD_PAD:(hh + 1) * HEAD_PAD]
        st = _dot_nt(k_t, qs[hh][cols])
        if bias is not None:
            st = st + bias
        dst[hh, :, cols] = st
        return jnp.max(st, axis=0, keepdims=True)

    def consume(hh, j, src, tile_max, acc, m_old, cols=slice(None)):
        m_new = jnp.maximum(m_old, tile_max)
        alpha = jnp.exp2(m_old - m_new)
        pt = jnp.exp2(src[hh, :, cols] - m_new).astype(BF16)
        vt_h = jnp.concatenate([vt_ref[j, hh * C_V:(hh + 1) * C_V, :], ones_rows], axis=0)
        acc[hh, :, cols] = alpha * acc[hh, :, cols] + _dot(vt_h, pt)
        return m_new

    def overlap(qs, prod, cons):
        maxes, ms = [], []
        for hh in range(ATT_HEADS):
            if prod is not None:
                maxes.append(produce(qs, hh, *prod))
            if cons is not None:
                j, src, tile_max, acc, m_old = cons
                ms.append(consume(hh, j, src, tile_max[hh], acc, m_old[hh]))
        return tuple(maxes), tuple(ms)

    def clear(acc_a, acc_b):
        acc_a[...] = jnp.zeros(acc_a.shape, F32)
        acc_b[...] = jnp.zeros(acc_b.shape, F32)

    def steady(qi, qs, acc_a, acc_b, max_a):
        ms0 = tuple(jnp.full((1, tq), NEG_BIG, F32) for _ in range(ATT_HEADS))

        def body(p, carry):
            ms_a, ms_b, in_a, max_a = carry
            max_b, ms_a = overlap(qs, (2 * p, sb_ref), (in_a, sa_ref, max_a, acc_a, ms_a))
            max_a, ms_b = overlap(qs, (2 * p + 1, sa_ref), (2 * p, sb_ref, max_b, acc_b, ms_b))
            return ms_a, ms_b, 2 * p + 1, max_a

        return lax.fori_loop(0, qi, body, (ms0, ms0, 2 * qi, max_a))

    def drain_a(qi, qs, acc_a, state):
        ms_a, ms_b, in_a, max_a = state
        max_b, ms_a = overlap(qs, (2 * qi + 1, sb_ref, bias_ref[:, 0:tk], late),
                              (in_a, sa_ref, max_a, acc_a, ms_a))
        return ms_a, ms_b, max_b

    def drain_b(qi, acc_a, acc_b, ms_a, ms_b, max_b, out, next_block=None):
        next_max, ms_late = [], []
        for hh in range(ATT_HEADS):
            if next_block is not None:
                next_max.append(produce(next_block[1], hh, 2 * next_block[0], sa_ref, bias_ref[...]))
            ms_late.append(consume(hh, 2 * qi + 1, sb_ref, max_b[hh], acc_b, ms_b[hh][:, late], late))
        ms_b = tuple(jnp.concatenate([ms_b[hh][:, 0:tq - tk], ms_late[hh]], axis=1) for hh in range(ATT_HEADS))
        for hh in range(ATT_HEADS):
            m = jnp.maximum(ms_a[hh], ms_b[hh])
            acc = acc_a[hh] * jnp.exp2(ms_a[hh] - m) + acc_b[hh] * jnp.exp2(ms_b[hh] - m)
            out[hh * C_V:(hh + 1) * C_V, :] = (acc[0:C_V] / acc[C_V:C_V + 1]).astype(out.dtype)
        return tuple(next_max)

    i = pl.program_id(2)
    qi_x, qi_y = i, 2 * pl.num_programs(2) - 1 - i
    qs_x = [qx_ref[:, hh * HEAD_PAD:(hh + 1) * HEAD_PAD] for hh in range(ATT_HEADS)]
    qs_y = [qy_ref[:, hh * HEAD_PAD:(hh + 1) * HEAD_PAD] for hh in range(ATT_HEADS)]
    acc_x, acc_y = (acc_ref.at[0, 0], acc_ref.at[0, 1]), (acc_ref.at[1, 0], acc_ref.at[1, 1])

    clear(*acc_x)
    clear(*acc_y)
    max_a_x, _ = overlap(qs_x, (2 * qi_x, sa_ref, bias_ref[...]), None)
    state_x = steady(qi_x, qs_x, *acc_x, max_a_x)
    ms_a, ms_b, max_b = drain_a(qi_x, qs_x, acc_x[0], state_x)
    max_a_y = drain_b(qi_x, *acc_x, ms_a, ms_b, max_b, o_ref.at[0], next_block=(qi_y, qs_y))
    state_y = steady(qi_y, qs_y, *acc_y, max_a_y)
    ms_a, ms_b, max_b = drain_a(qi_y, qs_y, acc_y[0], state_y)
    drain_b(qi_y, *acc_y, ms_a, ms_b, max_b, o_ref.at[1])


def _attention(q, k, vt, bsz, seq):
    assert ATT_TQ == 2 * ATT_TK
    nh = ATT_HEADS
    nq = seq // ATT_TQ
    nk = seq // ATT_TK
    acc_shape = (2, 2, nh, C_V + ATT_ONES, ATT_TQ)
    return pl.pallas_call(
        _attn_kernel,
        out_shape=jax.ShapeDtypeStruct((bsz, 2, nq // 2, C_WIDTH, ATT_TQ), BF16),
        grid=(bsz, C_HEADS // nh, nq // 2),
        in_specs=[pl.BlockSpec((None, ATT_TQ, nh * HEAD_PAD), lambda b, g, i: (b, i, g)),
                  pl.BlockSpec((None, ATT_TQ, nh * HEAD_PAD), lambda b, g, i: (b, nq - 1 - i, g)),
                  pl.BlockSpec((None, seq, nh * HEAD_PAD), lambda b, g, i: (b, 0, g)),
                  pl.BlockSpec((nk, nh * C_V, ATT_TK), lambda b, g, i: (b, g, 0)),
                  _const_spec((ATT_TK, ATT_TQ))],
        out_specs=pl.BlockSpec((None, 2, None, nh * C_V, ATT_TQ), lambda b, g, i: (b, 0, i, g, 0)),
        scratch_shapes=[pltpu.VMEM((nh, ATT_TK, ATT_TQ), F32), pltpu.VMEM((nh, ATT_TK, ATT_TQ), F32),
                        pltpu.VMEM(acc_shape, F32)],
        compiler_params=_params(("parallel", "parallel", "arbitrary")),
        name="mla_attention",
    )(q.reshape(bsz, seq, -1), q.reshape(bsz, seq, -1), k.reshape(bsz, seq, -1), vt, _diagonal_bias())


def _attn_out_spec(seq):
    assert CH_BLOCK == ATT_TQ
    nq = seq // ATT_TQ

    def index(r):
        qb = r % nq
        return r // nq, qb // (nq // 2), jnp.where(qb < nq // 2, qb, nq - 1 - qb), 0, 0

    return pl.BlockSpec((None, None, 1, C_WIDTH, ATT_TQ), index)


def _channel_kernel(oab_ref, oct_ref, h_ref, p_ref, wab_ref, wc_ref, g1_ref, b1_ref,
                    wgu_ref, wd_ref, wpg_ref, wpp_ref, g2_ref, b2_ref, o_ref, act_s, *,
                    token_major_out):
    mix_c = jnp.concatenate([_dot_tn(oct_ref[i], wc_ref[...]) for i in range(oct_ref.shape[0])], axis=0)
    mix = _dot(oab_ref[...], wab_ref[...]) + mix_c
    h = _layer_norm(DEEPNORM_ALPHA * h_ref[...] + mix, g1_ref[...], b1_ref[...])
    hb = h.astype(BF16)
    for c in range(D_FF // FF_CHUNK):
        cols = slice(c * FF_CHUNK, (c + 1) * FF_CHUNK)
        gate = _dot(hb, wgu_ref[:, cols])
        up = _dot(hb, wgu_ref[:, D_FF + c * FF_CHUNK:D_FF + (c + 1) * FF_CHUNK])
        act_s[:, cols] = (gate * _sigmoid(gate) * up).astype(BF16)
    ffn = _dot(act_s[...], wd_ref[...])
    ple = _sigmoid(_dot(hb, wpg_ref[...])) * _dot(_read_pos_major(p_ref).astype(BF16), wpp_ref[...])
    out = _layer_norm(DEEPNORM_ALPHA * h + ffn + ple, g2_ref[...], b2_ref[...])
    if token_major_out:
        _write_token_major(o_ref, out)
    else:
        o_ref[...] = out


def _channel(o_ab, o_ct, h, p, sw, layer, seq, token_major_out):
    t = h.shape[0]
    tm = CH_BLOCK
    row = lambda w: pl.BlockSpec((tm, w), lambda i: (i, 0))
    if token_major_out:
        out_shape = jax.ShapeDtypeStruct((t // MIX_BLOCK, MIX_NCHUNK, A_CHUNK, D_MODEL), F32)
        out_spec = _token_block_spec(D_MODEL, tm)
    else:
        out_shape = jax.ShapeDtypeStruct((t, D_MODEL), F32)
        out_spec = row(D_MODEL)
    names = ("w_out_ab", "w_out_c", "ln1_g", "ln1_b", "wgu", "wd", "wpg", "wpp", "ln2_g", "ln2_b")
    return pl.pallas_call(
        functools.partial(_channel_kernel, token_major_out=token_major_out),
        out_shape=out_shape,
        grid=(t // tm,),
        in_specs=[row(A_WIDTH + B_WIDTH), _attn_out_spec(seq), row(D_MODEL),
                  _token_block_spec(PLE_DIM, tm, layer * (t // tm))] + [_layer_spec(sw[n], layer, single_buffer=True) for n in names],
        out_specs=out_spec,
        scratch_shapes=[pltpu.VMEM((tm, D_FF), BF16)],
        compiler_params=_params(("parallel",)),
        name="channel_mix",
    )(o_ab, o_ct, h, _token_blocks(p), *[sw[n] for n in names])


def _head_pad_cols(w_nope, w_rope):
    lead = w_nope.shape[:2]
    pad = jnp.zeros(lead + (C_HEADS, HEAD_PAD - C_NOPE - w_rope.shape[-1]), w_nope.dtype)
    return jnp.concatenate([w_nope, w_rope, pad], axis=-1).reshape(lead + (C_HEADS * HEAD_PAD,))


def _pos_major(x, axis):
    cps = B_CHUNK // A_CHUNK
    shape = x.shape
    x = x.reshape(shape[:axis] + (cps, A_CHUNK) + shape[axis + 1:])
    return jnp.swapaxes(x, axis, axis + 1).reshape(shape)


def _prep_weights(lower_bounds, w_in, hgrn_norm_g, sgu_ln_g, sgu_ln_b, sgu_w_s, sgu_b_s,
                  mla_q_norm_g, mla_w_uq, mla_kv_norm_g, mla_w_ukv, w_out, ln1_g, ln1_b,
                  w_gate_up, w_down, ple_w_gate, ple_w_proj, ln2_g, ln2_b):
    nl = w_in.shape[0]
    o_kr = 4 * A_WIDTH + 2 * B_WIDTH + C_Q_RANK + C_KV_RANK
    w_in_bf = w_in.astype(BF16)
    w_in_placed = jnp.concatenate(
        [w_in_bf[:, :, :o_kr], jnp.zeros((nl, D_MODEL, C_NOPE), BF16), w_in_bf[:, :, o_kr:],
         jnp.zeros((nl, D_MODEL, HEAD_PAD - C_NOPE - C_ROPE), BF16)], axis=-1)
    wq = mla_w_uq.reshape(nl, C_Q_RANK, C_HEADS, C_NOPE + C_ROPE)
    wq_nope, wq_rope = wq[..., :C_NOPE], wq[..., C_NOPE:]
    wkv = mla_w_ukv.reshape(nl, C_KV_RANK, C_HEADS, C_NOPE + C_V)
    wk_nope, wv = wkv[..., :C_NOPE], wkv[..., C_NOPE:]
    zeros_rope = jnp.zeros((nl, C_KV_RANK, C_HEADS, C_ROPE), F32)
    tri = jnp.tril(jnp.ones((B_CHUNK, B_CHUNK), F32))
    head_id = np.arange(A_WIDTH) // A_DK
    vec = lambda g: g.reshape(nl, 1, -1)
    return dict(
        w_in=w_in_placed,
        gq=vec(mla_q_norm_g), gkv=vec(mla_kv_norm_g),
        wq=_head_pad_cols(wq_nope, jnp.concatenate(
            [wq_rope, wq_rope[..., C_ROPE // 2:], wq_rope[..., :C_ROPE // 2]], axis=-1)).astype(BF16),
        wk=_head_pad_cols(wk_nope, zeros_rope).astype(BF16),
        wv=jnp.swapaxes(wv.reshape(nl, C_KV_RANK, C_WIDTH), 1, 2).astype(BF16),
        lb=vec(lower_bounds), hgrn_g=vec(hgrn_norm_g), sgu_g=vec(sgu_ln_g), sgu_b=vec(sgu_ln_b),
        ws=jnp.swapaxes(_pos_major(_pos_major(sgu_w_s * tri, 2), 3), 1, 2).reshape(
            nl, B_CHUNK, B_GROUPS * B_CHUNK).astype(BF16),
        bs=_pos_major(jnp.repeat(jnp.swapaxes(sgu_b_s, 1, 2), B_CH, axis=2), 1),
        ones_bd=jnp.asarray(head_id[:, None] == head_id[None, :], BF16),
        w_out_ab=w_out[:, :A_WIDTH + B_WIDTH].astype(BF16), w_out_c=w_out[:, A_WIDTH + B_WIDTH:].astype(BF16),
        ln1_g=vec(ln1_g), ln1_b=vec(ln1_b),
        wgu=w_gate_up.astype(BF16),
        wd=w_down.astype(BF16), wpg=ple_w_gate.astype(BF16), wpp=ple_w_proj.astype(BF16),
        ln2_g=vec(ln2_g), ln2_b=vec(ln2_b),
    )


def kernel(x, p, positions, ln_in_g, ln_in_b, w_in, hgrn_lb_logits, hgrn_norm_g, sgu_ln_g, sgu_ln_b, sgu_w_s, sgu_b_s, mla_q_norm_g, mla_w_uq, mla_kv_norm_g, mla_w_ukv, w_out, ln1_g, ln1_b, w_gate_up, w_down, ple_w_gate, ple_w_proj, ln2_g, ln2_b):
    bsz, seq, d = x.shape
    t = bsz * seq
    lb_cum = jnp.cumsum(jax.nn.softmax(hgrn_lb_logits.astype(F32), axis=0), axis=0)
    lower_bounds = lb_cum - lb_cum[0]
    pos_pm = positions.reshape(bsz, seq // MIX_BLOCK, MIX_NCHUNK, A_CHUNK).swapaxes(2, 3).reshape(bsz, seq)
    rope_pats = _rope_tables(pos_pm)
    sw = _prep_weights(lower_bounds, w_in, hgrn_norm_g, sgu_ln_g, sgu_ln_b, sgu_w_s, sgu_b_s,
                       mla_q_norm_g, mla_w_uq, mla_kv_norm_g, mla_w_ukv, w_out, ln1_g, ln1_b,
                       w_gate_up, w_down, ple_w_gate, ple_w_proj, ln2_g, ln2_b)

    h = x.reshape(t, d)
    for i in range(DEPTH):
        if i == 0:
            hg, sg, q, k, v, h = _in_proj(h, rope_pats, sw, i, ln=(ln_in_g, ln_in_b))
        else:
            hg, sg, q, k, v = _in_proj(h, rope_pats, sw, i)
        o_ab = _mixers(hg, sg, sw, i, bsz, seq)
        o_c = _attention(q, k, v, bsz, seq)
        h = _channel(o_ab, o_c, h, p.reshape(DEPTH * t, PLE_DIM), sw, i, seq, token_major_out=(i == DEPTH - 1))
    return h.reshape(bsz, seq, d)
```

```python
import functools
import math

import numpy as np
import jax
import jax.numpy as jnp
from jax import lax
from jax.experimental import pallas as pl
from jax.experimental.pallas import tpu as pltpu

F32 = jnp.float32
BF16 = jnp.bfloat16

D_MODEL = 1024
DEPTH = 2
PLE_DIM = 256

A_WIDTH = 256
A_DK = 64
A_HEADS = 4
A_CHUNK = 16
B_WIDTH = 256
B_CH = 64
B_GROUPS = 4
B_CHUNK = 128
C_WIDTH = 512
C_NOPE = 64
C_ROPE = 32
C_V = 64
C_HEADS = 8
C_Q_RANK = 384
C_KV_RANK = 256
ROPE_THETA = 10000.0
LANES = 128
HEAD_PAD = LANES

D_FF = 2816
LN_EPS = 1e-5
RMS_EPS = 1e-6
DEEPNORM_ALPHA = (2 * DEPTH) ** 0.25
ATT_SCALE = (C_NOPE + C_ROPE) ** -0.5
Q_SCALE = ATT_SCALE * math.log2(math.e)

MIX_BLOCK = 256
MIX_NCHUNK = MIX_BLOCK // A_CHUNK
ATT_TQ = 512
ATT_TK = 256
ATT_HEADS = 8
ATT_ONES = 16
ROW_BLOCK = 1024
CH_BLOCK = 512
FF_CHUNK = 256
VMEM_LIMIT = 56 * 1024 * 1024
NEG_BIG = -1e30


def _dot(a, b):
    return jnp.dot(a, b, preferred_element_type=F32)


def _dot_nt(a, b):
    return lax.dot_general(a, b, (((1,), (1,)), ((), ())), preferred_element_type=F32)


def _dot_tn(a, b):
    return lax.dot_general(a, b, (((0,), (0,)), ((), ())), preferred_element_type=F32)


def _layer_norm(x, g, b):
    mu = jnp.mean(x, axis=-1, keepdims=True)
    xc = x - mu
    var = jnp.mean(xc * xc, axis=-1, keepdims=True)
    return xc * lax.rsqrt(var + LN_EPS) * g + b


def _rms_norm(x, g):
    return x * lax.rsqrt(jnp.mean(x * x, axis=-1, keepdims=True) + RMS_EPS) * g


def _sigmoid(x):
    return 1.0 / (1.0 + jnp.exp(-x))


def _const_spec(shape):
    nd = len(shape)
    return pl.BlockSpec(shape, lambda *_: (0,) * nd)


def _layer_spec(arr, layer, single_buffer=False):
    shape = arr.shape[1:]
    nd = len(shape)
    kw = dict(pipeline_mode=pl.Buffered(1)) if single_buffer else {}
    return pl.BlockSpec((None,) + shape, lambda *_: (layer,) + (0,) * nd, **kw)


def _params(sem):
    return pltpu.CompilerParams(dimension_semantics=sem, vmem_limit_bytes=VMEM_LIMIT)


def _rope_table_kernel(pos_ref, freq_ref, cos_ref, sin_ref, c_s, s_s):
    half = C_ROPE // 2
    g = pl.program_id(0)

    @pl.when(g == 0)
    def _():
        ang = pos_ref[...] * freq_ref[...]
        c_s[...] = jnp.cos(ang)
        s_s[...] = jnp.sin(ang)

    lane = lax.broadcasted_iota(jnp.int32, c_s.shape, 1)
    on_x1 = (lane >= C_NOPE) & (lane < C_NOPE + half)
    on_x2 = (lane >= C_NOPE + half) & (lane < C_NOPE + C_ROPE)
    to_x1 = (C_NOPE - g * half) % LANES
    to_x2 = (C_NOPE + half - g * half) % LANES
    c, s = c_s[...], s_s[...]
    nope = jnp.where(lane < C_NOPE, 1.0, 0.0)
    cos_ref[...] = jnp.where(on_x1, pltpu.roll(c, to_x1, axis=1), jnp.where(on_x2, pltpu.roll(c, to_x2, axis=1), nope))
    sin_ref[...] = jnp.where(on_x1, -pltpu.roll(s, to_x1, axis=1), jnp.where(on_x2, pltpu.roll(s, to_x2, axis=1), 0.0))


def _rope_tables(positions):
    t = positions.size
    half = C_ROPE // 2
    groups = LANES // half
    rows = t // groups
    inv_freq = ROPE_THETA ** (-jnp.arange(0, C_ROPE, 2, dtype=F32) / C_ROPE)
    pos_rep = jnp.repeat(positions.astype(F32).reshape(groups, rows).T, half, axis=1)
    freq = jnp.tile(inv_freq, groups).reshape(1, LANES)
    table = jax.ShapeDtypeStruct((t, HEAD_PAD), F32)
    out_spec = pl.BlockSpec((rows, HEAD_PAD), lambda g: (g, 0))
    return pl.pallas_call(
        _rope_table_kernel,
        out_shape=(table,) * 2,
        grid=(groups,),
        in_specs=[_const_spec((rows, LANES)), _const_spec((1, LANES))],
        out_specs=(out_spec,) * 2,
        scratch_shapes=[pltpu.VMEM((rows, LANES), F32), pltpu.VMEM((rows, LANES), F32)],
        compiler_params=_params(("arbitrary",)),
        name="rope_tables",
    )(pos_rep, freq)


def _read_pos_major(ref):
    return jnp.concatenate([ref[blk, :, p, :] for blk in range(ref.shape[0]) for p in range(A_CHUNK)], axis=0)


def _write_token_major(ref, val):
    for blk in range(ref.shape[0]):
        for p in range(A_CHUNK):
            r0 = blk * MIX_BLOCK + p * MIX_NCHUNK
            ref[blk, :, p, :] = val[r0:r0 + MIX_NCHUNK]


def _token_blocks(x2d):
    t, wd = x2d.shape
    return x2d.reshape(t // MIX_BLOCK, MIX_NCHUNK, A_CHUNK, wd)


def _token_block_spec(wd, rows=ROW_BLOCK, first_block=0):
    nblk = rows // MIX_BLOCK
    return pl.BlockSpec((nblk, MIX_NCHUNK, A_CHUNK, wd), lambda i: (first_block + i, 0, 0, 0))


def _rope_lanes(x, cos, sin):
    half = C_ROPE // 2
    lane = lax.broadcasted_iota(jnp.int32, x.shape, 1)
    right = pltpu.roll(x, HEAD_PAD - half, axis=1)
    left = pltpu.roll(x, half, axis=1)
    return x * cos + jnp.where(lane < C_NOPE + half, right, left) * sin


def _inproj_ln_kernel(x_ref, lng_ref, lnb_ref, *rest):
    *rest, h_out = rest
    h = _layer_norm(_read_pos_major(x_ref), lng_ref[...], lnb_ref[...])
    h_out[...] = h
    _inproj_body(h, *rest)


def _inproj_kernel(h_ref, *rest):
    _inproj_body(h_ref[...], *rest)


def _inproj_body(h, cos_ref, sin_ref, w_in,
                 gq_ref, wq_ref, gkv_ref, wk_ref, wv_ref,
                 hg_out, sg_out, q_out, k_out, v_out):
    hb = h.astype(BF16)
    o_sg, o_lat = 4 * A_WIDTH, 4 * A_WIDTH + 2 * B_WIDTH
    proj = _dot(hb, w_in[...])
    hg_out[...] = proj[:, 0:o_sg]
    sg_out[...] = proj[:, o_sg:o_lat]
    lat = proj[:, o_lat:]
    cos, sin = cos_ref[...], sin_ref[...]

    q_table = (cos + pltpu.roll(sin, C_ROPE, axis=1)) * Q_SCALE
    cqn = _rms_norm(lat[:, 0:C_Q_RANK], gq_ref[...]).astype(BF16)
    q_lin = _dot(cqn, wq_ref[...])
    for h in range(C_HEADS):
        sl = slice(h * HEAD_PAD, (h + 1) * HEAD_PAD)
        q_out[:, sl] = (q_lin[:, sl] * q_table).astype(BF16)

    ckvn = _rms_norm(lat[:, C_Q_RANK:C_Q_RANK + C_KV_RANK], gkv_ref[...]).astype(BF16)
    k_lin = _dot(ckvn, wk_ref[...])
    vt = _dot_nt(wv_ref[...], ckvn).astype(BF16)
    for c in range(v_out.shape[0]):
        v_out[c] = vt[:, c * ATT_TK:(c + 1) * ATT_TK]
    k_raw = pltpu.roll(lat[:, C_Q_RANK + C_KV_RANK:], C_NOPE, axis=1)
    k_rope = _rope_lanes(k_raw, cos, sin)
    k_rope = k_rope + pltpu.roll(k_rope, C_ROPE, axis=1)
    for h in range(C_HEADS):
        sl = slice(h * HEAD_PAD, (h + 1) * HEAD_PAD)
        k_out[:, sl] = (k_lin[:, sl] + k_rope).astype(BF16)


def _in_proj(h, rope_pats, sw, layer, ln=None):
    t = h.shape[0]
    tm = ROW_BLOCK
    row = lambda w: pl.BlockSpec((tm, w), lambda i: (i, 0))
    weights = [sw[n] for n in ("w_in", "gq", "wq", "gkv", "wk", "wv")]
    hp = C_HEADS * HEAD_PAD
    out_shape = [jax.ShapeDtypeStruct((t, 4 * A_WIDTH), F32),
                 jax.ShapeDtypeStruct((t, 2 * B_WIDTH), F32),
                 jax.ShapeDtypeStruct((t, hp), BF16),
                 jax.ShapeDtypeStruct((t, hp), BF16),
                 jax.ShapeDtypeStruct((t // ATT_TK, C_WIDTH, ATT_TK), BF16)]
    out_specs = [row(4 * A_WIDTH), row(2 * B_WIDTH), row(hp), row(hp),
                 pl.BlockSpec((tm // ATT_TK, C_WIDTH, ATT_TK), lambda i: (i, 0, 0))]
    common_specs = [row(HEAD_PAD)] * 2 + [_layer_spec(w, layer) for w in weights]
    if ln is None:
        body, first_specs, first_args = _inproj_kernel, [row(D_MODEL)], [h]
    else:
        body = _inproj_ln_kernel
        first_specs = [_token_block_spec(D_MODEL), _const_spec((1, D_MODEL)), _const_spec((1, D_MODEL))]
        first_args = [_token_blocks(h), ln[0].reshape(1, D_MODEL), ln[1].reshape(1, D_MODEL)]
        out_shape.append(jax.ShapeDtypeStruct((t, D_MODEL), F32))
        out_specs.append(row(D_MODEL))
    return pl.pallas_call(
        body,
        out_shape=tuple(out_shape),
        grid=(t // tm,),
        in_specs=first_specs + common_specs,
        out_specs=tuple(out_specs),
        compiler_params=_params(("parallel",)),
        name="in_proj",
    )(*first_args, *rope_pats, *weights)


def _erf(x):
    return lax.erf(x)


def _gelu(x):
    return 0.5 * x * (1.0 + _erf(x * (2.0 ** -0.5)))


def _rows_to_array(rows, like):
    zero = jnp.zeros_like(like)
    return jnp.concatenate([zero if r is None else r for r in rows], axis=0)


def _mixer_kernel(hg_ref, sg_ref, lb_ref, ng_ref, lng_ref, lnb_ref, ws_ref, bs_ref, ones_ref, gmask_ref,
                  o_ref,
                  q_s, k_s, f_s, v_s, qt_s, kt_s, oi_s, a_s, b_s, qs_s, ks_s, state_s):
    nc, cs, w = MIX_NCHUNK, A_CHUNK, A_WIDTH
    tb = nc * cs
    slab = lambda p: slice(p * nc, (p + 1) * nc)

    @pl.when(pl.program_id(1) == 0)
    def _():
        state_s[...] = jnp.zeros_like(state_s)

    lb = lb_ref[...]
    for p in range(cs):
        aq = hg_ref[slab(p), 0:w]
        f_p = lb + (1.0 - lb) * _sigmoid(hg_ref[slab(p), w:2 * w])
        q_s[p] = aq * _sigmoid(aq)
        f_s[p] = f_p
        k_s[p] = 1.0 - f_p
        v_s[p] = hg_ref[slab(p), 2 * w:3 * w]

    ones_bd = ones_ref[...]

    kd = []
    for p in range(cs):
        f_p = f_s[p]
        q_p = q_s[p]
        kd = [k_s[p]] + [f_p * x for x in kd]
        xs = jnp.concatenate([q_p * x for x in kd], axis=0).astype(BF16)
        wgt = _dot(xs, ones_bd)
        o_p = wgt[0:nc] * v_s[p]
        for d in range(1, p + 1):
            o_p = o_p + wgt[d * nc:(d + 1) * nc] * v_s[p - d]
        oi_s[p] = o_p

    pref = f_s[0]
    qt_s[0] = q_s[0] * pref
    for p in range(1, cs):
        pref = pref * f_s[p]
        qt_s[p] = q_s[p] * pref
    chunk_decay = pref
    suf = jnp.ones_like(pref)
    kt_s[cs - 1] = k_s[cs - 1]
    for p in range(cs - 2, -1, -1):
        suf = suf * f_s[p + 1]
        kt_s[p] = k_s[p] * suf

    drow = [chunk_decay[c:c + 1, :] for c in range(nc)]
    one_row = jnp.ones_like(drow[0])
    levels = []
    m = 1
    while m < nc:
        levels.append(m)
        m *= 2
    la_arrays, lb_arrays = [], []
    for m in levels:
        la = [None] * nc
        lbr = [None] * nc
        for base in range(0, nc, 2 * m):
            mid = base + m
            la[mid] = one_row
            for i in range(mid + 1, mid + m):
                la[i] = la[i - 1] * drow[i - 1]
            lbr[mid - 1] = one_row
            for j in range(mid - 2, base - 1, -1):
                lbr[j] = lbr[j + 1] * drow[j + 1]
        la_arrays.append(_rows_to_array(la, one_row))
        lb_arrays.append(_rows_to_array(lbr, one_row))
    ep = [one_row]
    for c in range(1, nc):
        ep.append(ep[-1] * drow[c - 1])
    es = [one_row] * nc
    for c in range(nc - 2, -1, -1):
        es[c] = es[c + 1] * drow[c + 1]
    total_decay = ep[-1] * drow[nc - 1]
    ep_arr = jnp.concatenate(ep, axis=0)
    es_arr = jnp.concatenate(es, axis=0)

    for p in range(cs):
        qt_p = qt_s[p]
        kt_p = kt_s[p]
        for li in range(len(levels)):
            a_s[li, p] = qt_p * la_arrays[li]
            b_s[li, p] = kt_p * lb_arrays[li]
        qs_s[p] = qt_p * ep_arr
        ks_s[p] = kt_p * es_arr

    lane = lax.broadcasted_iota(jnp.int32, (tb, w), 1)
    head_masks = [(lane >= h * A_DK) & (lane < (h + 1) * A_DK) for h in range(A_HEADS)]
    v_tok = v_s[...].reshape(tb, w)
    v_bf = v_tok.astype(BF16)
    a_tok = [a_s[li].reshape(tb, w).astype(BF16) for li in range(len(levels))]
    b_tok = [b_s[li].reshape(tb, w).astype(BF16) for li in range(len(levels))]
    scores, v_heads = [], []
    for h in range(A_HEADS):
        hs = slice(h * A_DK, (h + 1) * A_DK)
        sc = None
        for li, m in enumerate(levels):
            s_l = _dot_nt(a_tok[li][:, hs], b_tok[li][:, hs])
            if 2 * m < nc:
                s_l = s_l * gmask_ref[li]
            sc = s_l if sc is None else sc + s_l
        scores.append(sc.astype(BF16))
        v_heads.append(jnp.where(head_masks[h], v_tok, 0.0).astype(BF16))
    o_cross = _dot(jnp.concatenate(scores, axis=1), jnp.concatenate(v_heads, axis=0))

    st = state_s[...]
    o_state = _dot_nt(qs_s[...].reshape(tb, w).astype(BF16), st.astype(BF16))
    kv = _dot_tn(v_bf, ks_s[...].reshape(tb, w).astype(BF16))
    rr = lax.broadcasted_iota(jnp.int32, (w, w), 0)
    cc = lax.broadcasted_iota(jnp.int32, (w, w), 1)
    state_s[...] = st * total_decay + jnp.where((rr // A_DK) == (cc // A_DK), kv, 0.0)

    o = oi_s[...].reshape(tb, w) + o_cross + o_state
    ms = _dot((o * o).astype(BF16), ones_bd) * (1.0 / A_DK)
    ag = hg_ref[:, 3 * w:4 * w]
    o_ref[:, 0:w] = (o * lax.rsqrt(ms + RMS_EPS) * ng_ref[...] * (ag * _sigmoid(ag))).astype(o_ref.dtype)

    cps = B_CHUNK // cs
    u = _gelu(sg_ref[:, 0:B_WIDTH])
    vv = _layer_norm(_gelu(sg_ref[:, B_WIDTH:2 * B_WIDTH]), lng_ref[...], lnb_ref[...])
    lane_b = lax.broadcasted_iota(jnp.int32, (B_CHUNK, B_WIDTH), 1)
    zs = []
    for ci in range(nc // cps):
        v_c = jnp.concatenate([vv[p * nc + ci * cps:p * nc + (ci + 1) * cps] for p in range(cs)],
                              axis=0).astype(BF16)
        v_groups = jnp.concatenate(
            [jnp.where((lane_b >= g * B_CH) & (lane_b < (g + 1) * B_CH), v_c, 0.0) for g in range(B_GROUPS)], axis=0)
        zs.append(bs_ref[...] + _dot(ws_ref[...], v_groups))
    z_all = jnp.concatenate([z[p * cps:(p + 1) * cps] for p in range(cs) for z in zs], axis=0)
    o_ref[:, w:w + B_WIDTH] = (u * z_all).astype(o_ref.dtype)


def _group_masks():
    chunk = np.arange(MIX_BLOCK) % MIX_NCHUNK
    levels = int(round(math.log2(MIX_NCHUNK))) - 1
    return np.stack([(chunk[:, None] >> (l + 1)) == (chunk[None, :] >> (l + 1)) for l in range(levels)]
                    ).astype(np.float32)


def _mixers(hg, sg, sw, layer, bsz, seq):
    nb = seq // MIX_BLOCK
    nc, cs, w = MIX_NCHUNK, A_CHUNK, A_WIDTH
    nlev = int(round(math.log2(nc)))
    blk = lambda width: pl.BlockSpec((MIX_BLOCK, width), lambda b, i: (b * nb + i, 0))
    pm = lambda: pltpu.VMEM((cs, nc, w), F32)
    ow = A_WIDTH + B_WIDTH
    return pl.pallas_call(
        _mixer_kernel,
        out_shape=jax.ShapeDtypeStruct((bsz * seq, ow), BF16),
        grid=(bsz, nb),
        in_specs=[blk(4 * A_WIDTH), blk(2 * B_WIDTH),
                  ] + [_layer_spec(sw[n], layer) for n in ("lb", "hgrn_g", "sgu_g", "sgu_b", "ws", "bs")] + [
                  _const_spec((w, w)), _const_spec((nlev - 1, MIX_BLOCK, MIX_BLOCK))],
        out_specs=blk(ow),
        scratch_shapes=[pm(), pm(), pm(), pm(), pm(), pm(), pm(),
                        pltpu.VMEM((nlev, cs, nc, w), F32), pltpu.VMEM((nlev, cs, nc, w), F32),
                        pm(), pm(), pltpu.VMEM((w, w), F32)],
        compiler_params=_params(("parallel", "arbitrary")),
        name="mixers",
    )(hg, sg, sw["lb"], sw["hgrn_g"], sw["sgu_g"], sw["sgu_b"], sw["ws"], sw["bs"], sw["ones_bd"], _group_masks())


def _token_offset(i):
    r = i % MIX_BLOCK
    return (i - r) + (r % MIX_NCHUNK) * A_CHUNK + r // MIX_NCHUNK


def _diagonal_bias():
    key = _token_offset(np.arange(ATT_TK))[:, None]
    qry = _token_offset(np.arange(ATT_TQ))[None, :]
    return np.where(key <= qry, 0.0, NEG_BIG).astype(np.float32)


def _attn_kernel(qx_ref, qy_ref, k_ref, vt_ref, bias_ref, o_ref, sa_ref, sb_ref, acc_ref):
    tq, tk = ATT_TQ, ATT_TK
    ones_rows = jnp.ones((ATT_ONES, tk), BF16)
    late = slice(tq - tk, tq)

    def produce(qs, hh, j, dst, bias=None, cols=slice(None)):
        off = pl.multiple_of(j * tk, tk)
        k_t = k_ref[pl.ds(off, tk), hh * HEAD_PAD:(hh + 1) * HEAD_PAD]
        st = _dot_nt(k_t, qs[hh][cols])
        if bias is not None:
            st = st + bias
        dst[hh, :, cols] = st
        return jnp.max(st, axis=0, keepdims=True)

    def consume(hh, j, src, tile_max, acc, m_old, cols=slice(None)):
        m_new = jnp.maximum(m_old, tile_max)
        alpha = jnp.exp2(m_old - m_new)
        pt = jnp.exp2(src[hh, :, cols] - m_new).astype(BF16)
        vt_h = jnp.concatenate([vt_ref[j, hh * C_V:(hh + 1) * C_V, :], ones_rows], axis=0)
        acc[hh, :, cols] = alpha * acc[hh, :, cols] + _dot(vt_h, pt)
        return m_new

    def overlap(qs, prod, cons):
        maxes, ms = [], []
        for hh in range(ATT_HEADS):
            if prod is not None:
                maxes.append(produce(qs, hh, *prod))
            if cons is not None:
                j, src, tile_max, acc, m_old = cons
                ms.append(consume(hh, j, src, tile_max[hh], acc, m_old[hh]))
        return tuple(maxes), tuple(ms)

    def clear(acc_a, acc_b):
        acc_a[...] = jnp.zeros(acc_a.shape, F32)
        acc_b[...] = jnp.zeros(acc_b.shape, F32)

    def steady(qi, qs, acc_a, acc_b, max_a):
        ms0 = tuple(jnp.full((1, tq), NEG_BIG, F32) for _ in range(ATT_HEADS))

        def body(p, carry):
            ms_a, ms_b, in_a, max_a = carry
            max_b, ms_a = overlap(qs, (2 * p, sb_ref), (in_a, sa_ref, max_a, acc_a, ms_a))
            max_a, ms_b = overlap(qs, (2 * p + 1, sa_ref), (2 * p, sb_ref, max_b, acc_b, ms_b))
            return ms_a, ms_b, 2 * p + 1, max_a

        return lax.fori_loop(0, qi, body, (ms0, ms0, 2 * qi, max_a))

    def drain_a(qi, qs, acc_a, state):
        ms_a, ms_b, in_a, max_a = state
        max_b, ms_a = overlap(qs, (2 * qi + 1, sb_ref, bias_ref[:, 0:tk], late),
                              (in_a, sa_ref, max_a, acc_a, ms_a))
        return ms_a, ms_b, max_b

    def drain_b(qi, acc_a, acc_b, ms_a, ms_b, max_b, out, next_block=None):
        next_max, ms_late = [], []
        for hh in range(ATT_HEADS):
            if next_block is not None:
                next_max.append(produce(next_block[1], hh, 2 * next_block[0], sa_ref, bias_ref[...]))
            ms_late.append(consume(hh, 2 * qi + 1, sb_ref, max_b[hh], acc_b, ms_b[hh][:, late], late))
        ms_b = tuple(jnp.concatenate([ms_b[hh][:, 0:tq - tk], ms_late[hh]], axis=1) for hh in range(ATT_HEADS))
        for hh in range(ATT_HEADS):
            m = jnp.maximum(ms_a[hh], ms_b[hh])
            acc = acc_a[hh] * jnp.exp2(ms_a[hh] - m) + acc_b[hh] * jnp.exp2(ms_b[hh] - m)
            out[hh * C_V:(hh + 1) * C_V, :] = (acc[0:C_V] / acc[C_V:C_V + 1]).astype(out.dtype)
        return tuple(next_max)

    i = pl.program_id(2)
    qi_x, qi_y = i, 2 * pl.num_programs(2) - 1 - i
    qs_x = [qx_ref[:, hh * HEAD_PAD:(hh + 1) * HEAD_PAD] for hh in range(ATT_HEADS)]
    qs_y = [qy_ref[:, hh * HEAD_PAD:(hh + 1) * HEAD_PAD] for hh in range(ATT_HEADS)]
    acc_x, acc_y = (acc_ref.at[0, 0], acc_ref.at[0, 1]), (acc_ref.at[1, 0], acc_ref.at[1, 1])

    clear(*acc_x)
    clear(*acc_y)
    max_a_x, _ = overlap(qs_x, (2 * qi_x, sa_ref, bias_ref[...]), None)
    state_x = steady(qi_x, qs_x, *acc_x, max_a_x)
    ms_a, ms_b, max_b = drain_a(qi_x, qs_x, acc_x[0], state_x)
    max_a_y = drain_b(qi_x, *acc_x, ms_a, ms_b, max_b, o_ref.at[0], next_block=(qi_y, qs_y))
    state_y = steady(qi_y, qs_y, *acc_y, max_a_y)
    ms_a, ms_b, max_b = drain_a(qi_y, qs_y, acc_y[0], state_y)
    drain_b(qi_y, *acc_y, ms_a, ms_b, max_b, o_ref.at[1])


def _attention(q, k, vt, bsz, seq):
    assert ATT_TQ == 2 * ATT_TK
    nh = ATT_HEADS
    nq = seq // ATT_TQ
    nk = seq // ATT_TK
    acc_shape = (2, 2, nh, C_V + ATT_ONES, ATT_TQ)
    return pl.pallas_call(
        _attn_kernel,
        out_shape=jax.ShapeDtypeStruct((bsz, 2, nq // 2, C_WIDTH, ATT_TQ), BF16),
        grid=(bsz, C_HEADS // nh, nq // 2),
        in_specs=[pl.BlockSpec((None, ATT_TQ, nh * HEAD_PAD), lambda b, g, i: (b, i, g)),
                  pl.BlockSpec((None, ATT_TQ, nh * HEAD_PAD), lambda b, g, i: (b, nq - 1 - i, g)),
                  pl.BlockSpec((None, seq, nh * HEAD_PAD), lambda b, g, i: (b, 0, g)),
                  pl.BlockSpec((nk, nh * C_V, ATT_TK), lambda b, g, i: (b, g, 0)),
                  _const_spec((ATT_TK, ATT_TQ))],
        out_specs=pl.BlockSpec((None, 2, None, nh * C_V, ATT_TQ), lambda b, g, i: (b, 0, i, g, 0)),
        scratch_shapes=[pltpu.VMEM((nh, ATT_TK, ATT_TQ), F32), pltpu.VMEM((nh, ATT_TK, ATT_TQ), F32),
                        pltpu.VMEM(acc_shape, F32)],
        compiler_params=_params(("parallel", "parallel", "arbitrary")),
        name="mla_attention",
    )(q.reshape(bsz, seq, -1), q.reshape(bsz, seq, -1), k.reshape(bsz, seq, -1), vt, _diagonal_bias())


def _attn_out_spec(seq):
    assert CH_BLOCK == ATT_TQ
    nq = seq // ATT_TQ

    def index(r):
        qb = r % nq
        return r // nq, qb // (nq // 2), jnp.where(qb < nq // 2, qb, nq - 1 - qb), 0, 0

    return pl.BlockSpec((None, None, 1, C_WIDTH, ATT_TQ), index)


def _channel_kernel(oab_ref, oct_ref, h_ref, p_ref, wab_ref, wc_ref, g1_ref, b1_ref,
                    wgu_ref, wd_ref, wpg_ref, wpp_ref, g2_ref, b2_ref, o_ref, act_s, *,
                    token_major_out):
    mix_c = jnp.concatenate([_dot_tn(oct_ref[i], wc_ref[...]) for i in range(oct_ref.shape[0])], axis=0)
    mix = _dot(oab_ref[...], wab_ref[...]) + mix_c
    h = _layer_norm(DEEPNORM_ALPHA * h_ref[...] + mix, g1_ref[...], b1_ref[...])
    hb = h.astype(BF16)
    for c in range(D_FF // FF_CHUNK):
        cols = slice(c * FF_CHUNK, (c + 1) * FF_CHUNK)
        gate = _dot(hb, wgu_ref[:, cols])
        up = _dot(hb, wgu_ref[:, D_FF + c * FF_CHUNK:D_FF + (c + 1) * FF_CHUNK])
        act_s[:, cols] = (gate * _sigmoid(gate) * up).astype(BF16)
    ffn = _dot(act_s[...], wd_ref[...])
    ple = _sigmoid(_dot(hb, wpg_ref[...])) * _dot(_read_pos_major(p_ref).astype(BF16), wpp_ref[...])
    out = _layer_norm(DEEPNORM_ALPHA * h + ffn + ple, g2_ref[...], b2_ref[...])
    if token_major_out:
        _write_token_major(o_ref, out)
    else:
        o_ref[...] = out


def _channel(o_ab, o_ct, h, p, sw, layer, seq, token_major_out):
    t = h.shape[0]
    tm = CH_BLOCK
    row = lambda w: pl.BlockSpec((tm, w), lambda i: (i, 0))
    if token_major_out:
        out_shape = jax.ShapeDtypeStruct((t // MIX_BLOCK, MIX_NCHUNK, A_CHUNK, D_MODEL), F32)
        out_spec = _token_block_spec(D_MODEL, tm)
    else:
        out_shape = jax.ShapeDtypeStruct((t, D_MODEL), F32)
        out_spec = row(D_MODEL)
    names = ("w_out_ab", "w_out_c", "ln1_g", "ln1_b", "wgu", "wd", "wpg", "wpp", "ln2_g", "ln2_b")
    return pl.pallas_call(
        functools.partial(_channel_kernel, token_major_out=token_major_out),
        out_shape=out_shape,
        grid=(t // tm,),
        in_specs=[row(A_WIDTH + B_WIDTH), _attn_out_spec(seq), row(D_MODEL),
                  _token_block_spec(PLE_DIM, tm, layer * (t // tm))] + [_layer_spec(sw[n], layer, single_buffer=True) for n in names],
        out_specs=out_spec,
        scratch_shapes=[pltpu.VMEM((tm, D_FF), BF16)],
        compiler_params=_params(("parallel",)),
        name="channel_mix",
    )(o_ab, o_ct, h, _token_blocks(p), *[sw[n] for n in names])


def _head_pad_cols(w_nope, w_rope):
    lead = w_nope.shape[:2]
    pad = jnp.zeros(lead + (C_HEADS, HEAD_PAD - C_NOPE - w_rope.shape[-1]), w_nope.dtype)
    return jnp.concatenate([w_nope, w_rope, pad], axis=-1).reshape(lead + (C_HEADS * HEAD_PAD,))


def _pos_major(x, axis):
    cps = B_CHUNK // A_CHUNK
    shape = x.shape
    x = x.reshape(shape[:axis] + (cps, A_CHUNK) + shape[axis + 1:])
    return jnp.swapaxes(x, axis, axis + 1).reshape(shape)


def _prep_weights(lower_bounds, w_in, hgrn_norm_g, sgu_ln_g, sgu_ln_b, sgu_w_s, sgu_b_s,
                  mla_q_norm_g, mla_w_uq, mla_kv_norm_g, mla_w_ukv, w_out, ln1_g, ln1_b,
                  w_gate_up, w_down, ple_w_gate, ple_w_proj, ln2_g, ln2_b):
    nl = w_in.shape[0]
    w_in_placed = jnp.pad(w_in.astype(BF16), ((0, 0), (0, 0), (0, HEAD_PAD - C_ROPE)))
    wq = mla_w_uq.reshape(nl, C_Q_RANK, C_HEADS, C_NOPE + C_ROPE)
    wq_nope, wq_rope = wq[..., :C_NOPE], wq[..., C_NOPE:]
    wkv = mla_w_ukv.reshape(nl, C_KV_RANK, C_HEADS, C_NOPE + C_V)
    wk_nope, wv = wkv[..., :C_NOPE], wkv[..., C_NOPE:]
    zeros_rope = jnp.zeros((nl, C_KV_RANK, C_HEADS, C_ROPE), F32)
    tri = jnp.tril(jnp.ones((B_CHUNK, B_CHUNK), F32))
    head_id = np.arange(A_WIDTH) // A_DK
    vec = lambda g: g.reshape(nl, 1, -1)
    return dict(
        w_in=w_in_placed,
        gq=vec(mla_q_norm_g), gkv=vec(mla_kv_norm_g),
        wq=_head_pad_cols(wq_nope, jnp.concatenate(
            [wq_rope, wq_rope[..., C_ROPE // 2:], wq_rope[..., :C_ROPE // 2]], axis=-1)).astype(BF16),
        wk=_head_pad_cols(wk_nope, zeros_rope).astype(BF16),
        wv=jnp.swapaxes(wv.reshape(nl, C_KV_RANK, C_WIDTH), 1, 2).astype(BF16),
        lb=vec(lower_bounds), hgrn_g=vec(hgrn_norm_g), sgu_g=vec(sgu_ln_g), sgu_b=vec(sgu_ln_b),
        ws=jnp.swapaxes(_pos_major(_pos_major(sgu_w_s * tri, 2), 3), 1, 2).reshape(
            nl, B_CHUNK, B_GROUPS * B_CHUNK).astype(BF16),
        bs=_pos_major(jnp.repeat(jnp.swapaxes(sgu_b_s, 1, 2), B_CH, axis=2), 1),
        ones_bd=jnp.asarray(head_id[:, None] == head_id[None, :], BF16),
        w_out_ab=w_out[:, :A_WIDTH + B_WIDTH].astype(BF16), w_out_c=w_out[:, A_WIDTH + B_WIDTH:].astype(BF16),
        ln1_g=vec(ln1_g), ln1_b=vec(ln1_b),
        wgu=w_gate_up.astype(BF16),
        wd=w_down.astype(BF16), wpg=ple_w_gate.astype(BF16), wpp=ple_w_proj.astype(BF16),
        ln2_g=vec(ln2_g), ln2_b=vec(ln2_b),
    )


def kernel(x, p, positions, ln_in_g, ln_in_b, w_in, hgrn_lb_logits, hgrn_norm_g, sgu_ln_g, sgu_ln_b, sgu_w_s, sgu_b_s, mla_q_norm_g, mla_w_uq, mla_kv_norm_g, mla_w_ukv, w_out, ln1_g, ln1_b, w_gate_up, w_down, ple_w_gate, ple_w_proj, ln2_g, ln2_b):
    bsz, seq, d = x.shape
    t = bsz * seq
    lb_cum = jnp.cumsum(jax.nn.softmax(hgrn_lb_logits.astype(F32), axis=0), axis=0)
    lower_bounds = lb_cum - lb_cum[0]
    pos_pm = positions.reshape(bsz, seq // MIX_BLOCK, MIX_NCHUNK, A_CHUNK).swapaxes(2, 3).reshape(bsz, seq)
    rope_pats = _rope_tables(pos_pm)
    sw = _prep_weights(lower_bounds, w_in, hgrn_norm_g, sgu_ln_g, sgu_ln_b, sgu_w_s, sgu_b_s,
                       mla_q_norm_g, mla_w_uq, mla_kv_norm_g, mla_w_ukv, w_out, ln1_g, ln1_b,
                       w_gate_up, w_down, ple_w_gate, ple_w_proj, ln2_g, ln2_b)

    h = x.reshape(t, d)
    for i in range(DEPTH):
        if i == 0:
            hg, sg, q, k, v, h = _in_proj(h, rope_pats, sw, i, ln=(ln_in_g, ln_in_b))
        else:
            hg, sg, q, k, v = _in_proj(h, rope_pats, sw, i)
        o_ab = _mixers(hg, sg, sw, i, bsz, seq)
        o_c = _attention(q, k, v, bsz, seq)
        h = _channel(o_ab, o_c, h, p.reshape(DEPTH * t, PLE_DIM), sw, i, seq, token_major_out=(i == DEPTH - 1))
    return h.reshape(bsz, seq, d)
```

```python
import functools
import math

import numpy as np
import jax
import jax.numpy as jnp
from jax import lax
from jax.experimental import pallas as pl
from jax.experimental.pallas import tpu as pltpu

F32 = jnp.float32
BF16 = jnp.bfloat16

D_MODEL = 1024
DEPTH = 2
PLE_DIM = 256

A_WIDTH = 256
A_DK = 64
A_HEADS = 4
A_CHUNK = 16
B_WIDTH = 256
B_CH = 64
B_GROUPS = 4
B_CHUNK = 128
C_WIDTH = 512
C_NOPE = 64
C_ROPE = 32
C_V = 64
C_HEADS = 8
C_Q_RANK = 384
C_KV_RANK = 256
ROPE_THETA = 10000.0
LANES = 128
HEAD_PAD = LANES

D_FF = 2816
LN_EPS = 1e-5
RMS_EPS = 1e-6
DEEPNORM_ALPHA = (2 * DEPTH) ** 0.25
ATT_SCALE = (C_NOPE + C_ROPE) ** -0.5
Q_SCALE = ATT_SCALE * math.log2(math.e)

MIX_BLOCK = 256
MIX_NCHUNK = MIX_BLOCK // A_CHUNK
ATT_TQ = 512
ATT_TK = 256
ATT_HEADS = 8
ATT_ONES = 16
ROW_BLOCK = 1024
CH_BLOCK = 512
FF_CHUNK = 256
VMEM_LIMIT = 56 * 1024 * 1024
NEG_BIG = -1e30


def _dot(a, b):
    return jnp.dot(a, b, preferred_element_type=F32)


def _dot_nt(a, b):
    return lax.dot_general(a, b, (((1,), (1,)), ((), ())), preferred_element_type=F32)


def _dot_tn(a, b):
    return lax.dot_general(a, b, (((0,), (0,)), ((), ())), preferred_element_type=F32)


def _layer_norm(x, g, b):
    mu = jnp.mean(x, axis=-1, keepdims=True)
    xc = x - mu
    var = jnp.mean(xc * xc, axis=-1, keepdims=True)
    return xc * lax.rsqrt(var + LN_EPS) * g + b


def _rms_norm(x, g):
    return x * lax.rsqrt(jnp.mean(x * x, axis=-1, keepdims=True) + RMS_EPS) * g


def _sigmoid(x):
    return 1.0 / (1.0 + jnp.exp(-x))


def _const_spec(shape):
    nd = len(shape)
    return pl.BlockSpec(shape, lambda *_: (0,) * nd)


def _layer_spec(arr, layer, single_buffer=False):
    shape = arr.shape[1:]
    nd = len(shape)
    kw = dict(pipeline_mode=pl.Buffered(1)) if single_buffer else {}
    return pl.BlockSpec((None,) + shape, lambda *_: (layer,) + (0,) * nd, **kw)


def _params(sem):
    return pltpu.CompilerParams(dimension_semantics=sem, vmem_limit_bytes=VMEM_LIMIT)


def _rope_table_kernel(pos_ref, freq_ref, cos_ref, sin_ref, c_s, s_s):
    half = C_ROPE // 2
    g = pl.program_id(0)

    @pl.when(g == 0)
    def _():
        ang = pos_ref[...] * freq_ref[...]
        c_s[...] = jnp.cos(ang)
        s_s[...] = jnp.sin(ang)

    lane = lax.broadcasted_iota(jnp.int32, c_s.shape, 1)
    on_x1 = (lane >= C_NOPE) & (lane < C_NOPE + half)
    on_x2 = (lane >= C_NOPE + half) & (lane < C_NOPE + C_ROPE)
    to_x1 = (C_NOPE - g * half) % LANES
    to_x2 = (C_NOPE + half - g * half) % LANES
    c, s = c_s[...], s_s[...]
    nope = jnp.where(lane < C_NOPE, 1.0, 0.0)
    cos_ref[...] = jnp.where(on_x1, pltpu.roll(c, to_x1, axis=1), jnp.where(on_x2, pltpu.roll(c, to_x2, axis=1), nope))
    sin_ref[...] = jnp.where(on_x1, -pltpu.roll(s, to_x1, axis=1), jnp.where(on_x2, pltpu.roll(s, to_x2, axis=1), 0.0))


def _rope_tables(positions):
    t = positions.size
    half = C_ROPE // 2
    groups = LANES // half
    rows = t // groups
    inv_freq = ROPE_THETA ** (-jnp.arange(0, C_ROPE, 2, dtype=F32) / C_ROPE)
    pos_rep = jnp.repeat(positions.astype(F32).reshape(groups, rows).T, half, axis=1)
    freq = jnp.tile(inv_freq, groups).reshape(1, LANES)
    table = jax.ShapeDtypeStruct((t, HEAD_PAD), F32)
    out_spec = pl.BlockSpec((rows, HEAD_PAD), lambda g: (g, 0))
    return pl.pallas_call(
        _rope_table_kernel,
        out_shape=(table,) * 2,
        grid=(groups,),
        in_specs=[_const_spec((rows, LANES)), _const_spec((1, LANES))],
        out_specs=(out_spec,) * 2,
        scratch_shapes=[pltpu.VMEM((rows, LANES), F32), pltpu.VMEM((rows, LANES), F32)],
        compiler_params=_params(("arbitrary",)),
        name="rope_tables",
    )(pos_rep, freq)


def _read_pos_major(ref):
    return jnp.concatenate([ref[blk, :, p, :] for blk in range(ref.shape[0]) for p in range(A_CHUNK)], axis=0)


def _write_token_major(ref, val):
    for blk in range(ref.shape[0]):
        for p in range(A_CHUNK):
            r0 = blk * MIX_BLOCK + p * MIX_NCHUNK
            ref[blk, :, p, :] = val[r0:r0 + MIX_NCHUNK]


def _token_blocks(x2d):
    t, wd = x2d.shape
    return x2d.reshape(t // MIX_BLOCK, MIX_NCHUNK, A_CHUNK, wd)


def _token_block_spec(wd, rows=ROW_BLOCK, first_block=0):
    nblk = rows // MIX_BLOCK
    return pl.BlockSpec((nblk, MIX_NCHUNK, A_CHUNK, wd), lambda i: (first_block + i, 0, 0, 0))


def _rope_lanes(x, cos, sin):
    half = C_ROPE // 2
    lane = lax.broadcasted_iota(jnp.int32, x.shape, 1)
    right = pltpu.roll(x, HEAD_PAD - half, axis=1)
    left = pltpu.roll(x, half, axis=1)
    return x * cos + jnp.where(lane < C_NOPE + half, right, left) * sin


def _inproj_ln_kernel(x_ref, lng_ref, lnb_ref, *rest):
    *rest, h_out = rest
    h = _layer_norm(_read_pos_major(x_ref), lng_ref[...], lnb_ref[...])
    h_out[...] = h
    _inproj_body(h, *rest)


def _inproj_kernel(h_ref, *rest):
    _inproj_body(h_ref[...], *rest)


def _inproj_body(h, cos_ref, sin_ref, w_in,
                 gq_ref, wq_ref, gkv_ref, wk_ref, wv_ref,
                 hg_out, sg_out, q_out, k_out, v_out):
    hb = h.astype(BF16)
    o_sg, o_lat = 4 * A_WIDTH, 4 * A_WIDTH + 2 * B_WIDTH
    proj = _dot(hb, w_in[...])
    hg_out[...] = proj[:, 0:o_sg]
    sg_out[...] = proj[:, o_sg:o_lat]
    lat = proj[:, o_lat:]
    cos, sin = cos_ref[...], sin_ref[...]

    q_table = (cos + pltpu.roll(sin, C_ROPE, axis=1)) * Q_SCALE
    cqn = _rms_norm(lat[:, 0:C_Q_RANK], gq_ref[...]).astype(BF16)
    q_lin = _dot(cqn, wq_ref[...])
    for h in range(C_HEADS):
        sl = slice(h * HEAD_PAD, (h + 1) * HEAD_PAD)
        q_out[:, sl] = (q_lin[:, sl] * q_table).astype(BF16)

    ckvn = _rms_norm(lat[:, C_Q_RANK:C_Q_RANK + C_KV_RANK], gkv_ref[...]).astype(BF16)
    k_lin = _dot(ckvn, wk_ref[...])
    vt = _dot_nt(wv_ref[...], ckvn).astype(BF16)
    for c in range(v_out.shape[0]):
        v_out[c] = vt[:, c * ATT_TK:(c + 1) * ATT_TK]
    k_raw = pltpu.roll(lat[:, C_Q_RANK + C_KV_RANK:], C_NOPE, axis=1)
    k_rope = _rope_lanes(k_raw, cos, sin)
    k_rope = k_rope + pltpu.roll(k_rope, C_ROPE, axis=1)
    for h in range(C_HEADS):
        sl = slice(h * HEAD_PAD, (h + 1) * HEAD_PAD)
        k_out[:, sl] = (k_lin[:, sl] + k_rope).astype(BF16)


def _in_proj(h, rope_pats, sw, layer, ln=None):
    t = h.shape[0]
    tm = ROW_BLOCK
    row = lambda w: pl.BlockSpec((tm, w), lambda i: (i, 0))
    weights = [sw[n] for n in ("w_in", "gq", "wq", "gkv", "wk", "wv")]
    hp = C_HEADS * HEAD_PAD
    out_shape = [jax.ShapeDtypeStruct((t, 4 * A_WIDTH), F32),
                 jax.ShapeDtypeStruct((t, 2 * B_WIDTH), F32),
                 jax.ShapeDtypeStruct((t, hp), BF16),
                 jax.ShapeDtypeStruct((t, hp), BF16),
                 jax.ShapeDtypeStruct((t // ATT_TK, C_WIDTH, ATT_TK), BF16)]
    out_specs = [row(4 * A_WIDTH), row(2 * B_WIDTH), row(hp), row(hp),
                 pl.BlockSpec((tm // ATT_TK, C_WIDTH, ATT_TK), lambda i: (i, 0, 0))]
    common_specs = [row(HEAD_PAD)] * 2 + [_layer_spec(w, layer) for w in weights]
    if ln is None:
        body, first_specs, first_args = _inproj_kernel, [row(D_MODEL)], [h]
    else:
        body = _inproj_ln_kernel
        first_specs = [_token_block_spec(D_MODEL), _const_spec((1, D_MODEL)), _const_spec((1, D_MODEL))]
        first_args = [_token_blocks(h), ln[0].reshape(1, D_MODEL), ln[1].reshape(1, D_MODEL)]
        out_shape.append(jax.ShapeDtypeStruct((t, D_MODEL), F32))
        out_specs.append(row(D_MODEL))
    return pl.pallas_call(
        body,
        out_shape=tuple(out_shape),
        grid=(t // tm,),
        in_specs=first_specs + common_specs,
        out_specs=tuple(out_specs),
        compiler_params=_params(("parallel",)),
        name="in_proj",
    )(*first_args, *rope_pats, *weights)


def _erf(x):
    return lax.erf(x)


def _gelu(x):
    return 0.5 * x * (1.0 + _erf(x * (2.0 ** -0.5)))


def _rows_to_array(rows, like):
    zero = jnp.zeros_like(like)
    return jnp.concatenate([zero if r is None else r for r in rows], axis=0)


def _mixer_kernel(hg_ref, sg_ref, lb_ref, ng_ref, lng_ref, lnb_ref, ws_ref, bs_ref, ones_ref, gmask_ref,
                  o_ref,
                  q_s, k_s, f_s, v_s, qt_s, kt_s, oi_s, a_s, b_s, qs_s, ks_s, state_s):
    nc, cs, w = MIX_NCHUNK, A_CHUNK, A_WIDTH
    tb = nc * cs
    slab = lambda p: slice(p * nc, (p + 1) * nc)

    @pl.when(pl.program_id(1) == 0)
    def _():
        state_s[...] = jnp.zeros_like(state_s)

    lb = lb_ref[...]
    for p in range(cs):
        aq = hg_ref[slab(p), 0:w]
        f_p = lb + (1.0 - lb) * _sigmoid(hg_ref[slab(p), w:2 * w])
        q_s[p] = aq * _sigmoid(aq)
        f_s[p] = f_p
        k_s[p] = 1.0 - f_p
        v_s[p] = hg_ref[slab(p), 2 * w:3 * w]

    ones_bd = ones_ref[...]

    kd = []
    for p in range(cs):
        f_p = f_s[p]
        q_p = q_s[p]
        kd = [k_s[p]] + [f_p * x for x in kd]
        xs = jnp.concatenate([q_p * x for x in kd], axis=0).astype(BF16)
        wgt = _dot(xs, ones_bd)
        o_p = wgt[0:nc] * v_s[p]
        for d in range(1, p + 1):
            o_p = o_p + wgt[d * nc:(d + 1) * nc] * v_s[p - d]
        oi_s[p] = o_p

    pref = f_s[0]
    qt_s[0] = q_s[0] * pref
    for p in range(1, cs):
        pref = pref * f_s[p]
        qt_s[p] = q_s[p] * pref
    chunk_decay = pref
    suf = jnp.ones_like(pref)
    kt_s[cs - 1] = k_s[cs - 1]
    for p in range(cs - 2, -1, -1):
        suf = suf * f_s[p + 1]
        kt_s[p] = k_s[p] * suf

    drow = [chunk_decay[c:c + 1, :] for c in range(nc)]
    one_row = jnp.ones_like(drow[0])
    levels = []
    m = 1
    while m < nc:
        levels.append(m)
        m *= 2
    la_arrays, lb_arrays = [], []
    for m in levels:
        la = [None] * nc
        lbr = [None] * nc
        for base in range(0, nc, 2 * m):
            mid = base + m
            la[mid] = one_row
            for i in range(mid + 1, mid + m):
                la[i] = la[i - 1] * drow[i - 1]
            lbr[mid - 1] = one_row
            for j in range(mid - 2, base - 1, -1):
                lbr[j] = lbr[j + 1] * drow[j + 1]
        la_arrays.append(_rows_to_array(la, one_row))
        lb_arrays.append(_rows_to_array(lbr, one_row))
    ep = [one_row]
    for c in range(1, nc):
        ep.append(ep[-1] * drow[c - 1])
    es = [one_row] * nc
    for c in range(nc - 2, -1, -1):
        es[c] = es[c + 1] * drow[c + 1]
    total_decay = ep[-1] * drow[nc - 1]
    ep_arr = jnp.concatenate(ep, axis=0)
    es_arr = jnp.concatenate(es, axis=0)

    for p in range(cs):
        qt_p = qt_s[p]
        kt_p = kt_s[p]
        for li in range(len(levels)):
            a_s[li, p] = qt_p * la_arrays[li]
            b_s[li, p] = kt_p * lb_arrays[li]
        qs_s[p] = qt_p * ep_arr
        ks_s[p] = kt_p * es_arr

    lane = lax.broadcasted_iota(jnp.int32, (tb, w), 1)
    head_masks = [(lane >= h * A_DK) & (lane < (h + 1) * A_DK) for h in range(A_HEADS)]
    v_tok = v_s[...].reshape(tb, w)
    v_bf = v_tok.astype(BF16)
    a_tok = [a_s[li].reshape(tb, w).astype(BF16) for li in range(len(levels))]
    b_tok = [b_s[li].reshape(tb, w).astype(BF16) for li in range(len(levels))]
    scores, v_heads = [], []
    for h in range(A_HEADS):
        hs = slice(h * A_DK, (h + 1) * A_DK)
        sc = None
        for li, m in enumerate(levels):
            s_l = _dot_nt(a_tok[li][:, hs], b_tok[li][:, hs])
            if 2 * m < nc:
                s_l = s_l * gmask_ref[li]
            sc = s_l if sc is None else sc + s_l
        scores.append(sc.astype(BF16))
        v_heads.append(jnp.where(head_masks[h], v_tok, 0.0).astype(BF16))
    o_cross = _dot(jnp.concatenate(scores, axis=1), jnp.concatenate(v_heads, axis=0))

    st = state_s[...]
    o_state = _dot_nt(qs_s[...].reshape(tb, w).astype(BF16), st.astype(BF16))
    kv = _dot_tn(v_bf, ks_s[...].reshape(tb, w).astype(BF16))
    rr = lax.broadcasted_iota(jnp.int32, (w, w), 0)
    cc = lax.broadcasted_iota(jnp.int32, (w, w), 1)
    state_s[...] = st * total_decay + jnp.where((rr // A_DK) == (cc // A_DK), kv, 0.0)

    o = oi_s[...].reshape(tb, w) + o_cross + o_state
    ms = _dot((o * o).astype(BF16), ones_bd) * (1.0 / A_DK)
    ag = hg_ref[:, 3 * w:4 * w]
    o_ref[:, 0:w] = (o * lax.rsqrt(ms + RMS_EPS) * ng_ref[...] * (ag * _sigmoid(ag))).astype(o_ref.dtype)

    cps = B_CHUNK // cs
    u = _gelu(sg_ref[:, 0:B_WIDTH])
    vv = _layer_norm(_gelu(sg_ref[:, B_WIDTH:2 * B_WIDTH]), lng_ref[...], lnb_ref[...])
    lane_b = lax.broadcasted_iota(jnp.int32, (B_CHUNK, B_WIDTH), 1)
    zs = []
    for ci in range(nc // cps):
        v_c = jnp.concatenate([vv[p * nc + ci * cps:p * nc + (ci + 1) * cps] for p in range(cs)],
                              axis=0).astype(BF16)
        v_groups = jnp.concatenate(
            [jnp.where((lane_b >= g * B_CH) & (lane_b < (g + 1) * B_CH), v_c, 0.0) for g in range(B_GROUPS)], axis=0)
        zs.append(bs_ref[...] + _dot(ws_ref[...], v_groups))
    z_all = jnp.concatenate([z[p * cps:(p + 1) * cps] for p in range(cs) for z in zs], axis=0)
    o_ref[:, w:w + B_WIDTH] = (u * z_all).astype(o_ref.dtype)


def _group_masks():
    chunk = np.arange(MIX_BLOCK) % MIX_NCHUNK
    levels = int(round(math.log2(MIX_NCHUNK))) - 1
    return np.stack([(chunk[:, None] >> (l + 1)) == (chunk[None, :] >> (l + 1)) for l in range(levels)]
                    ).astype(np.float32)


def _mixers(hg, sg, sw, layer, bsz, seq):
    nb = seq // MIX_BLOCK
    nc, cs, w = MIX_NCHUNK, A_CHUNK, A_WIDTH
    nlev = int(round(math.log2(nc)))
    blk = lambda width: pl.BlockSpec((MIX_BLOCK, width), lambda b, i: (b * nb + i, 0))
    pm = lambda: pltpu.VMEM((cs, nc, w), F32)
    ow = A_WIDTH + B_WIDTH
    return pl.pallas_call(
        _mixer_kernel,
        out_shape=jax.ShapeDtypeStruct((bsz * seq, ow), BF16),
        grid=(bsz, nb),
        in_specs=[blk(4 * A_WIDTH), blk(2 * B_WIDTH),
                  ] + [_layer_spec(sw[n], layer) for n in ("lb", "hgrn_g", "sgu_g", "sgu_b", "ws", "bs")] + [
                  _const_spec((w, w)), _const_spec((nlev - 1, MIX_BLOCK, MIX_BLOCK))],
        out_specs=blk(ow),
        scratch_shapes=[pm(), pm(), pm(), pm(), pm(), pm(), pm(),
                        pltpu.VMEM((nlev, cs, nc, w), F32), pltpu.VMEM((nlev, cs, nc, w), F32),
                        pm(), pm(), pltpu.VMEM((w, w), F32)],
        compiler_params=_params(("parallel", "arbitrary")),
        name="mixers",
    )(hg, sg, sw["lb"], sw["hgrn_g"], sw["sgu_g"], sw["sgu_b"], sw["ws"], sw["bs"], sw["ones_bd"], _group_masks())


def _token_offset(i):
    r = i % MIX_BLOCK
    return (i - r) + (r % MIX_NCHUNK) * A_CHUNK + r // MIX_NCHUNK


def _diagonal_bias():
    key = _token_offset(np.arange(ATT_TK))[:, None]
    qry = _token_offset(np.arange(ATT_TQ))[None, :]
    return np.where(key <= qry, 0.0, NEG_BIG).astype(np.float32)


def _attn_kernel(qx_ref, qy_ref, k_ref, vt_ref, bias_ref, o_ref, sa_ref, sb_ref, acc_ref):
    tq, tk = ATT_TQ, ATT_TK
    ones_rows = jnp.ones((ATT_ONES, tk), BF16)
    late = slice(tq - tk, tq)

    def produce(qs, hh, j, dst, bias=None, cols=slice(None)):
        off = pl.multiple_of(j * tk, tk)
        k_t = k_ref[pl.ds(off, tk), hh * HEAD_PAD:(hh + 1) * HEAD_PAD]
        st = _dot_nt(k_t, qs[hh][cols])
        if bias is not None:
            st = st + bias
        dst[hh, :, cols] = st
        return jnp.max(st, axis=0, keepdims=True)

    def consume(hh, j, src, tile_max, acc, m_old, cols=slice(None)):
        m_new = jnp.maximum(m_old, tile_max)
        alpha = jnp.exp2(m_old - m_new)
        pt = jnp.exp2(src[hh, :, cols] - m_new).astype(BF16)
        vt_h = jnp.concatenate([vt_ref[j, hh * C_V:(hh + 1) * C_V, :], ones_rows], axis=0)
        acc[hh, :, cols] = alpha * acc[hh, :, cols] + _dot(vt_h, pt)
        return m_new

    def overlap(qs, prod, cons):
        maxes, ms = [], []
        for hh in range(ATT_HEADS):
            if prod is not None:
                maxes.append(produce(qs, hh, *prod))
            if cons is not None:
                j, src, tile_max, acc, m_old = cons
                ms.append(consume(hh, j, src, tile_max[hh], acc, m_old[hh]))
        return tuple(maxes), tuple(ms)

    def clear(acc_a, acc_b):
        acc_a[...] = jnp.zeros(acc_a.shape, F32)
        acc_b[...] = jnp.zeros(acc_b.shape, F32)

    def steady(qi, qs, acc_a, acc_b, max_a):
        ms0 = tuple(jnp.full((1, tq), NEG_BIG, F32) for _ in range(ATT_HEADS))

        def body(p, carry):
            ms_a, ms_b, in_a, max_a = carry
            max_b, ms_a = overlap(qs, (2 * p, sb_ref), (in_a, sa_ref, max_a, acc_a, ms_a))
            max_a, ms_b = overlap(qs, (2 * p + 1, sa_ref), (2 * p, sb_ref, max_b, acc_b, ms_b))
            return ms_a, ms_b, 2 * p + 1, max_a

        return lax.fori_loop(0, qi, body, (ms0, ms0, 2 * qi, max_a))

    def drain_a(qi, qs, acc_a, state):
        ms_a, ms_b, in_a, max_a = state
        max_b, ms_a = overlap(qs, (2 * qi + 1, sb_ref, bias_ref[:, 0:tk], late),
                              (in_a, sa_ref, max_a, acc_a, ms_a))
        return ms_a, ms_b, max_b

    def drain_b(qi, acc_a, acc_b, ms_a, ms_b, max_b, out, next_block=None):
        next_max, ms_late = [], []
        for hh in range(ATT_HEADS):
            if next_block is not None:
                next_max.append(produce(next_block[1], hh, 2 * next_block[0], sa_ref, bias_ref[...]))
            ms_late.append(consume(hh, 2 * qi + 1, sb_ref, max_b[hh], acc_b, ms_b[hh][:, late], late))
        ms_b = tuple(jnp.concatenate([ms_b[hh][:, 0:tq - tk], ms_late[hh]], axis=1) for hh in range(ATT_HEADS))
        for hh in range(ATT_HEADS):
            m = jnp.maximum(ms_a[hh], ms_b[hh])
            acc = acc_a[hh] * jnp.exp2(ms_a[hh] - m) + acc_b[hh] * jnp.exp2(ms_b[hh] - m)
            out[hh * C_V:(hh + 1) * C_V, :] = (acc[0:C_V] / acc[C_V:C_V + 1]).astype(out.dtype)
        return tuple(next_max)

    i = pl.program_id(2)
    qi_x, qi_y = i, 2 * pl.num_programs(2) - 1 - i
    qs_x = [qx_ref[:, hh * HEAD_PAD:(hh + 1) * HEAD_PAD] for hh in range(ATT_HEADS)]
    qs_y = [qy_ref[:, hh * HEAD_PAD:(hh + 1) * HEAD_PAD] for hh in range(ATT_HEADS)]
    acc_x, acc_y = (acc_ref.at[0, 0], acc_ref.at[0, 1]), (acc_ref.at[1, 0], acc_ref.at[1, 1])

    clear(*acc_x)
    clear(*acc_y)
    max_a_x, _ = overlap(qs_x, (2 * qi_x, sa_ref, bias_ref[...]), None)
    state_x = steady(qi_x, qs_x, *acc_x, max_a_x)
    ms_a, ms_b, max_b = drain_a(qi_x, qs_x, acc_x[0], state_x)
    max_a_y = drain_b(qi_x, *acc_x, ms_a, ms_b, max_b, o_ref.at[0], next_block=(qi_y, qs_y))
    state_y = steady(qi_y, qs_y, *acc_y, max_a_y)
    ms_a, ms_b, max_b = drain_a(qi_y, qs_y, acc_y[0], state_y)
    drain_b(qi_y, *acc_y, ms_a, ms_b, max_b, o_ref.at[1])


def _attention(q, k, vt, bsz, seq):
    assert ATT_TQ == 2 * ATT_TK
    nh = ATT_HEADS
    nq = seq // ATT_TQ
    nk = seq // ATT_TK
    acc_shape = (2, 2, nh, C_V + ATT_ONES, ATT_TQ)
    return pl.pallas_call(
        _attn_kernel,
        out_shape=jax.ShapeDtypeStruct((bsz, 2, nq // 2, C_WIDTH, ATT_TQ), BF16),
        grid=(bsz, C_HEADS // nh, nq // 2),
        in_specs=[pl.BlockSpec((None, ATT_TQ, nh * HEAD_PAD), lambda b, g, i: (b, i, g)),
                  pl.BlockSpec((None, ATT_TQ, nh * HEAD_PAD), lambda b, g, i: (b, nq - 1 - i, g)),
                  pl.BlockSpec((None, seq, nh * HEAD_PAD), lambda b, g, i: (b, 0, g)),
                  pl.BlockSpec((nk, nh * C_V, ATT_TK), lambda b, g, i: (b, g, 0)),
                  _const_spec((ATT_TK, ATT_TQ))],
        out_specs=pl.BlockSpec((None, 2, None, nh * C_V, ATT_TQ), lambda b, g, i: (b, 0, i, g, 0)),
        scratch_shapes=[pltpu.VMEM((nh, ATT_TK, ATT_TQ), F32), pltpu.VMEM((nh, ATT_TK, ATT_TQ), F32),
                        pltpu.VMEM(acc_shape, F32)],
        compiler_params=_params(("parallel", "parallel", "arbitrary")),
        name="mla_attention",
    )(q.reshape(bsz, seq, -1), q.reshape(bsz, seq, -1), k.reshape(bsz, seq, -1), vt, _diagonal_bias())


def _attn_out_spec(seq):
    assert CH_BLOCK == ATT_TQ
    nq = seq // ATT_TQ

    def index(r):
        qb = r % nq
        return r // nq, qb // (nq // 2), jnp.where(qb < nq // 2, qb, nq - 1 - qb), 0, 0

    return pl.BlockSpec((None, None, 1, C_WIDTH, ATT_TQ), index)


def _channel_kernel(oab_ref, oct_ref, h_ref, p_ref, wab_ref, wc_ref, g1_ref, b1_ref,
                    wgu_ref, wd_ref, wpg_ref, wpp_ref, g2_ref, b2_ref, o_ref, act_s, *,
                    token_major_out):
    mix_c = jnp.concatenate([_dot_tn(oct_ref[i], wc_ref[...]) for i in range(oct_ref.shape[0])], axis=0)
    mix = _dot(oab_ref[...], wab_ref[...]) + mix_c
    h = _layer_norm(DEEPNORM_ALPHA * h_ref[...] + mix, g1_ref[...], b1_ref[...])
    hb = h.astype(BF16)
    ple = _sigmoid(_dot(hb, wpg_ref[...])) * _dot(_read_pos_major(p_ref).astype(BF16), wpp_ref[...])
    for c in range(D_FF // FF_CHUNK):
        cols = slice(c * FF_CHUNK, (c + 1) * FF_CHUNK)
        gate = _dot(hb, wgu_ref[:, cols])
        up = _dot(hb, wgu_ref[:, D_FF + c * FF_CHUNK:D_FF + (c + 1) * FF_CHUNK])
        act_s[:, cols] = (gate * _sigmoid(gate) * up).astype(BF16)
    ffn = _dot(act_s[...], wd_ref[...])
    out = _layer_norm(DEEPNORM_ALPHA * h + ffn + ple, g2_ref[...], b2_ref[...])
    if token_major_out:
        _write_token_major(o_ref, out)
    else:
        o_ref[...] = out


def _channel(o_ab, o_ct, h, p, sw, layer, seq, token_major_out):
    t = h.shape[0]
    tm = CH_BLOCK
    row = lambda w: pl.BlockSpec((tm, w), lambda i: (i, 0))
    if token_major_out:
        out_shape = jax.ShapeDtypeStruct((t // MIX_BLOCK, MIX_NCHUNK, A_CHUNK, D_MODEL), F32)
        out_spec = _token_block_spec(D_MODEL, tm)
    else:
        out_shape = jax.ShapeDtypeStruct((t, D_MODEL), F32)
        out_spec = row(D_MODEL)
    names = ("w_out_ab", "w_out_c", "ln1_g", "ln1_b", "wgu", "wd", "wpg", "wpp", "ln2_g", "ln2_b")
    return pl.pallas_call(
        functools.partial(_channel_kernel, token_major_out=token_major_out),
        out_shape=out_shape,
        grid=(t // tm,),
        in_specs=[row(A_WIDTH + B_WIDTH), _attn_out_spec(seq), row(D_MODEL),
                  _token_block_spec(PLE_DIM, tm, layer * (t // tm))] + [_layer_spec(sw[n], layer, single_buffer=True) for n in names],
        out_specs=out_spec,
        scratch_shapes=[pltpu.VMEM((tm, D_FF), BF16)],
        compiler_params=_params(("parallel",)),
        name="channel_mix",
    )(o_ab, o_ct, h, _token_blocks(p), *[sw[n] for n in names])


def _head_pad_cols(w_nope, w_rope):
    lead = w_nope.shape[:2]
    pad = jnp.zeros(lead + (C_HEADS, HEAD_PAD - C_NOPE - w_rope.shape[-1]), w_nope.dtype)
    return jnp.concatenate([w_nope, w_rope, pad], axis=-1).reshape(lead + (C_HEADS * HEAD_PAD,))


def _pos_major(x, axis):
    cps = B_CHUNK // A_CHUNK
    shape = x.shape
    x = x.reshape(shape[:axis] + (cps, A_CHUNK) + shape[axis + 1:])
    return jnp.swapaxes(x, axis, axis + 1).reshape(shape)


def _prep_weights(lower_bounds, w_in, hgrn_norm_g, sgu_ln_g, sgu_ln_b, sgu_w_s, sgu_b_s,
                  mla_q_norm_g, mla_w_uq, mla_kv_norm_g, mla_w_ukv, w_out, ln1_g, ln1_b,
                  w_gate_up, w_down, ple_w_gate, ple_w_proj, ln2_g, ln2_b):
    nl = w_in.shape[0]
    w_in_placed = jnp.pad(w_in.astype(BF16), ((0, 0), (0, 0), (0, HEAD_PAD - C_ROPE)))
    wq = mla_w_uq.reshape(nl, C_Q_RANK, C_HEADS, C_NOPE + C_ROPE)
    wq_nope, wq_rope = wq[..., :C_NOPE], wq[..., C_NOPE:]
    wkv = mla_w_ukv.reshape(nl, C_KV_RANK, C_HEADS, C_NOPE + C_V)
    wk_nope, wv = wkv[..., :C_NOPE], wkv[..., C_NOPE:]
    zeros_rope = jnp.zeros((nl, C_KV_RANK, C_HEADS, C_ROPE), F32)
    tri = jnp.tril(jnp.ones((B_CHUNK, B_CHUNK), F32))
    head_id = np.arange(A_WIDTH) // A_DK
    vec = lambda g: g.reshape(nl, 1, -1)
    return dict(
        w_in=w_in_placed,
        gq=vec(mla_q_norm_g), gkv=vec(mla_kv_norm_g),
        wq=_head_pad_cols(wq_nope, jnp.concatenate(
            [wq_rope, wq_rope[..., C_ROPE // 2:], wq_rope[..., :C_ROPE // 2]], axis=-1)).astype(BF16),
        wk=_head_pad_cols(wk_nope, zeros_rope).astype(BF16),
        wv=jnp.swapaxes(wv.reshape(nl, C_KV_RANK, C_WIDTH), 1, 2).astype(BF16),
        lb=vec(lower_bounds), hgrn_g=vec(hgrn_norm_g), sgu_g=vec(sgu_ln_g), sgu_b=vec(sgu_ln_b),
        ws=jnp.swapaxes(_pos_major(_pos_major(sgu_w_s * tri, 2), 3), 1, 2).reshape(
            nl, B_CHUNK, B_GROUPS * B_CHUNK).astype(BF16),
        bs=_pos_major(jnp.repeat(jnp.swapaxes(sgu_b_s, 1, 2), B_CH, axis=2), 1),
        ones_bd=jnp.asarray(head_id[:, None] == head_id[None, :], BF16),
        w_out_ab=w_out[:, :A_WIDTH + B_WIDTH].astype(BF16), w_out_c=w_out[:, A_WIDTH + B_WIDTH:].astype(BF16),
        ln1_g=vec(ln1_g), ln1_b=vec(ln1_b),
        wgu=w_gate_up.astype(BF16),
        wd=w_down.astype(BF16), wpg=ple_w_gate.astype(BF16), wpp=ple_w_proj.astype(BF16),
        ln2_g=vec(ln2_g), ln2_b=vec(ln2_b),
    )


def kernel(x, p, positions, ln_in_g, ln_in_b, w_in, hgrn_lb_logits, hgrn_norm_g, sgu_ln_g, sgu_ln_b, sgu_w_s, sgu_b_s, mla_q_norm_g, mla_w_uq, mla_kv_norm_g, mla_w_ukv, w_out, ln1_g, ln1_b, w_gate_up, w_down, ple_w_gate, ple_w_proj, ln2_g, ln2_b):
    bsz, seq, d = x.shape
    t = bsz * seq
    lb_cum = jnp.cumsum(jax.nn.softmax(hgrn_lb_logits.astype(F32), axis=0), axis=0)
    lower_bounds = lb_cum - lb_cum[0]
    pos_pm = positions.reshape(bsz, seq // MIX_BLOCK, MIX_NCHUNK, A_CHUNK).swapaxes(2, 3).reshape(bsz, seq)
    rope_pats = _rope_tables(pos_pm)
    sw = _prep_weights(lower_bounds, w_in, hgrn_norm_g, sgu_ln_g, sgu_ln_b, sgu_w_s, sgu_b_s,
                       mla_q_norm_g, mla_w_uq, mla_kv_norm_g, mla_w_ukv, w_out, ln1_g, ln1_b,
                       w_gate_up, w_down, ple_w_gate, ple_w_proj, ln2_g, ln2_b)

    h = x.reshape(t, d)
    for i in range(DEPTH):
        if i == 0:
            hg, sg, q, k, v, h = _in_proj(h, rope_pats, sw, i, ln=(ln_in_g, ln_in_b))
        else:
            hg, sg, q, k, v = _in_proj(h, rope_pats, sw, i)
        o_ab = _mixers(hg, sg, sw, i, bsz, seq)
        o_c = _attention(q, k, v, bsz, seq)
        h = _channel(o_ab, o_c, h, p.reshape(DEPTH * t, PLE_DIM), sw, i, seq, token_major_out=(i == DEPTH - 1))
    return h.reshape(bsz, seq, d)
```

```python
import functools
import math

import numpy as np
import jax
import jax.numpy as jnp
from jax import lax
from jax.experimental import pallas as pl
from jax.experimental.pallas import tpu as pltpu

F32 = jnp.float32
BF16 = jnp.bfloat16

D_MODEL = 1024
DEPTH = 2
PLE_DIM = 256

A_WIDTH = 256
A_DK = 64
A_HEADS = 4
A_CHUNK = 16
B_WIDTH = 256
B_CH = 64
B_GROUPS = 4
B_CHUNK = 128
C_WIDTH = 512
C_NOPE = 64
C_ROPE = 32
C_V = 64
C_HEADS = 8
C_Q_RANK = 384
C_KV_RANK = 256
ROPE_THETA = 10000.0
LANES = 128
HEAD_PAD = LANES

D_FF = 2816
LN_EPS = 1e-5
RMS_EPS = 1e-6
DEEPNORM_ALPHA = (2 * DEPTH) ** 0.25
ATT_SCALE = (C_NOPE + C_ROPE) ** -0.5
Q_SCALE = ATT_SCALE * math.log2(math.e)

MIX_BLOCK = 256
MIX_NCHUNK = MIX_BLOCK // A_CHUNK
ATT_TQ = 512
ATT_TK = 256
ATT_HEADS = 8
ATT_ONES = 16
ROW_BLOCK = 1024
CH_BLOCK = 512
FF_CHUNK = 256
VMEM_LIMIT = 56 * 1024 * 1024
NEG_BIG = -1e30


def _dot(a, b):
    return jnp.dot(a, b, preferred_element_type=F32)


def _dot_nt(a, b):
    return lax.dot_general(a, b, (((1,), (1,)), ((), ())), preferred_element_type=F32)


def _dot_tn(a, b):
    return lax.dot_general(a, b, (((0,), (0,)), ((), ())), preferred_element_type=F32)


def _layer_norm(x, g, b):
    mu = jnp.mean(x, axis=-1, keepdims=True)
    xc = x - mu
    var = jnp.mean(xc * xc, axis=-1, keepdims=True)
    return xc * lax.rsqrt(var + LN_EPS) * g + b


def _rms_norm(x, g):
    return x * lax.rsqrt(jnp.mean(x * x, axis=-1, keepdims=True) + RMS_EPS) * g


def _sigmoid(x):
    return 1.0 / (1.0 + jnp.exp(-x))


def _const_spec(shape):
    nd = len(shape)
    return pl.BlockSpec(shape, lambda *_: (0,) * nd)


def _layer_spec(arr, layer, single_buffer=False):
    shape = arr.shape[1:]
    nd = len(shape)
    kw = dict(pipeline_mode=pl.Buffered(1)) if single_buffer else {}
    return pl.BlockSpec((None,) + shape, lambda *_: (layer,) + (0,) * nd, **kw)


def _params(sem):
    return pltpu.CompilerParams(dimension_semantics=sem, vmem_limit_bytes=VMEM_LIMIT)


def _rope_table_kernel(pos_ref, freq_ref, cos_ref, sin_ref, c_s, s_s):
    half = C_ROPE // 2
    g = pl.program_id(0)

    @pl.when(g == 0)
    def _():
        ang = pos_ref[...] * freq_ref[...]
        c_s[...] = jnp.cos(ang)
        s_s[...] = jnp.sin(ang)

    lane = lax.broadcasted_iota(jnp.int32, c_s.shape, 1)
    on_x1 = (lane >= C_NOPE) & (lane < C_NOPE + half)
    on_x2 = (lane >= C_NOPE + half) & (lane < C_NOPE + C_ROPE)
    to_x1 = (C_NOPE - g * half) % LANES
    to_x2 = (C_NOPE + half - g * half) % LANES
    c, s = c_s[...], s_s[...]
    nope = jnp.where(lane < C_NOPE, 1.0, 0.0)
    cos_ref[...] = jnp.where(on_x1, pltpu.roll(c, to_x1, axis=1), jnp.where(on_x2, pltpu.roll(c, to_x2, axis=1), nope))
    sin_ref[...] = jnp.where(on_x1, -pltpu.roll(s, to_x1, axis=1), jnp.where(on_x2, pltpu.roll(s, to_x2, axis=1), 0.0))


def _rope_tables(positions):
    t = positions.size
    half = C_ROPE // 2
    groups = LANES // half
    rows = t // groups
    inv_freq = ROPE_THETA ** (-jnp.arange(0, C_ROPE, 2, dtype=F32) / C_ROPE)
    pos_rep = jnp.repeat(positions.astype(F32).reshape(groups, rows).T, half, axis=1)
    freq = jnp.tile(inv_freq, groups).reshape(1, LANES)
    table = jax.ShapeDtypeStruct((t, HEAD_PAD), F32)
    out_spec = pl.BlockSpec((rows, HEAD_PAD), lambda g: (g, 0))
    return pl.pallas_call(
        _rope_table_kernel,
        out_shape=(table,) * 2,
        grid=(groups,),
        in_specs=[_const_spec((rows, LANES)), _const_spec((1, LANES))],
        out_specs=(out_spec,) * 2,
        scratch_shapes=[pltpu.VMEM((rows, LANES), F32), pltpu.VMEM((rows, LANES), F32)],
        compiler_params=_params(("arbitrary",)),
        name="rope_tables",
    )(pos_rep, freq)


def _read_pos_major(ref):
    return jnp.concatenate([ref[blk, :, p, :] for blk in range(ref.shape[0]) for p in range(A_CHUNK)], axis=0)


def _write_token_major(ref, val):
    for blk in range(ref.shape[0]):
        for p in range(A_CHUNK):
            r0 = blk * MIX_BLOCK + p * MIX_NCHUNK
            ref[blk, :, p, :] = val[r0:r0 + MIX_NCHUNK]


def _token_blocks(x2d):
    t, wd = x2d.shape
    return x2d.reshape(t // MIX_BLOCK, MIX_NCHUNK, A_CHUNK, wd)


def _token_block_spec(wd, rows=ROW_BLOCK, first_block=0):
    nblk = rows // MIX_BLOCK
    return pl.BlockSpec((nblk, MIX_NCHUNK, A_CHUNK, wd), lambda i: (first_block + i, 0, 0, 0))


def _rope_lanes(x, cos, sin):
    half = C_ROPE // 2
    lane = lax.broadcasted_iota(jnp.int32, x.shape, 1)
    right = pltpu.roll(x, HEAD_PAD - half, axis=1)
    left = pltpu.roll(x, half, axis=1)
    return x * cos + jnp.where(lane < C_NOPE + half, right, left) * sin


def _inproj_ln_kernel(x_ref, lng_ref, lnb_ref, *rest):
    *rest, h_out = rest
    h = _layer_norm(_read_pos_major(x_ref), lng_ref[...], lnb_ref[...])
    h_out[...] = h
    _inproj_body(h, *rest)


def _inproj_kernel(h_ref, *rest):
    _inproj_body(h_ref[...], *rest)


def _inproj_body(h, cos_ref, sin_ref, w_in,
                 gq_ref, wq_ref, gkv_ref, wk_ref, wv_ref,
                 hg_out, sg_out, q_out, k_out, v_out):
    hb = h.astype(BF16)
    o_sg, o_lat = 4 * A_WIDTH, 4 * A_WIDTH + 2 * B_WIDTH
    proj = _dot(hb, w_in[...])
    hg_out[...] = proj[:, 0:o_sg]
    sg_out[...] = proj[:, o_sg:o_lat]
    lat = proj[:, o_lat:]
    cos, sin = cos_ref[...], sin_ref[...]

    q_table = (cos + pltpu.roll(sin, C_ROPE, axis=1)) * Q_SCALE
    cqn = _rms_norm(lat[:, 0:C_Q_RANK], gq_ref[...]).astype(BF16)
    q_lin = _dot(cqn, wq_ref[...])
    for h in range(C_HEADS):
        sl = slice(h * HEAD_PAD, (h + 1) * HEAD_PAD)
        q_out[:, sl] = (q_lin[:, sl] * q_table).astype(BF16)

    ckvn = _rms_norm(lat[:, C_Q_RANK:C_Q_RANK + C_KV_RANK], gkv_ref[...]).astype(BF16)
    k_lin = _dot(ckvn, wk_ref[...])
    vt = _dot_nt(wv_ref[...], ckvn).astype(BF16)
    for c in range(v_out.shape[0]):
        v_out[c] = vt[:, c * ATT_TK:(c + 1) * ATT_TK]
    k_raw = pltpu.roll(lat[:, C_Q_RANK + C_KV_RANK:], C_NOPE, axis=1)
    k_rope = _rope_lanes(k_raw, cos, sin)
    k_rope = k_rope + pltpu.roll(k_rope, C_ROPE, axis=1)
    for h in range(C_HEADS):
        sl = slice(h * HEAD_PAD, (h + 1) * HEAD_PAD)
        k_out[:, sl] = (k_lin[:, sl] + k_rope).astype(BF16)


def _in_proj(h, rope_pats, sw, layer, ln=None):
    t = h.shape[0]
    tm = ROW_BLOCK
    row = lambda w: pl.BlockSpec((tm, w), lambda i: (i, 0))
    weights = [sw[n] for n in ("w_in", "gq", "wq", "gkv", "wk", "wv")]
    hp = C_HEADS * HEAD_PAD
    out_shape = [jax.ShapeDtypeStruct((t, 4 * A_WIDTH), F32),
                 jax.ShapeDtypeStruct((t, 2 * B_WIDTH), F32),
                 jax.ShapeDtypeStruct((t, hp), BF16),
                 jax.ShapeDtypeStruct((t, hp), BF16),
                 jax.ShapeDtypeStruct((t // ATT_TK, C_WIDTH, ATT_TK), BF16)]
    out_specs = [row(4 * A_WIDTH), row(2 * B_WIDTH), row(hp), row(hp),
                 pl.BlockSpec((tm // ATT_TK, C_WIDTH, ATT_TK), lambda i: (i, 0, 0))]
    common_specs = [row(HEAD_PAD)] * 2 + [_layer_spec(w, layer) for w in weights]
    if ln is None:
        body, first_specs, first_args = _inproj_kernel, [row(D_MODEL)], [h]
    else:
        body = _inproj_ln_kernel
        first_specs = [_token_block_spec(D_MODEL), _const_spec((1, D_MODEL)), _const_spec((1, D_MODEL))]
        first_args = [_token_blocks(h), ln[0].reshape(1, D_MODEL), ln[1].reshape(1, D_MODEL)]
        out_shape.append(jax.ShapeDtypeStruct((t, D_MODEL), F32))
        out_specs.append(row(D_MODEL))
    return pl.pallas_call(
        body,
        out_shape=tuple(out_shape),
        grid=(t // tm,),
        in_specs=first_specs + common_specs,
        out_specs=tuple(out_specs),
        compiler_params=_params(("parallel",)),
        name="in_proj",
    )(*first_args, *rope_pats, *weights)


def _erf(x):
    return lax.erf(x)


def _gelu(x):
    return 0.5 * x * (1.0 + _erf(x * (2.0 ** -0.5)))


def _rows_to_array(rows, like):
    zero = jnp.zeros_like(like)
    return jnp.concatenate([zero if r is None else r for r in rows], axis=0)


def _mixer_kernel(hg_ref, sg_ref, lb_ref, ng_ref, lng_ref, lnb_ref, ws_ref, bs_ref, ones_ref, gmask_ref,
                  o_ref,
                  q_s, k_s, f_s, v_s, qt_s, kt_s, oi_s, a_s, b_s, qs_s, ks_s, state_s):
    nc, cs, w = MIX_NCHUNK, A_CHUNK, A_WIDTH
    tb = nc * cs
    slab = lambda p: slice(p * nc, (p + 1) * nc)

    @pl.when(pl.program_id(1) == 0)
    def _():
        state_s[...] = jnp.zeros_like(state_s)

    lb = lb_ref[...]
    for p in range(cs):
        aq = hg_ref[slab(p), 0:w]
        f_p = lb + (1.0 - lb) * _sigmoid(hg_ref[slab(p), w:2 * w])
        q_s[p] = aq * _sigmoid(aq)
        f_s[p] = f_p
        k_s[p] = 1.0 - f_p
        v_s[p] = hg_ref[slab(p), 2 * w:3 * w]

    ones_bd = ones_ref[...]

    kd = []
    for p in range(cs):
        f_p = f_s[p]
        q_p = q_s[p]
        kd = [k_s[p]] + [f_p * x for x in kd]
        xs = jnp.concatenate([q_p * x for x in kd], axis=0).astype(BF16)
        wgt = _dot(xs, ones_bd)
        o_p = wgt[0:nc] * v_s[p]
        for d in range(1, p + 1):
            o_p = o_p + wgt[d * nc:(d + 1) * nc] * v_s[p - d]
        oi_s[p] = o_p

    pref = f_s[0]
    qt_s[0] = q_s[0] * pref
    for p in range(1, cs):
        pref = pref * f_s[p]
        qt_s[p] = q_s[p] * pref
    chunk_decay = pref
    suf = jnp.ones_like(pref)
    kt_s[cs - 1] = k_s[cs - 1]
    for p in range(cs - 2, -1, -1):
        suf = suf * f_s[p + 1]
        kt_s[p] = k_s[p] * suf

    drow = [chunk_decay[c:c + 1, :] for c in range(nc)]
    one_row = jnp.ones_like(drow[0])
    levels = []
    m = 1
    while m < nc:
        levels.append(m)
        m *= 2
    la_arrays, lb_arrays = [], []
    for m in levels:
        la = [None] * nc
        lbr = [None] * nc
        for base in range(0, nc, 2 * m):
            mid = base + m
            la[mid] = one_row
            for i in range(mid + 1, mid + m):
                la[i] = la[i - 1] * drow[i - 1]
            lbr[mid - 1] = one_row
            for j in range(mid - 2, base - 1, -1):
                lbr[j] = lbr[j + 1] * drow[j + 1]
        la_arrays.append(_rows_to_array(la, one_row))
        lb_arrays.append(_rows_to_array(lbr, one_row))
    ep = [one_row]
    for c in range(1, nc):
        ep.append(ep[-1] * drow[c - 1])
    es = [one_row] * nc
    for c in range(nc - 2, -1, -1):
        es[c] = es[c + 1] * drow[c + 1]
    total_decay = ep[-1] * drow[nc - 1]
    ep_arr = jnp.concatenate(ep, axis=0)
    es_arr = jnp.concatenate(es, axis=0)

    for p in range(cs):
        qt_p = qt_s[p]
        kt_p = kt_s[p]
        for li in range(len(levels)):
            a_s[li, p] = qt_p * la_arrays[li]
            b_s[li, p] = kt_p * lb_arrays[li]
        qs_s[p] = qt_p * ep_arr
        ks_s[p] = kt_p * es_arr

    lane = lax.broadcasted_iota(jnp.int32, (tb, w), 1)
    head_masks = [(lane >= h * A_DK) & (lane < (h + 1) * A_DK) for h in range(A_HEADS)]
    v_tok = v_s[...].reshape(tb, w)
    v_bf = v_tok.astype(BF16)
    a_tok = [a_s[li].reshape(tb, w).astype(BF16) for li in range(len(levels))]
    b_tok = [b_s[li].reshape(tb, w).astype(BF16) for li in range(len(levels))]
    scores, v_heads = [], []
    for h in range(A_HEADS):
        hs = slice(h * A_DK, (h + 1) * A_DK)
        sc = None
        for li, m in enumerate(levels):
            s_l = _dot_nt(a_tok[li][:, hs], b_tok[li][:, hs])
            if 2 * m < nc:
                s_l = s_l * gmask_ref[li]
            sc = s_l if sc is None else sc + s_l
        scores.append(sc.astype(BF16))
        v_heads.append(jnp.where(head_masks[h], v_tok, 0.0).astype(BF16))
    o_cross = _dot(jnp.concatenate(scores, axis=1), jnp.concatenate(v_heads, axis=0))

    st = state_s[...]
    o_state = _dot_nt(qs_s[...].reshape(tb, w).astype(BF16), st.astype(BF16))
    kv = _dot_tn(v_bf, ks_s[...].reshape(tb, w).astype(BF16))
    rr = lax.broadcasted_iota(jnp.int32, (w, w), 0)
    cc = lax.broadcasted_iota(jnp.int32, (w, w), 1)
    state_s[...] = st * total_decay + jnp.where((rr // A_DK) == (cc // A_DK), kv, 0.0)

    o = oi_s[...].reshape(tb, w) + o_cross + o_state
    ms = _dot((o * o).astype(BF16), ones_bd) * (1.0 / A_DK)
    ag = hg_ref[:, 3 * w:4 * w]
    o_ref[:, 0:w] = (o * lax.rsqrt(ms + RMS_EPS) * ng_ref[...] * (ag * _sigmoid(ag))).astype(o_ref.dtype)

    cps = B_CHUNK // cs
    u = _gelu(sg_ref[:, 0:B_WIDTH])
    vv = _layer_norm(_gelu(sg_ref[:, B_WIDTH:2 * B_WIDTH]), lng_ref[...], lnb_ref[...])
    lane_b = lax.broadcasted_iota(jnp.int32, (B_CHUNK, B_WIDTH), 1)
    zs = []
    for ci in range(nc // cps):
        v_c = jnp.concatenate([vv[p * nc + ci * cps:p * nc + (ci + 1) * cps] for p in range(cs)],
                              axis=0).astype(BF16)
        v_groups = jnp.concatenate(
            [jnp.where((lane_b >= g * B_CH) & (lane_b < (g + 1) * B_CH), v_c, 0.0) for g in range(B_GROUPS)], axis=0)
        zs.append(bs_ref[...] + _dot(ws_ref[...], v_groups))
    z_all = jnp.concatenate([z[p * cps:(p + 1) * cps] for p in range(cs) for z in zs], axis=0)
    o_ref[:, w:w + B_WIDTH] = (u * z_all).astype(o_ref.dtype)


def _group_masks():
    chunk = np.arange(MIX_BLOCK) % MIX_NCHUNK
    levels = int(round(math.log2(MIX_NCHUNK))) - 1
    return np.stack([(chunk[:, None] >> (l + 1)) == (chunk[None, :] >> (l + 1)) for l in range(levels)]
                    ).astype(np.float32)


def _mixers(hg, sg, sw, layer, bsz, seq):
    nb = seq // MIX_BLOCK
    nc, cs, w = MIX_NCHUNK, A_CHUNK, A_WIDTH
    nlev = int(round(math.log2(nc)))
    blk = lambda width: pl.BlockSpec((MIX_BLOCK, width), lambda b, i: (b * nb + i, 0))
    pm = lambda: pltpu.VMEM((cs, nc, w), F32)
    ow = A_WIDTH + B_WIDTH
    return pl.pallas_call(
        _mixer_kernel,
        out_shape=jax.ShapeDtypeStruct((bsz * seq, ow), BF16),
        grid=(bsz, nb),
        in_specs=[blk(4 * A_WIDTH), blk(2 * B_WIDTH),
                  ] + [_layer_spec(sw[n], layer) for n in ("lb", "hgrn_g", "sgu_g", "sgu_b", "ws", "bs")] + [
                  _const_spec((w, w)), _const_spec((nlev - 1, MIX_BLOCK, MIX_BLOCK))],
        out_specs=blk(ow),
        scratch_shapes=[pm(), pm(), pm(), pm(), pm(), pm(), pm(),
                        pltpu.VMEM((nlev, cs, nc, w), F32), pltpu.VMEM((nlev, cs, nc, w), F32),
                        pm(), pm(), pltpu.VMEM((w, w), F32)],
        compiler_params=_params(("parallel", "arbitrary")),
        name="mixers",
    )(hg, sg, sw["lb"], sw["hgrn_g"], sw["sgu_g"], sw["sgu_b"], sw["ws"], sw["bs"], sw["ones_bd"], _group_masks())


def _token_offset(i):
    r = i % MIX_BLOCK
    return (i - r) + (r % MIX_NCHUNK) * A_CHUNK + r // MIX_NCHUNK


def _diagonal_bias():
    key = _token_offset(np.arange(ATT_TK))[:, None]
    qry = _token_offset(np.arange(ATT_TQ))[None, :]
    return np.where(key <= qry, 0.0, NEG_BIG).astype(np.float32)


def _attn_kernel(qx_ref, qy_ref, k_ref, vt_ref, bias_ref, o_ref, sa_ref, sb_ref, acc_ref):
    tq, tk = ATT_TQ, ATT_TK
    ones_rows = jnp.ones((ATT_ONES, tk), BF16)
    late = slice(tq - tk, tq)

    def produce(qs, hh, j, dst, bias=None, cols=slice(None)):
        off = pl.multiple_of(j * tk, tk)
        k_t = k_ref[pl.ds(off, tk), hh * HEAD_PAD:(hh + 1) * HEAD_PAD]
        st = _dot_nt(k_t, qs[hh][cols])
        if bias is not None:
            st = st + bias
        dst[hh, :, cols] = st
        return jnp.max(st, axis=0, keepdims=True)

    def consume(hh, j, src, tile_max, acc, m_old, cols=slice(None)):
        m_new = jnp.maximum(m_old, tile_max)
        alpha = jnp.exp2(m_old - m_new)
        pt = jnp.exp2(src[hh, :, cols] - m_new).astype(BF16)
        vt_h = jnp.concatenate([vt_ref[j, hh * C_V:(hh + 1) * C_V, :], ones_rows], axis=0)
        acc[hh, :, cols] = alpha * acc[hh, :, cols] + _dot(vt_h, pt)
        return m_new

    def overlap(qs, prod, cons):
        maxes, ms = [], []
        for hh in range(ATT_HEADS):
            if prod is not None:
                maxes.append(produce(qs, hh, *prod))
            if cons is not None:
                j, src, tile_max, acc, m_old = cons
                ms.append(consume(hh, j, src, tile_max[hh], acc, m_old[hh]))
        return tuple(maxes), tuple(ms)

    def clear(acc_a, acc_b):
        acc_a[...] = jnp.zeros(acc_a.shape, F32)
        acc_b[...] = jnp.zeros(acc_b.shape, F32)

    def steady(qi, qs, acc_a, acc_b, max_a):
        ms0 = tuple(jnp.full((1, tq), NEG_BIG, F32) for _ in range(ATT_HEADS))

        def body(p, carry):
            ms_a, ms_b, in_a, max_a = carry
            max_b, ms_a = overlap(qs, (2 * p, sb_ref), (in_a, sa_ref, max_a, acc_a, ms_a))
            max_a, ms_b = overlap(qs, (2 * p + 1, sa_ref), (2 * p, sb_ref, max_b, acc_b, ms_b))
            return ms_a, ms_b, 2 * p + 1, max_a

        return lax.fori_loop(0, qi, body, (ms0, ms0, 2 * qi, max_a))

    def drain_a(qi, qs, acc_a, state):
        ms_a, ms_b, in_a, max_a = state
        max_b, ms_a = overlap(qs, (2 * qi + 1, sb_ref, bias_ref[:, 0:tk], late),
                              (in_a, sa_ref, max_a, acc_a, ms_a))
        return ms_a, ms_b, max_b

    def drain_b(qi, acc_a, acc_b, ms_a, ms_b, max_b, out, next_block=None):
        next_max, ms_late = [], []
        for hh in range(ATT_HEADS):
            if next_block is not None:
                next_max.append(produce(next_block[1], hh, 2 * next_block[0], sa_ref, bias_ref[...]))
            ms_late.append(consume(hh, 2 * qi + 1, sb_ref, max_b[hh], acc_b, ms_b[hh][:, late], late))
        ms_b = tuple(jnp.concatenate([ms_b[hh][:, 0:tq - tk], ms_late[hh]], axis=1) for hh in range(ATT_HEADS))
        for hh in range(ATT_HEADS):
            m = jnp.maximum(ms_a[hh], ms_b[hh])
            acc = acc_a[hh] * jnp.exp2(ms_a[hh] - m) + acc_b[hh] * jnp.exp2(ms_b[hh] - m)
            out[hh * C_V:(hh + 1) * C_V, :] = (acc[0:C_V] / acc[C_V:C_V + 1]).astype(out.dtype)
        return tuple(next_max)

    i = pl.program_id(2)
    qi_x, qi_y = i, 2 * pl.num_programs(2) - 1 - i
    qs_x = [qx_ref[:, hh * HEAD_PAD:(hh + 1) * HEAD_PAD] for hh in range(ATT_HEADS)]
    qs_y = [qy_ref[:, hh * HEAD_PAD:(hh + 1) * HEAD_PAD] for hh in range(ATT_HEADS)]
    acc_x, acc_y = (acc_ref.at[0, 0], acc_ref.at[0, 1]), (acc_ref.at[1, 0], acc_ref.at[1, 1])

    clear(*acc_x)
    clear(*acc_y)
    max_a_x, _ = overlap(qs_x, (2 * qi_x, sa_ref, bias_ref[...]), None)
    state_x = steady(qi_x, qs_x, *acc_x, max_a_x)
    ms_a, ms_b, max_b = drain_a(qi_x, qs_x, acc_x[0], state_x)
    max_a_y = drain_b(qi_x, *acc_x, ms_a, ms_b, max_b, o_ref.at[0], next_block=(qi_y, qs_y))
    state_y = steady(qi_y, qs_y, *acc_y, max_a_y)
    ms_a, ms_b, max_b = drain_a(qi_y, qs_y, acc_y[0], state_y)
    drain_b(qi_y, *acc_y, ms_a, ms_b, max_b, o_ref.at[1])


def _attention(q, k, vt, bsz, seq):
    assert ATT_TQ == 2 * ATT_TK
    nh = ATT_HEADS
    nq = seq // ATT_TQ
    nk = seq // ATT_TK
    acc_shape = (2, 2, nh, C_V + ATT_ONES, ATT_TQ)
    return pl.pallas_call(
        _attn_kernel,
        out_shape=jax.ShapeDtypeStruct((bsz, 2, nq // 2, C_WIDTH, ATT_TQ), BF16),
        grid=(bsz, C_HEADS // nh, nq // 2),
        in_specs=[pl.BlockSpec((None, ATT_TQ, nh * HEAD_PAD), lambda b, g, i: (b, i, g)),
                  pl.BlockSpec((None, ATT_TQ, nh * HEAD_PAD), lambda b, g, i: (b, nq - 1 - i, g)),
                  pl.BlockSpec((None, seq, nh * HEAD_PAD), lambda b, g, i: (b, 0, g)),
                  pl.BlockSpec((nk, nh * C_V, ATT_TK), lambda b, g, i: (b, g, 0)),
                  _const_spec((ATT_TK, ATT_TQ))],
        out_specs=pl.BlockSpec((None, 2, None, nh * C_V, ATT_TQ), lambda b, g, i: (b, 0, i, g, 0)),
        scratch_shapes=[pltpu.VMEM((nh, ATT_TK, ATT_TQ), F32), pltpu.VMEM((nh, ATT_TK, ATT_TQ), F32),
                        pltpu.VMEM(acc_shape, F32)],
        compiler_params=_params(("parallel", "parallel", "arbitrary")),
        name="mla_attention",
    )(q.reshape(bsz, seq, -1), q.reshape(bsz, seq, -1), k.reshape(bsz, seq, -1), vt, _diagonal_bias())


def _attn_out_spec(seq):
    assert CH_BLOCK == ATT_TQ
    nq = seq // ATT_TQ

    def index(r):
        qb = r % nq
        return r // nq, qb // (nq // 2), jnp.where(qb < nq // 2, qb, nq - 1 - qb), 0, 0

    return pl.BlockSpec((None, None, 1, C_WIDTH, ATT_TQ), index)


def _channel_kernel(oab_ref, oct_ref, h_ref, p_ref, wab_ref, wc_ref, g1_ref, b1_ref,
                    wgu_ref, wd_ref, wpg_ref, wpp_ref, g2_ref, b2_ref, o_ref, act_s, *,
                    token_major_out):
    p_proj = _dot(_read_pos_major(p_ref).astype(BF16), wpp_ref[...])
    mix_c = jnp.concatenate([_dot_tn(oct_ref[i], wc_ref[...]) for i in range(oct_ref.shape[0])], axis=0)
    mix = _dot(oab_ref[...], wab_ref[...]) + mix_c
    h = _layer_norm(DEEPNORM_ALPHA * h_ref[...] + mix, g1_ref[...], b1_ref[...])
    hb = h.astype(BF16)
    ple = _sigmoid(_dot(hb, wpg_ref[...])) * p_proj
    for c in range(D_FF // FF_CHUNK):
        cols = slice(c * FF_CHUNK, (c + 1) * FF_CHUNK)
        gate = _dot(hb, wgu_ref[:, cols])
        up = _dot(hb, wgu_ref[:, D_FF + c * FF_CHUNK:D_FF + (c + 1) * FF_CHUNK])
        act_s[:, cols] = (gate * _sigmoid(gate) * up).astype(BF16)
    ffn = _dot(act_s[...], wd_ref[...])
    out = _layer_norm(DEEPNORM_ALPHA * h + ffn + ple, g2_ref[...], b2_ref[...])
    if token_major_out:
        _write_token_major(o_ref, out)
    else:
        o_ref[...] = out


def _channel(o_ab, o_ct, h, p, sw, layer, seq, token_major_out):
    t = h.shape[0]
    tm = CH_BLOCK
    row = lambda w: pl.BlockSpec((tm, w), lambda i: (i, 0))
    if token_major_out:
        out_shape = jax.ShapeDtypeStruct((t // MIX_BLOCK, MIX_NCHUNK, A_CHUNK, D_MODEL), F32)
        out_spec = _token_block_spec(D_MODEL, tm)
    else:
        out_shape = jax.ShapeDtypeStruct((t, D_MODEL), F32)
        out_spec = row(D_MODEL)
    names = ("w_out_ab", "w_out_c", "ln1_g", "ln1_b", "wgu", "wd", "wpg", "wpp", "ln2_g", "ln2_b")
    return pl.pallas_call(
        functools.partial(_channel_kernel, token_major_out=token_major_out),
        out_shape=out_shape,
        grid=(t // tm,),
        in_specs=[row(A_WIDTH + B_WIDTH), _attn_out_spec(seq), row(D_MODEL),
                  _token_block_spec(PLE_DIM, tm, layer * (t // tm))] + [_layer_spec(sw[n], layer, single_buffer=True) for n in names],
        out_specs=out_spec,
        scratch_shapes=[pltpu.VMEM((tm, D_FF), BF16)],
        compiler_params=_params(("parallel",)),
        name="channel_mix",
    )(o_ab, o_ct, h, _token_blocks(p), *[sw[n] for n in names])


def _head_pad_cols(w_nope, w_rope):
    lead = w_nope.shape[:2]
    pad = jnp.zeros(lead + (C_HEADS, HEAD_PAD - C_NOPE - w_rope.shape[-1]), w_nope.dtype)
    return jnp.concatenate([w_nope, w_rope, pad], axis=-1).reshape(lead + (C_HEADS * HEAD_PAD,))


def _pos_major(x, axis):
    cps = B_CHUNK // A_CHUNK
    shape = x.shape
    x = x.reshape(shape[:axis] + (cps, A_CHUNK) + shape[axis + 1:])
    return jnp.swapaxes(x, axis, axis + 1).reshape(shape)


def _prep_weights(lower_bounds, w_in, hgrn_norm_g, sgu_ln_g, sgu_ln_b, sgu_w_s, sgu_b_s,
                  mla_q_norm_g, mla_w_uq, mla_kv_norm_g, mla_w_ukv, w_out, ln1_g, ln1_b,
                  w_gate_up, w_down, ple_w_gate, ple_w_proj, ln2_g, ln2_b):
    nl = w_in.shape[0]
    w_in_placed = jnp.pad(w_in.astype(BF16), ((0, 0), (0, 0), (0, HEAD_PAD - C_ROPE)))
    wq = mla_w_uq.reshape(nl, C_Q_RANK, C_HEADS, C_NOPE + C_ROPE)
    wq_nope, wq_rope = wq[..., :C_NOPE], wq[..., C_NOPE:]
    wkv = mla_w_ukv.reshape(nl, C_KV_RANK, C_HEADS, C_NOPE + C_V)
    wk_nope, wv = wkv[..., :C_NOPE], wkv[..., C_NOPE:]
    zeros_rope = jnp.zeros((nl, C_KV_RANK, C_HEADS, C_ROPE), F32)
    tri = jnp.tril(jnp.ones((B_CHUNK, B_CHUNK), F32))
    head_id = np.arange(A_WIDTH) // A_DK
    vec = lambda g: g.reshape(nl, 1, -1)
    return dict(
        w_in=w_in_placed,
        gq=vec(mla_q_norm_g), gkv=vec(mla_kv_norm_g),
        wq=_head_pad_cols(wq_nope, jnp.concatenate(
            [wq_rope, wq_rope[..., C_ROPE // 2:], wq_rope[..., :C_ROPE // 2]], axis=-1)).astype(BF16),
        wk=_head_pad_cols(wk_nope, zeros_rope).astype(BF16),
        wv=jnp.swapaxes(wv.reshape(nl, C_KV_RANK, C_WIDTH), 1, 2).astype(BF16),
        lb=vec(lower_bounds), hgrn_g=vec(hgrn_norm_g), sgu_g=vec(sgu_ln_g), sgu_b=vec(sgu_ln_b),
        ws=jnp.swapaxes(_pos_major(_pos_major(sgu_w_s * tri, 2), 3), 1, 2).reshape(
            nl, B_CHUNK, B_GROUPS * B_CHUNK).astype(BF16),
        bs=_pos_major(jnp.repeat(jnp.swapaxes(sgu_b_s, 1, 2), B_CH, axis=2), 1),
        ones_bd=jnp.asarray(head_id[:, None] == head_id[None, :], BF16),
        w_out_ab=w_out[:, :A_WIDTH + B_WIDTH].astype(BF16), w_out_c=w_out[:, A_WIDTH + B_WIDTH:].astype(BF16),
        ln1_g=vec(ln1_g), ln1_b=vec(ln1_b),
        wgu=w_gate_up.astype(BF16),
        wd=w_down.astype(BF16), wpg=ple_w_gate.astype(BF16), wpp=ple_w_proj.astype(BF16),
        ln2_g=vec(ln2_g), ln2_b=vec(ln2_b),
    )


def kernel(x, p, positions, ln_in_g, ln_in_b, w_in, hgrn_lb_logits, hgrn_norm_g, sgu_ln_g, sgu_ln_b, sgu_w_s, sgu_b_s, mla_q_norm_g, mla_w_uq, mla_kv_norm_g, mla_w_ukv, w_out, ln1_g, ln1_b, w_gate_up, w_down, ple_w_gate, ple_w_proj, ln2_g, ln2_b):
    bsz, seq, d = x.shape
    t = bsz * seq
    lb_cum = jnp.cumsum(jax.nn.softmax(hgrn_lb_logits.astype(F32), axis=0), axis=0)
    lower_bounds = lb_cum - lb_cum[0]
    pos_pm = positions.reshape(bsz, seq // MIX_BLOCK, MIX_NCHUNK, A_CHUNK).swapaxes(2, 3).reshape(bsz, seq)
    rope_pats = _rope_tables(pos_pm)
    sw = _prep_weights(lower_bounds, w_in, hgrn_norm_g, sgu_ln_g, sgu_ln_b, sgu_w_s, sgu_b_s,
                       mla_q_norm_g, mla_w_uq, mla_kv_norm_g, mla_w_ukv, w_out, ln1_g, ln1_b,
                       w_gate_up, w_down, ple_w_gate, ple_w_proj, ln2_g, ln2_b)

    h = x.reshape(t, d)
    for i in range(DEPTH):
        if i == 0:
            hg, sg, q, k, v, h = _in_proj(h, rope_pats, sw, i, ln=(ln_in_g, ln_in_b))
        else:
            hg, sg, q, k, v = _in_proj(h, rope_pats, sw, i)
        o_ab = _mixers(hg, sg, sw, i, bsz, seq)
        o_c = _attention(q, k, v, bsz, seq)
        h = _channel(o_ab, o_c, h, p.reshape(DEPTH * t, PLE_DIM), sw, i, seq, token_major_out=(i == DEPTH - 1))
    return h.reshape(bsz, seq, d)
```
